```python
import math
import jax, jax.numpy as jnp
from jax import lax
import numpy as np

D_MODEL = 4096
BATCH = 1
SEQ = 8192
DEPTH = 1

POOL_GROUPS = 4
POOL_GROUP_WIDTH = 512
POOL_WIDTH = POOL_GROUPS * POOL_GROUP_WIDTH
POOL_WINDOWS = (2, 4, 8, 16)
N_HEADS = 16
N_KV_GROUPS = 4
HEADS_PER_GROUP = N_HEADS // N_KV_GROUPS
HEAD_DIM = 128
Q_WIDTH = N_HEADS * HEAD_DIM
KV_WIDTH = N_KV_GROUPS * HEAD_DIM
CMP_BLOCK = 32
CMP_STRIDE = 16
SLC_BLOCK = 64
SLC_TOPK = 16
WINDOW = 512
Q_BLOCK = 128
FORCE_SCORE = 1e4
PEER_HEADS = 8
PEER_NKEYS = 128
PEER_N_EXPERTS = PEER_NKEYS * PEER_NKEYS
PEER_QDIM = 256
PEER_HALF = PEER_QDIM // 2
PEER_TOPK = 16
PEER_TOKEN_BLOCK = 128
PLE_DIM = 256
RMS_EPS = 1e-6
NEG = -1e30
IN_SPLITS = (POOL_WIDTH, Q_WIDTH, 6 * KV_WIDTH, 3 * N_HEADS, D_MODEL, D_MODEL)
IN_WIDTH = sum(IN_SPLITS)

kernel_name = "hybrid_pool_nsa_peer_block"


def rms_norm(x, g):
    xf = x.astype(jnp.float32)
    y = xf * lax.rsqrt(jnp.mean(xf * xf, axis=-1, keepdims=True) + RMS_EPS)
    return (y * g.astype(jnp.float32)).astype(x.dtype)


def alibi_slopes():
    s = 2.0 ** (-8.0 * np.arange(1, N_HEADS + 1) / N_HEADS)
    return jnp.asarray(s, dtype=jnp.float32).reshape(N_KV_GROUPS, HEADS_PER_GROUP)


def pool_mixer(u, pool_w, pool_scale):
    B, T, _ = u.shape
    ug = u.astype(jnp.float32).reshape(B, T, POOL_GROUPS, POOL_GROUP_WIDTH)
    cs = jnp.cumsum(ug, axis=1)
    pos = jnp.arange(T, dtype=jnp.float32)
    outs = []
    for gi, w in enumerate(POOL_WINDOWS):
        c = cs[:, :, gi]
        lower = jnp.pad(c, ((0, 0), (w, 0), (0, 0)))[:, :T]
        cnt = jnp.minimum(pos + 1.0, float(w))[None, :, None]
        outs.append((c - lower) / cnt)
    pooled = jnp.stack(outs, axis=2) - ug
    mixed = jnp.einsum('btgc,gcd->btgd', pooled.astype(u.dtype), pool_w)
    return mixed.reshape(B, T, POOL_WIDTH) * pool_scale


def compress_blocks(kv, pos_emb, w):
    T = kv.shape[1]
    n_cmp = (T - CMP_BLOCK) // CMP_STRIDE + 1
    idx = (np.arange(n_cmp)[:, None] * CMP_STRIDE + np.arange(CMP_BLOCK)[None, :]).astype(np.int32)
    blocks = kv[:, idx] + pos_emb[None, None, :, None, :]
    return jnp.einsum('bnlgd,ldc->bngc', blocks, w.reshape(CMP_BLOCK, HEAD_DIM, HEAD_DIM))


def nsa_attention(q, k_cmp, v_cmp, k_slc, v_slc, k_win, v_win, gates,
                  cmp_pos_k, cmp_pos_v, cmp_w_k, cmp_w_v):
    B, T = q.shape[:2]
    G, Hg, dk = N_KV_GROUPS, HEADS_PER_GROUP, HEAD_DIM
    slopes = alibi_slopes()
    qh = (q.reshape(B, T, G, Hg, dk) * (dk ** -0.5)).transpose(0, 2, 3, 1, 4)
    tpos = jnp.arange(T, dtype=jnp.int32)

    kc = compress_blocks(k_cmp, cmp_pos_k, cmp_w_k)
    vc = compress_blocks(v_cmp, cmp_pos_v, cmp_w_v)
    n_cmp = kc.shape[1]
    cmp_end = jnp.arange(n_cmp, dtype=jnp.int32) * CMP_STRIDE + CMP_BLOCK - 1
    dist_c = (tpos[:, None] - cmp_end[None, :]).astype(jnp.float32)
    valid_c = dist_c >= 0
    s_c = jnp.einsum('bghtd,bngd->bghtn', qh, kc).astype(jnp.float32) \
        - slopes[None, :, :, None, None] * dist_c
    s_c = jnp.where(valid_c, s_c, NEG)
    p_c = jax.nn.softmax(s_c, axis=-1) * valid_c
    o_cmp = jnp.einsum('bghtn,bngd->bghtd', p_c.astype(vc.dtype), vc)

    n_sel = T // SLC_BLOCK
    k_sel = min(SLC_TOPK, n_sel)
    c_start = np.arange(n_cmp) * CMP_STRIDE
    s_start = np.arange(n_sel) * SLC_BLOCK
    overlap = ((c_start[:, None] < s_start[None, :] + SLC_BLOCK) &
               (c_start[:, None] + CMP_BLOCK > s_start[None, :])).astype(np.float32)
    imp = jnp.einsum('bghtn,nk->bgtk', p_c, jnp.asarray(overlap))
    cur = (tpos // SLC_BLOCK)[:, None]
    kk = jnp.arange(n_sel, dtype=jnp.int32)[None, :]
    forced = (kk == 0) | (kk == cur) | (kk == cur - 1)
    score = jnp.where(forced, FORCE_SCORE, jnp.where(kk <= cur, imp, -1.0))
    _, sel_idx = lax.top_k(score, k_sel)
    sel_ok = sel_idx <= cur[None, None]

    n_qb = T // Q_BLOCK
    ks = k_slc.reshape(B, n_sel, SLC_BLOCK, G, dk).transpose(0, 3, 1, 2, 4)
    vs = v_slc.reshape(B, n_sel, SLC_BLOCK, G, dk).transpose(0, 3, 1, 2, 4)
    q_b = jnp.moveaxis(qh.reshape(B, G, Hg, n_qb, Q_BLOCK, dk), 3, 0)
    idx_b = jnp.moveaxis(sel_idx.reshape(B, G, n_qb, Q_BLOCK, k_sel), 2, 0)
    ok_b = jnp.moveaxis(sel_ok.reshape(B, G, n_qb, Q_BLOCK, k_sel), 2, 0)
    t_b = tpos.reshape(n_qb, Q_BLOCK)
    bi = jnp.arange(B)[:, None, None, None]
    gi = jnp.arange(G)[None, :, None, None]
    in_blk = jnp.arange(SLC_BLOCK, dtype=jnp.int32)

    def sel_block(args):
        qb, ib, okb, tb = args
        kg = ks[bi, gi, ib]
        vg = vs[bi, gi, ib]
        spos = ib[..., None] * SLC_BLOCK + in_blk
        d = tb[None, None, :, None, None] - spos
        m = okb[..., None] & (d >= 0)
        sc = jnp.einsum('bghqd,bgqkld->bghqkl', qb, kg).astype(jnp.float32) \
            - slopes[None, :, :, None, None, None] * d[:, :, None].astype(jnp.float32)
        sc = jnp.where(m[:, :, None], sc, NEG)
        sh = sc.shape
        pr = jax.nn.softmax(sc.reshape(sh[:4] + (-1,)), axis=-1).reshape(sh)
        return jnp.einsum('bghqkl,bgqkld->bghqd', pr.astype(vg.dtype), vg)

    o_slc = lax.map(sel_block, (q_b, idx_b, ok_b, t_b))
    o_slc = jnp.moveaxis(o_slc, 0, 3).reshape(B, G, Hg, T, dk)

    span = WINDOW + Q_BLOCK
    kp = jnp.pad(k_win, ((0, 0), (WINDOW, 0), (0, 0), (0, 0)))
    vp = jnp.pad(v_win, ((0, 0), (WINDOW, 0), (0, 0), (0, 0)))
    widx = (np.arange(n_qb)[:, None] * Q_BLOCK + np.arange(span)[None, :]).astype(np.int32)
    kw = kp[:, widx]
    vw = vp[:, widx]
    spos_w = jnp.asarray(widx - WINDOW)
    d_w = t_b[:, :, None] - spos_w[:, None, :]
    m_w = (d_w >= 0) & (d_w < WINDOW) & (spos_w[:, None, :] >= 0)
    qw = qh.reshape(B, G, Hg, n_qb, Q_BLOCK, dk)
    s_w = jnp.einsum('bghiqd,bisgd->bghiqs', qw, kw).astype(jnp.float32) \
        - slopes[None, :, :, None, None, None] * d_w.astype(jnp.float32)
    s_w = jnp.where(m_w, s_w, NEG)
    p_w = jax.nn.softmax(s_w, axis=-1)
    o_win = jnp.einsum('bghiqs,bisgd->bghiqd', p_w.astype(vw.dtype), vw).reshape(B, G, Hg, T, dk)

    gt = jax.nn.sigmoid(gates.astype(jnp.float32)).reshape(B, T, G, Hg, 3).transpose(0, 2, 3, 1, 4)
    o = gt[..., 0:1] * o_cmp + gt[..., 1:2] * o_slc + gt[..., 2:3] * o_win
    return o.transpose(0, 3, 1, 2, 4).reshape(B, T, Q_WIDTH).astype(q.dtype)


def peer_ffn(h, w_q, keys1, keys2, u_tab, v_tab):
    B, T, D = h.shape
    q = (h @ w_q).reshape(B, T, PEER_HEADS, PEER_QDIM)
    s1 = jnp.einsum('bthd,nd->bthn', q[..., :PEER_HALF], keys1).astype(jnp.float32)
    s2 = jnp.einsum('bthd,nd->bthn', q[..., PEER_HALF:], keys2).astype(jnp.float32)
    v1, i1 = lax.top_k(s1, PEER_TOPK)
    v2, i2 = lax.top_k(s2, PEER_TOPK)
    cand = (v1[..., :, None] + v2[..., None, :]).reshape(B, T, PEER_HEADS, PEER_TOPK * PEER_TOPK)
    vbest, ci = lax.top_k(cand, PEER_TOPK)
    e = jnp.take_along_axis(i1, ci // PEER_TOPK, axis=-1) * PEER_NKEYS \
        + jnp.take_along_axis(i2, ci % PEER_TOPK, axis=-1)
    g = jax.nn.softmax(vbest, axis=-1)
    n_tb = (B * T) // PEER_TOKEN_BLOCK
    n_sel = PEER_HEADS * PEER_TOPK
    xb = h.reshape(n_tb, PEER_TOKEN_BLOCK, D)
    eb = e.reshape(n_tb, PEER_TOKEN_BLOCK, n_sel)
    gb = g.reshape(n_tb, PEER_TOKEN_BLOCK, n_sel)

    def expert_block(args):
        xt, et, gt = args
        a = jnp.einsum('nd,nkd->nk', xt, u_tab[et]).astype(jnp.float32)
        coef = gt * jax.nn.gelu(a)
        return jnp.einsum('nk,nkd->nd', coef.astype(v_tab.dtype), v_tab[et])

    out = lax.map(expert_block, (xb, eb, gb))
    return out.reshape(B, T, D).astype(h.dtype)


def setup_inputs(seed: int = 0) -> dict:
    key = jax.random.key(seed)
    ks = jax.random.split(key, 24)
    f32 = jnp.float32
    L, D = DEPTH, D_MODEL

    def nrm(k, shape, scale):
        return jax.random.normal(k, shape, f32) * scale

    def gain(k, shape):
        return 1.0 + 0.02 * jax.random.normal(k, shape, f32)

    return {
        "x": nrm(ks[0], (BATCH, SEQ, D), 1.0),
        "p": nrm(ks[1], (DEPTH, BATCH, SEQ, PLE_DIM), 1.0),
        "norm_mix_g": gain(ks[2], (L, D)),
        "w_in": nrm(ks[3], (L, D, IN_WIDTH), D ** -0.5),
        "pool_w": nrm(ks[4], (L, POOL_GROUPS, POOL_GROUP_WIDTH, POOL_GROUP_WIDTH), POOL_GROUP_WIDTH ** -0.5),
        "pool_scale": gain(ks[5], (L, POOL_WIDTH)),
        "cmp_pos_k": nrm(ks[6], (L, CMP_BLOCK, HEAD_DIM), 0.02),
        "cmp_pos_v": nrm(ks[7], (L, CMP_BLOCK, HEAD_DIM), 0.02),
        "cmp_w_k": nrm(ks[8], (L, CMP_BLOCK * HEAD_DIM, HEAD_DIM), (CMP_BLOCK * HEAD_DIM) ** -0.5),
        "cmp_w_v": nrm(ks[9], (L, CMP_BLOCK * HEAD_DIM, HEAD_DIM), (CMP_BLOCK * HEAD_DIM) ** -0.5),
        "w_up_pool": nrm(ks[10], (L, POOL_WIDTH, D), POOL_WIDTH ** -0.5),
        "w_up_nsa": nrm(ks[11], (L, Q_WIDTH, D), Q_WIDTH ** -0.5),
        "w_out": nrm(ks[12], (L, D, D), D ** -0.5),
        "norm_ffn_g": gain(ks[13], (L, D)),
        "peer_w_q": nrm(ks[14], (L, D, PEER_HEADS * PEER_QDIM), D ** -0.5),
        "peer_keys1": nrm(ks[15], (L, PEER_NKEYS, PEER_HALF), PEER_HALF ** -0.5),
        "peer_keys2": nrm(ks[16], (L, PEER_NKEYS, PEER_HALF), PEER_HALF ** -0.5),
        "peer_u": nrm(ks[17], (L, PEER_N_EXPERTS, D), D ** -0.5),
        "peer_v": nrm(ks[18], (L, PEER_N_EXPERTS, D), PEER_HEADS ** -0.5),
        "norm_ple_g": gain(ks[19], (L, D)),
        "ple_w_gate": nrm(ks[20], (L, D, D), D ** -0.5),
        "ple_w_proj": nrm(ks[21], (L, PLE_DIM, D), PLE_DIM ** -0.5),
        "norm_final_g": gain(ks[22], (D,)),
    }


def reference(x, p, norm_mix_g, w_in, pool_w, pool_scale, cmp_pos_k, cmp_pos_v, cmp_w_k, cmp_w_v,
              w_up_pool, w_up_nsa, w_out, norm_ffn_g, peer_w_q, peer_keys1, peer_keys2, peer_u, peer_v,
              norm_ple_g, ple_w_gate, ple_w_proj, norm_final_g):
    B, T, _ = x.shape
    offsets = [int(o) for o in np.cumsum(IN_SPLITS)[:-1]]
    for i in range(DEPTH):
        h = rms_norm(x, norm_mix_g[i])
        z = h @ w_in[i]
        u_pool, q, kv, nsa_gates, gate_pool, gate_nsa = jnp.split(z, offsets, axis=-1)
        kv = kv.reshape(B, T, 6, N_KV_GROUPS, HEAD_DIM)
        pool_out = pool_mixer(u_pool, pool_w[i], pool_scale[i])
        nsa_out = nsa_attention(q, kv[:, :, 0], kv[:, :, 1], kv[:, :, 2], kv[:, :, 3], kv[:, :, 4], kv[:, :, 5],
                                nsa_gates, cmp_pos_k[i], cmp_pos_v[i], cmp_w_k[i], cmp_w_v[i])
        merged = jax.nn.sigmoid(gate_pool) * (pool_out.astype(x.dtype) @ w_up_pool[i]) \
            + jax.nn.sigmoid(gate_nsa) * (nsa_out @ w_up_nsa[i])
        x = x + merged.astype(x.dtype) @ w_out[i]
        h2 = rms_norm(x, norm_ffn_g[i])
        x = x + peer_ffn(h2, peer_w_q[i], peer_keys1[i], peer_keys2[i], peer_u[i], peer_v[i])
        r = rms_norm(x, norm_ple_g[i])
        x = x + (jax.nn.sigmoid(r @ ple_w_gate[i]) * (p[i] @ ple_w_proj[i])).astype(x.dtype)
    return rms_norm(x, norm_final_g)
```

```python
import functools

import numpy as np
import jax
import jax.numpy as jnp
from jax import lax
from jax.experimental import pallas as pl
from jax.experimental.pallas import tpu as pltpu

F32 = jnp.float32
BF16 = jnp.bfloat16

POOL_GROUPS = 4
POOL_GROUP_WIDTH = 512
POOL_WIDTH = POOL_GROUPS * POOL_GROUP_WIDTH
POOL_WINDOWS = (2, 4, 8, 16)
POOL_HALO = 16
N_HEADS = 16
N_KV_GROUPS = 4
HEADS_PER_GROUP = N_HEADS // N_KV_GROUPS
HEAD_DIM = 128
Q_WIDTH = N_HEADS * HEAD_DIM
KV_WIDTH = N_KV_GROUPS * HEAD_DIM
GROUP_Q_WIDTH = HEADS_PER_GROUP * HEAD_DIM
CMP_BLOCK = 32
CMP_STRIDE = 16
SLC_BLOCK = 64
SLC_TOPK = 16
WINDOW = 512
FORCE_SCORE = 1e4
PEER_HEADS = 8
PEER_NKEYS = 128
PEER_QDIM = 256
PEER_HALF = PEER_QDIM // 2
PEER_TOPK = 16
PLE_DIM = 256
RMS_EPS = 1e-6
NEG = -1e30
GATE_LANES = 128

V7X_VMEM_BYTES = 64 * 1024 * 1024
VMEM_LIMIT = 56 * 1024 * 1024

_NT = (((1,), (1,)), ((), ()))


def _params(*sem):
    return pltpu.CompilerParams(dimension_semantics=sem, vmem_limit_bytes=VMEM_LIMIT)


def _pick(n, cands):
    for c in cands:
        if n % c == 0:
            return c
    raise ValueError(f"no tile in {cands} divides {n}")


def _rmsnorm_kernel(x_ref, g_ref, o_ref):
    x = x_ref[...]
    ms = jnp.mean(x * x, axis=-1, keepdims=True)
    o_ref[...] = (x * lax.rsqrt(ms + RMS_EPS) * g_ref[...]).astype(o_ref.dtype)


def _rmsnorm(x, g, out_dtype=BF16):
    T, D = x.shape
    tm = _pick(T, (256, 128, 64, 8))
    return pl.pallas_call(
        _rmsnorm_kernel,
        grid=(T // tm,),
        in_specs=[pl.BlockSpec((tm, D), lambda i: (i, 0)), pl.BlockSpec((1, D), lambda i: (0, 0))],
        out_specs=pl.BlockSpec((tm, D), lambda i: (i, 0)),
        out_shape=jax.ShapeDtypeStruct((T, D), out_dtype),
        compiler_params=_params("parallel"),
        name="rmsnorm",
    )(x, g.reshape(1, D))


def _mm_kernel(a_ref, b_ref, *rest, has_res):
    o_ref = rest[-1]
    acc = jnp.dot(a_ref[...], b_ref[...], preferred_element_type=F32)
    if has_res:
        acc = acc + rest[0][...]
    o_ref[...] = acc.astype(o_ref.dtype)


def _mm(a, b, out_dtype, res=None, name="mm"):
    M, K = a.shape
    _, N = b.shape
    tm = _pick(M, (1024, 512, 256, 128))
    tn = _pick(N, (1024, 512, 256, 128))
    in_specs = [pl.BlockSpec((tm, K), lambda j, i: (i, 0)), pl.BlockSpec((K, tn), lambda j, i: (0, j))]
    args = [a, b]
    if res is not None:
        in_specs.append(pl.BlockSpec((tm, tn), lambda j, i: (i, j)))
        args.append(res)
    return pl.pallas_call(
        functools.partial(_mm_kernel, has_res=res is not None),
        grid=(N // tn, M // tm),
        in_specs=in_specs,
        out_specs=pl.BlockSpec((tm, tn), lambda j, i: (i, j)),
        out_shape=jax.ShapeDtypeStruct((M, N), out_dtype),
        compiler_params=_params("parallel", "parallel"),
        name=name,
    )(*args)


def _pool_merge_kernel(u_ref, halo_ref, pw_ref, ps_ref, wp_ref, nsa_ref, wn_ref, gp_ref, gn_ref,
                       o_ref, pool_scr, *, tm):
    i = pl.program_id(0)
    j = pl.program_id(1)

    @pl.when(j == 0)
    def _():
        u = u_ref[...].astype(F32)
        halo = halo_ref[...].astype(F32)
        halo = jnp.where(i == 0, 0.0, halo)
        ext = jnp.concatenate([halo, u], axis=0)
        t = (i * tm + lax.broadcasted_iota(jnp.int32, (tm, 1), 0)).astype(F32)
        for g, w in enumerate(POOL_WINDOWS):
            c0, c1 = g * POOL_GROUP_WIDTH, (g + 1) * POOL_GROUP_WIDTH
            s = ext[:, c0:c1]
            k = 1
            while k < w:
                s = s[:-k] + s[k:]
                k *= 2
            wsum = s[POOL_HALO + 1 - w:]
            cnt = jnp.minimum(t + 1.0, float(w))
            pooled = wsum / cnt - u[:, c0:c1]
            mixed = jnp.dot(pooled.astype(BF16), pw_ref[g], preferred_element_type=F32)
            pool_scr[:, c0:c1] = (mixed * ps_ref[:, c0:c1]).astype(BF16)

    up_pool = jnp.dot(pool_scr[...], wp_ref[...], preferred_element_type=F32)
    up_nsa = jnp.dot(nsa_ref[...], wn_ref[...], preferred_element_type=F32)
    merged = jax.nn.sigmoid(gp_ref[...].astype(F32)) * up_pool + jax.nn.sigmoid(gn_ref[...].astype(F32)) * up_nsa
    o_ref[...] = merged.astype(o_ref.dtype)


def _pool_merge(z, gp_off, gn_off, pool_w, pool_scale, w_up_pool, nsa_out, w_up_nsa, D):
    T = z.shape[0]
    tm = _pick(T, (512, 256, 128))
    tn = _pick(D, (1024, 512, 256, 128))
    assert gp_off % tn == 0 and gn_off % tn == 0 and tm % POOL_HALO == 0
    hb = tm // POOL_HALO
    return pl.pallas_call(
        functools.partial(_pool_merge_kernel, tm=tm),
        grid=(T // tm, D // tn),
        in_specs=[
            pl.BlockSpec((tm, POOL_WIDTH), lambda i, j: (i, 0)),
            pl.BlockSpec((POOL_HALO, POOL_WIDTH), lambda i, j: (jnp.maximum(i * hb - 1, 0), 0)),
            pl.BlockSpec((POOL_GROUPS, POOL_GROUP_WIDTH, POOL_GROUP_WIDTH), lambda i, j: (0, 0, 0)),
            pl.BlockSpec((1, POOL_WIDTH), lambda i, j: (0, 0)),
            pl.BlockSpec((POOL_WIDTH, tn), lambda i, j: (0, j)),
            pl.BlockSpec((tm, Q_WIDTH), lambda i, j: (i, 0)),
            pl.BlockSpec((Q_WIDTH, tn), lambda i, j: (0, j)),
            pl.BlockSpec((tm, tn), lambda i, j: (i, gp_off // tn + j)),
            pl.BlockSpec((tm, tn), lambda i, j: (i, gn_off // tn + j)),
        ],
        out_specs=pl.BlockSpec((tm, tn), lambda i, j: (i, j)),
        out_shape=jax.ShapeDtypeStruct((T, D), BF16),
        scratch_shapes=[pltpu.VMEM((tm, POOL_WIDTH), BF16)],
        compiler_params=_params("parallel", "arbitrary"),
        name="pool_merge",
    )(z, z, pool_w, pool_scale.reshape(1, POOL_WIDTH), w_up_pool, nsa_out, w_up_nsa, z, z)


def _compress_kernel(r_ref, w_ref, pos_ref, o_ref, *, n_cmp, transpose_out):
    half = CMP_STRIDE * HEAD_DIM
    r = r_ref[0]
    w = w_ref[...]
    top = jnp.dot(r, w[:half], preferred_element_type=F32)
    bot = jnp.dot(r, w[half:], preferred_element_type=F32)
    bias = jnp.dot(pos_ref[...], w, preferred_element_type=F32)
    ncp = top.shape[0]
    bot_next = jnp.concatenate([bot[1:], jnp.zeros((1, HEAD_DIM), F32)], axis=0)
    out = top + bot_next + bias
    row = lax.broadcasted_iota(jnp.int32, (ncp, 1), 0)
    out = jnp.where(row < n_cmp, out, 0.0)
    if transpose_out:
        o_ref[0] = out.T.astype(o_ref.dtype)
    else:
        o_ref[0] = out.astype(o_ref.dtype)


def _compress(r, w, pos, n_cmp, transpose_out):
    G, ncp, width = r.shape
    oshape = (G, HEAD_DIM, ncp) if transpose_out else (G, ncp, HEAD_DIM)
    return pl.pallas_call(
        functools.partial(_compress_kernel, n_cmp=n_cmp, transpose_out=transpose_out),
        grid=(G,),
        in_specs=[
            pl.BlockSpec((1, ncp, width), lambda g: (g, 0, 0)),
            pl.BlockSpec((CMP_BLOCK * HEAD_DIM, HEAD_DIM), lambda g: (0, 0)),
            pl.BlockSpec((1, CMP_BLOCK * HEAD_DIM), lambda g: (0, 0)),
        ],
        out_specs=pl.BlockSpec((1,) + oshape[1:], lambda g: (g, 0, 0)),
        out_shape=jax.ShapeDtypeStruct(oshape, BF16),
        compiler_params=_params("parallel"),
        name="nsa_compress",
    )(r, w, pos)


def _extract_topk(a, k, want_vals=False):
    n = a.shape[0]
    idx = lax.broadcasted_iota(jnp.int32, a.shape, 0).astype(F32)
    rank = jnp.full(a.shape, float(k), F32)
    vals = []
    for r in range(k):
        m = jnp.max(a, axis=0, keepdims=True)
        first = jnp.min(jnp.where(a == m, idx, float(n)), axis=0, keepdims=True)
        hit = idx == first
        rank = jnp.where(hit, float(r), rank)
        a = jnp.where(hit, -jnp.inf, a)
        vals.append(m)
    return (rank, vals) if want_vals else rank


def _cmp_select_kernel(slope_ref, q_ref, kc_ref, vct_ref, ovt_ref, o_ref, sel_ref, *, tq):
    g = pl.program_id(0)
    i = pl.program_id(1)
    ncp = kc_ref.shape[1]
    ns = ovt_ref.shape[0]
    t = i * tq + lax.broadcasted_iota(jnp.int32, (1, tq), 1)
    n = lax.broadcasted_iota(jnp.int32, (ncp, 1), 0)
    dist = (t - (n * CMP_STRIDE + CMP_BLOCK - 1)).astype(F32)
    valid = dist >= 0.0
    kc = kc_ref[0]
    vct = vct_ref[0]
    q = q_ref[...]
    psum = jnp.zeros((ncp, tq), F32)
    for h in range(HEADS_PER_GROUP):
        qh = (q[:, h * HEAD_DIM:(h + 1) * HEAD_DIM].astype(F32) * (HEAD_DIM ** -0.5)).astype(BF16)
        s = lax.dot_general(kc, qh, _NT, preferred_element_type=F32)
        s = s - slope_ref[g * HEADS_PER_GROUP + h] * dist
        s = jnp.where(valid, s, NEG)
        m = jnp.max(s, axis=0, keepdims=True)
        e = jnp.exp(s - m)
        l = jnp.sum(e, axis=0, keepdims=True)
        p = jnp.where(valid, e / l, 0.0)
        ot = jnp.dot(vct, p.astype(BF16), preferred_element_type=F32)
        o_ref[:, h * HEAD_DIM:(h + 1) * HEAD_DIM] = ot.T.astype(o_ref.dtype)
        psum = psum + p
    p_hi = psum.astype(BF16)
    p_lo = (psum - p_hi.astype(F32)).astype(BF16)
    ovt = ovt_ref[...]
    imp = jnp.dot(ovt, p_hi, preferred_element_type=F32) + jnp.dot(ovt, p_lo, preferred_element_type=F32)
    kk = lax.broadcasted_iota(jnp.int32, (ns, 1), 0)
    cur = jnp.right_shift(t, SLC_BLOCK.bit_length() - 1)
    forced = (kk == 0) | (kk == cur) | (kk == cur - 1)
    score = jnp.where(forced, FORCE_SCORE, jnp.where(kk <= cur, imp, -1.0))
    k_sel = min(SLC_TOPK, ns)
    rank = _extract_topk(score, k_sel)
    sel = jnp.where((rank < float(k_sel)) & (kk <= cur), 1.0, 0.0)
    sel_ref[0] = sel.T.astype(sel_ref.dtype)


def _cmp_select(slopes, z, q_off, kc, vct, ovt, T):
    tq = _pick(T, (256, 128))
    ncp = kc.shape[1]
    ns = ovt.shape[0]
    assert q_off % GROUP_Q_WIDTH == 0
    qb = q_off // GROUP_Q_WIDTH
    return pl.pallas_call(
        functools.partial(_cmp_select_kernel, tq=tq),
        grid=(N_KV_GROUPS, T // tq),
        in_specs=[
            pl.BlockSpec(memory_space=pltpu.SMEM),
            pl.BlockSpec((tq, GROUP_Q_WIDTH), lambda g, i: (i, qb + g)),
            pl.BlockSpec((1, ncp, HEAD_DIM), lambda g, i: (g, 0, 0)),
            pl.BlockSpec((1, HEAD_DIM, ncp), lambda g, i: (g, 0, 0)),
            pl.BlockSpec((ns, ncp), lambda g, i: (0, 0)),
        ],
        out_specs=[
            pl.BlockSpec((tq, GROUP_Q_WIDTH), lambda g, i: (i, g)),
            pl.BlockSpec((1, tq, ns), lambda g, i: (g, i, 0)),
        ],
        out_shape=[
            jax.ShapeDtypeStruct((T, Q_WIDTH), BF16),
            jax.ShapeDtypeStruct((N_KV_GROUPS, T, ns), BF16),
        ],
        compiler_params=_params("parallel", "parallel"),
        name="nsa_cmp_select",
    )(slopes, z, kc, vct, ovt)


def _nsa_attend_kernel(slope_ref, q_ref, ks_ref, vs_ref, kw_ref, vw_ref, ocmp_ref, sel_ref, exp_ref,
                       gate_ref, o_ref, q_scr, acc_scr, *, tq, tk, span):
    g = pl.program_id(0)
    i = pl.program_id(1)
    t0 = i * tq
    hg = HEADS_PER_GROUP
    q = q_ref[...]
    for h in range(hg):
        qh = q[:, h * HEAD_DIM:(h + 1) * HEAD_DIM].astype(F32) * (HEAD_DIM ** -0.5)
        q_scr[h * tq:(h + 1) * tq, :] = qh.astype(BF16)
    q_all = q_scr[...]
    slopes = [slope_ref[g * hg + h] for h in range(hg)]

    sel = sel_ref[0]
    rel = (lax.broadcasted_iota(jnp.int32, (tq, tk), 0) - lax.broadcasted_iota(jnp.int32, (tq, tk), 1)).astype(F32)
    acc_scr[...] = jnp.zeros_like(acc_scr)
    n_kv = (t0 + tq + tk - 1) // tk

    def kv_step(j, carry):
        ms, ls = carry
        s0 = pl.multiple_of(j * tk, tk)
        kt = ks_ref[pl.ds(s0, tk), :]
        vt = vs_ref[pl.ds(s0, tk), :]
        d = rel + (t0 - s0).astype(F32)
        chosen = jnp.dot(sel, exp_ref[j], preferred_element_type=F32)
        allow = (chosen > 0.5) & (d >= 0.0)
        s_all = lax.dot_general(q_all, kt, _NT, preferred_element_type=F32)
        new_ms, new_ls, ps = [], [], []
        for h in range(hg):
            s = s_all[h * tq:(h + 1) * tq] - slopes[h] * d
            s = jnp.where(allow, s, NEG)
            m_new = jnp.maximum(ms[h], jnp.max(s, axis=1, keepdims=True))
            alpha = jnp.exp(ms[h] - m_new)
            p = jnp.exp(s - m_new)
            new_ls.append(alpha * ls[h] + jnp.sum(p, axis=1, keepdims=True))
            new_ms.append(m_new)
            acc_scr[h * tq:(h + 1) * tq, :] = alpha * acc_scr[h * tq:(h + 1) * tq, :]
            ps.append(p.astype(BF16))
        p_all = jnp.concatenate(ps, axis=0)
        acc_scr[...] += jnp.dot(p_all, vt, preferred_element_type=F32)
        return tuple(new_ms), tuple(new_ls)

    init = (tuple(jnp.full((tq, 1), NEG, F32) for _ in range(hg)),
            tuple(jnp.zeros((tq, 1), F32) for _ in range(hg)))
    ms, ls = lax.fori_loop(0, n_kv, kv_step, init)

    w0 = pl.multiple_of(jnp.maximum(t0 - WINDOW, 0), 8)
    kw = kw_ref[pl.ds(w0, span), :]
    vw = vw_ref[pl.ds(w0, span), :]
    dw = ((t0 - w0) + lax.broadcasted_iota(jnp.int32, (tq, span), 0)
          - lax.broadcasted_iota(jnp.int32, (tq, span), 1)).astype(F32)
    allow_w = (dw >= 0.0) & (dw < float(WINDOW))
    sw_all = lax.dot_general(q_all, kw, _NT, preferred_element_type=F32)
    pws, lws = [], []
    for h in range(hg):
        s = sw_all[h * tq:(h + 1) * tq] - slopes[h] * dw
        s = jnp.where(allow_w, s, NEG)
        m = jnp.max(s, axis=1, keepdims=True)
        p = jnp.exp(s - m)
        lws.append(jnp.sum(p, axis=1, keepdims=True))
        pws.append(p.astype(BF16))
    ow_all = jnp.dot(jnp.concatenate(pws, axis=0), vw, preferred_element_type=F32)

    gt = jax.nn.sigmoid(gate_ref[...])
    ocmp = ocmp_ref[...].astype(F32)
    for h in range(hg):
        o_c = ocmp[:, h * HEAD_DIM:(h + 1) * HEAD_DIM]
        o_s = acc_scr[h * tq:(h + 1) * tq, :] / ls[h]
        o_w = ow_all[h * tq:(h + 1) * tq] / lws[h]
        o = gt[:, 3 * h:3 * h + 1] * o_c + gt[:, 3 * h + 1:3 * h + 2] * o_s + gt[:, 3 * h + 2:3 * h + 3] * o_w
        o_ref[:, h * HEAD_DIM:(h + 1) * HEAD_DIM] = o.astype(o_ref.dtype)


def _nsa_attend(slopes, z, q_off, kv_off, o_cmp, sel, expand, gates, T):
    tq = _pick(T, (256, 128))
    n_exp, ns, tk = expand.shape
    span = WINDOW + tq
    assert T >= span and q_off % GROUP_Q_WIDTH == 0 and kv_off % HEAD_DIM == 0
    qb = q_off // GROUP_Q_WIDTH
    kvb = kv_off // HEAD_DIM
    G = N_KV_GROUPS

    def kv_spec(branch):
        return pl.BlockSpec((T, HEAD_DIM), lambda g, i: (0, kvb + branch * G + g))

    return pl.pallas_call(
        functools.partial(_nsa_attend_kernel, tq=tq, tk=tk, span=span),
        grid=(G, T // tq),
        in_specs=[
            pl.BlockSpec(memory_space=pltpu.SMEM),
            pl.BlockSpec((tq, GROUP_Q_WIDTH), lambda g, i: (i, qb + g)),
            kv_spec(2), kv_spec(3), kv_spec(4), kv_spec(5),
            pl.BlockSpec((tq, GROUP_Q_WIDTH), lambda g, i: (i, g)),
            pl.BlockSpec((1, tq, ns), lambda g, i: (g, i, 0)),
            pl.BlockSpec((n_exp, ns, tk), lambda g, i: (0, 0, 0)),
            pl.BlockSpec((tq, GATE_LANES), lambda g, i: (i, g)),
        ],
        out_specs=pl.BlockSpec((tq, GROUP_Q_WIDTH), lambda g, i: (i, g)),
        out_shape=jax.ShapeDtypeStruct((T, Q_WIDTH), BF16),
        scratch_shapes=[
            pltpu.VMEM((HEADS_PER_GROUP * tq, HEAD_DIM), BF16),
            pltpu.VMEM((HEADS_PER_GROUP * tq, HEAD_DIM), F32),
        ],
        compiler_params=_params("parallel", "arbitrary"),
        name="nsa_attend",
    )(slopes, z, z, z, z, z, o_cmp, sel, expand, gates)


_CAND_PIECES = [(0, 16), (1, 8), (2, 8), (3, 8), (4, 8), (5, 8), (6, 8), (7, 8)]


def _peer_select_kernel(q_ref, k1_ref, k2_ref, cnt_ref, f1_ref, rank2_ref, e2_ref):
    K = PEER_TOPK
    q = q_ref[...].astype(BF16)
    s1 = lax.dot_general(k1_ref[...], q[:, :PEER_HALF], _NT, preferred_element_type=F32)
    s2 = lax.dot_general(k2_ref[...], q[:, PEER_HALF:], _NT, preferred_element_type=F32)
    rank1, v1 = _extract_topk(s1, K, want_vals=True)
    rank2, v2 = _extract_topk(s2, K, want_vals=True)
    v1s = jnp.concatenate(v1, axis=0)
    v2s = jnp.concatenate(v2, axis=0)
    pieces = [v1[r1] + v2s[:L] for r1, L in _CAND_PIECES]
    pieces.append(v1s[8:] + v2[0])
    cand = jnp.concatenate(pieces, axis=0)
    crank = _extract_topk(cand, K)
    chosen = crank < float(K)
    top = v1[0] + v2[0]
    z = jnp.sum(jnp.where(chosen, jnp.exp(cand - top), 0.0), axis=0, keepdims=True)
    chosen_f = jnp.where(chosen, 1.0, 0.0)
    counts = []
    off = 0
    for _, L in _CAND_PIECES:
        counts.append(jnp.sum(chosen_f[off:off + L], axis=0, keepdims=True))
        off += L
    tail = chosen_f[off:off + 8]
    for r in range(8):
        counts.append(tail[r:r + 1])
    cnt = jnp.zeros_like(s1)
    for r1 in range(K):
        cnt = jnp.where(rank1 == float(r1), counts[r1], cnt)
    cnt_ref[0] = cnt
    f1_ref[0] = jnp.exp(s1 - v1[0]) / z
    rank2_ref[0] = rank2
    e2_ref[0] = jnp.exp(s2 - v2[0])


def _peer_select(qp, keys1, keys2):
    T = qp.shape[0]
    tp = _pick(T, (128,))
    shp = jax.ShapeDtypeStruct((PEER_HEADS, PEER_NKEYS, T), F32)
    ospec = pl.BlockSpec((1, PEER_NKEYS, tp), lambda i, h: (h, 0, i))
    return pl.pallas_call(
        _peer_select_kernel,
        grid=(T // tp, PEER_HEADS),
        in_specs=[
            pl.BlockSpec((tp, PEER_QDIM), lambda i, h: (i, h)),
            pl.BlockSpec((PEER_NKEYS, PEER_HALF), lambda i, h: (0, 0)),
            pl.BlockSpec((PEER_NKEYS, PEER_HALF), lambda i, h: (0, 0)),
        ],
        out_specs=[ospec, ospec, ospec, ospec],
        out_shape=[shp, shp, shp, shp],
        compiler_params=_params("parallel", "parallel"),
        name="peer_select",
    )(qp, keys1, keys2)


def _peer_dense_kernel(h_ref, u_ref, v_ref, cnt_ref, f1_ref, rank2_ref, e2_ref, o_ref, *, ipc):
    c = pl.program_id(1)
    nk = PEER_NKEYS
    at = lax.dot_general(u_ref[...], h_ref[...], _NT, preferred_element_type=F32)
    coefs = []
    for k in range(ipc):
        i1 = c * ipc + k
        w = jnp.zeros((nk, at.shape[1]), F32)
        for h in range(PEER_HEADS):
            crow = cnt_ref[h, pl.ds(i1, 1), :]
            frow = f1_ref[h, pl.ds(i1, 1), :]
            w = w + jnp.where(rank2_ref[h] < crow, e2_ref[h] * frow, 0.0)
        coefs.append((w * jax.nn.gelu(at[k * nk:(k + 1) * nk])).T.astype(BF16))
    coef = jnp.concatenate(coefs, axis=1) if ipc > 1 else coefs[0]
    contrib = jnp.dot(coef, v_ref[...], preferred_element_type=F32)

    @pl.when(c == 0)
    def _():
        o_ref[...] = contrib

    @pl.when(c != 0)
    def _():
        o_ref[...] += contrib


def _peer_dense(h2, u, v, cnt, f1, rank2, e2):
    T, D = h2.shape
    NE = u.shape[0]
    tm = _pick(T, (512, 256, 128))
    ipc = 2
    ec = ipc * PEER_NKEYS
    sel_spec = pl.BlockSpec((PEER_HEADS, PEER_NKEYS, tm), lambda i, c: (0, 0, i))
    return pl.pallas_call(
        functools.partial(_peer_dense_kernel, ipc=ipc),
        grid=(T // tm, NE // ec),
        in_specs=[
            pl.BlockSpec((tm, D), lambda i, c: (i, 0)),
            pl.BlockSpec((ec, D), lambda i, c: (c, 0)),
            pl.BlockSpec((ec, D), lambda i, c: (c, 0)),
            sel_spec, sel_spec, sel_spec, sel_spec,
        ],
        out_specs=pl.BlockSpec((tm, D), lambda i, c: (i, 0)),
        out_shape=jax.ShapeDtypeStruct((T, D), F32),
        compiler_params=_params("parallel", "arbitrary"),
        name="peer_dense",
    )(h2, u, v, cnt, f1, rank2, e2)


def _ple_final_kernel(x1_ref, peer_ref, gple_ref, wg_ref, p_ref, wp_ref, gfin_ref, o_ref, r_scr, *, tn, nj):
    j = pl.program_id(1)

    @pl.when(j == 0)
    def _():
        x2 = x1_ref[...] + peer_ref[...]
        o_ref[...] = x2
        ms = jnp.mean(x2 * x2, axis=-1, keepdims=True)
        r_scr[...] = (x2 * lax.rsqrt(ms + RMS_EPS) * gple_ref[...]).astype(BF16)

    gate = jax.nn.sigmoid(jnp.dot(r_scr[...], wg_ref[...], preferred_element_type=F32))
    proj = jnp.dot(p_ref[...].astype(BF16), wp_ref[...], preferred_element_type=F32)
    for jj in range(nj):
        @pl.when(j == jj)
        def _():
            o_ref[:, jj * tn:(jj + 1) * tn] += gate * proj

    @pl.when(j == nj - 1)
    def _():
        x3 = o_ref[...]
        ms = jnp.mean(x3 * x3, axis=-1, keepdims=True)
        o_ref[...] = x3 * lax.rsqrt(ms + RMS_EPS) * gfin_ref[...]


def _ple_final(x1, peer, g_ple, w_gate, p, w_proj, g_final):
    T, D = x1.shape
    tm = _pick(T, (256, 128))
    tn = _pick(D, (1024, 512, 256, 128))
    nj = D // tn
    return pl.pallas_call(
        functools.partial(_ple_final_kernel, tn=tn, nj=nj),
        grid=(T // tm, nj),
        in_specs=[
            pl.BlockSpec((tm, D), lambda i, j: (i, 0)),
            pl.BlockSpec((tm, D), lambda i, j: (i, 0)),
            pl.BlockSpec((1, D), lambda i, j: (0, 0)),
            pl.BlockSpec((D, tn), lambda i, j: (0, j)),
            pl.BlockSpec((tm, PLE_DIM), lambda i, j: (i, 0)),
            pl.BlockSpec((PLE_DIM, tn), lambda i, j: (0, j)),
            pl.BlockSpec((1, D), lambda i, j: (0, 0)),
        ],
        out_specs=pl.BlockSpec((tm, D), lambda i, j: (i, 0)),
        out_shape=jax.ShapeDtypeStruct((T, D), F32),
        scratch_shapes=[pltpu.VMEM((tm, D), BF16)],
        compiler_params=_params("parallel", "arbitrary"),
        name="ple_final",
    )(x1, peer, g_ple.reshape(1, D), w_gate, p, w_proj, g_final.reshape(1, D))


def _alibi_slopes():
    return jnp.asarray(2.0 ** (-8.0 * np.arange(1, N_HEADS + 1) / N_HEADS), dtype=F32)


def _overlap_t(T):
    ncp = T // CMP_STRIDE
    n_cmp = (T - CMP_BLOCK) // CMP_STRIDE + 1
    n_sel = T // SLC_BLOCK
    c_start = np.arange(ncp) * CMP_STRIDE
    s_start = np.arange(n_sel) * SLC_BLOCK
    ov = (c_start[None, :] < s_start[:, None] + SLC_BLOCK) & (c_start[None, :] + CMP_BLOCK > s_start[:, None])
    ov = ov & (np.arange(ncp)[None, :] < n_cmp)
    return jnp.asarray(ov.astype(np.float32), dtype=BF16)


def _expand_blocks(T, tk):
    n_sel = T // SLC_BLOCK
    key_blk = (np.arange(T) // SLC_BLOCK).reshape(T // tk, 1, tk)
    e = key_blk == np.arange(n_sel).reshape(1, n_sel, 1)
    return jnp.asarray(e.astype(np.float32), dtype=BF16)


def _layer(x, p, norm_mix_g, w_in, pool_w, pool_scale, cmp_pos_k, cmp_pos_v, cmp_w_k, cmp_w_v,
           w_up_pool, w_up_nsa, w_out, norm_ffn_g, peer_w_q, peer_keys1, peer_keys2, peer_u, peer_v,
           norm_ple_g, ple_w_gate, ple_w_proj, norm_final_g):
    T, D = x.shape
    G = N_KV_GROUPS
    q_off = POOL_WIDTH
    kv_off = q_off + Q_WIDTH
    gates_off = kv_off + 6 * KV_WIDTH
    gp_src = gates_off + 3 * N_HEADS
    gn_src = gp_src + D
    gp_off = gates_off
    gn_off = gp_off + D

    w_main = jnp.concatenate([w_in[:, :gates_off], w_in[:, gp_src:]], axis=1).astype(BF16)
    w_g = w_in[:, gates_off:gp_src].reshape(D, G, 3 * HEADS_PER_GROUP)
    w_g = jnp.pad(w_g, ((0, 0), (0, 0), (0, GATE_LANES - 3 * HEADS_PER_GROUP))).reshape(D, G * GATE_LANES)
    w_g = w_g.astype(BF16)

    h = _rmsnorm(x, norm_mix_g)
    z = _mm(h, w_main, BF16, name="in_proj")
    gates = _mm(h, w_g, F32, name="gate_proj")

    ncp = T // CMP_STRIDE
    n_cmp = (T - CMP_BLOCK) // CMP_STRIDE + 1

    def blocks16(col):
        a = z[:, col:col + KV_WIDTH].reshape(ncp, CMP_STRIDE, G, HEAD_DIM)
        return a.transpose(2, 0, 1, 3).reshape(G, ncp, CMP_STRIDE * HEAD_DIM)

    kc = _compress(blocks16(kv_off), cmp_w_k.astype(BF16), cmp_pos_k.reshape(1, -1).astype(BF16), n_cmp, False)
    vct = _compress(blocks16(kv_off + KV_WIDTH), cmp_w_v.astype(BF16), cmp_pos_v.reshape(1, -1).astype(BF16),
                    n_cmp, True)
    slopes = _alibi_slopes()
    o_cmp, sel = _cmp_select(slopes, z, q_off, kc, vct, _overlap_t(T), T)
    tk = _pick(T, (512, 256))
    nsa_out = _nsa_attend(slopes, z, q_off, kv_off, o_cmp, sel, _expand_blocks(T, tk), gates, T)

    merged = _pool_merge(z, gp_off, gn_off, pool_w.astype(BF16), pool_scale, w_up_pool.astype(BF16),
                         nsa_out, w_up_nsa.astype(BF16), D)
    x1 = _mm(merged, w_out.astype(BF16), F32, res=x, name="out_proj")

    h2 = _rmsnorm(x1, norm_ffn_g)
    qp = _mm(h2, peer_w_q.astype(BF16), F32, name="peer_q")
    cnt, f1, rank2, e2 = _peer_select(qp, peer_keys1.astype(BF16), peer_keys2.astype(BF16))
    peer = _peer_dense(h2, peer_u.astype(BF16), peer_v.astype(BF16), cnt, f1, rank2, e2)

    return _ple_final(x1, peer, norm_ple_g, ple_w_gate.astype(BF16), p, ple_w_proj.astype(BF16), norm_final_g)


def kernel(x, p, norm_mix_g, w_in, pool_w, pool_scale, cmp_pos_k, cmp_pos_v, cmp_w_k, cmp_w_v, w_up_pool, w_up_nsa, w_out, norm_ffn_g, peer_w_q, peer_keys1, peer_keys2, peer_u, peer_v, norm_ple_g, ple_w_gate, ple_w_proj, norm_final_g):
    B, T, D = x.shape
    depth = w_in.shape[0]
    assert depth == 1, "the final rmsnorm is fused into the layer's last kernel"
    outs = []
    for b in range(B):
        outs.append(_layer(
            x[b], p[0, b], norm_mix_g[0], w_in[0], pool_w[0], pool_scale[0], cmp_pos_k[0], cmp_pos_v[0],
            cmp_w_k[0], cmp_w_v[0], w_up_pool[0], w_up_nsa[0], w_out[0], norm_ffn_g[0], peer_w_q[0],
            peer_keys1[0], peer_keys2[0], peer_u[0], peer_v[0], norm_ple_g[0], ple_w_gate[0], ple_w_proj[0],
            norm_final_g))
    return jnp.stack(outs, axis=0)
```

```python
import functools

import numpy as np
import jax
import jax.numpy as jnp
from jax import lax
from jax.experimental import pallas as pl
from jax.experimental.pallas import tpu as pltpu

F32 = jnp.float32
BF16 = jnp.bfloat16

POOL_GROUPS = 4
POOL_GROUP_WIDTH = 512
POOL_WIDTH = POOL_GROUPS * POOL_GROUP_WIDTH
POOL_WINDOWS = (2, 4, 8, 16)
POOL_HALO = 16
N_HEADS = 16
N_KV_GROUPS = 4
HEADS_PER_GROUP = N_HEADS // N_KV_GROUPS
HEAD_DIM = 128
Q_WIDTH = N_HEADS * HEAD_DIM
KV_WIDTH = N_KV_GROUPS * HEAD_DIM
GROUP_Q_WIDTH = HEADS_PER_GROUP * HEAD_DIM
CMP_BLOCK = 32
CMP_STRIDE = 16
SLC_BLOCK = 64
SLC_TOPK = 16
WINDOW = 512
FORCE_SCORE = 1e4
PEER_HEADS = 8
PEER_NKEYS = 128
PEER_QDIM = 256
PEER_HALF = PEER_QDIM // 2
PEER_TOPK = 16
PLE_DIM = 256
RMS_EPS = 1e-6
NEG = -1e30
GATE_LANES = 128

LANES = 128
V7X_VMEM_BYTES = 64 * 1024 * 1024
VMEM_LIMIT = 56 * 1024 * 1024

_NT = (((1,), (1,)), ((), ()))


def _params(*sem):
    return pltpu.CompilerParams(dimension_semantics=sem, vmem_limit_bytes=VMEM_LIMIT)


def _pick(n, cands):
    for c in cands:
        if n % c == 0:
            return c
    raise ValueError(f"no tile in {cands} divides {n}")


def _rmsnorm_kernel(x_ref, g_ref, o_ref):
    x = x_ref[...]
    ms = jnp.mean(x * x, axis=-1, keepdims=True)
    o_ref[...] = (x * lax.rsqrt(ms + RMS_EPS) * g_ref[...]).astype(o_ref.dtype)


def _rmsnorm(x, g, out_dtype=BF16):
    T, D = x.shape
    tm = _pick(T, (256, 128, 64, 8))
    return pl.pallas_call(
        _rmsnorm_kernel,
        grid=(T // tm,),
        in_specs=[pl.BlockSpec((tm, D), lambda i: (i, 0)), pl.BlockSpec((1, D), lambda i: (0, 0))],
        out_specs=pl.BlockSpec((tm, D), lambda i: (i, 0)),
        out_shape=jax.ShapeDtypeStruct((T, D), out_dtype),
        compiler_params=_params("parallel"),
        name="rmsnorm",
    )(x, g.reshape(1, D))


def _mm_kernel(a_ref, b_ref, *rest, has_res):
    o_ref = rest[-1]
    acc = jnp.dot(a_ref[...], b_ref[...], preferred_element_type=F32)
    if has_res:
        acc = acc + rest[0][...]
    o_ref[...] = acc.astype(o_ref.dtype)


def _mm(a, b, out_dtype, res=None, name="mm"):
    M, K = a.shape
    _, N = b.shape
    tm = _pick(M, (1024, 512, 256, 128))
    tn = _pick(N, (1024, 512, 256, 128))
    in_specs = [pl.BlockSpec((tm, K), lambda j, i: (i, 0)), pl.BlockSpec((K, tn), lambda j, i: (0, j))]
    args = [a, b]
    if res is not None:
        in_specs.append(pl.BlockSpec((tm, tn), lambda j, i: (i, j)))
        args.append(res)
    return pl.pallas_call(
        functools.partial(_mm_kernel, has_res=res is not None),
        grid=(N // tn, M // tm),
        in_specs=in_specs,
        out_specs=pl.BlockSpec((tm, tn), lambda j, i: (i, j)),
        out_shape=jax.ShapeDtypeStruct((M, N), out_dtype),
        compiler_params=_params("parallel", "parallel"),
        name=name,
    )(*args)


def _pool_merge_kernel(u_ref, halo_ref, pw_ref, ps_ref, wp_ref, nsa_ref, wn_ref, gp_ref, gn_ref,
                       o_ref, pool_scr, *, tm):
    i = pl.program_id(0)
    j = pl.program_id(1)

    @pl.when(j == 0)
    def _():
        u = u_ref[...].astype(F32)
        halo = halo_ref[...].astype(F32)
        halo = jnp.where(i == 0, 0.0, halo)
        ext = jnp.concatenate([halo, u], axis=0)
        t = (i * tm + lax.broadcasted_iota(jnp.int32, (tm, 1), 0)).astype(F32)
        for g, w in enumerate(POOL_WINDOWS):
            c0, c1 = g * POOL_GROUP_WIDTH, (g + 1) * POOL_GROUP_WIDTH
            s = ext[:, c0:c1]
            k = 1
            while k < w:
                s = s[:-k] + s[k:]
                k *= 2
            wsum = s[POOL_HALO + 1 - w:]
            cnt = jnp.minimum(t + 1.0, float(w))
            pooled = wsum / cnt - u[:, c0:c1]
            mixed = jnp.dot(pooled.astype(BF16), pw_ref[g], preferred_element_type=F32)
            pool_scr[:, c0:c1] = (mixed * ps_ref[:, c0:c1]).astype(BF16)

    up_pool = jnp.dot(pool_scr[...], wp_ref[...], preferred_element_type=F32)
    up_nsa = jnp.dot(nsa_ref[...], wn_ref[...], preferred_element_type=F32)
    merged = jax.nn.sigmoid(gp_ref[...].astype(F32)) * up_pool + jax.nn.sigmoid(gn_ref[...].astype(F32)) * up_nsa
    o_ref[...] = merged.astype(o_ref.dtype)


def _pool_merge(z, gp_off, gn_off, pool_w, pool_scale, w_up_pool, nsa_out, w_up_nsa, D):
    T = z.shape[0]
    tm = _pick(T, (512, 256, 128))
    tn = _pick(D, (1024, 512, 256, 128))
    assert gp_off % tn == 0 and gn_off % tn == 0 and tm % POOL_HALO == 0
    hb = tm // POOL_HALO
    return pl.pallas_call(
        functools.partial(_pool_merge_kernel, tm=tm),
        grid=(T // tm, D // tn),
        in_specs=[
            pl.BlockSpec((tm, POOL_WIDTH), lambda i, j: (i, 0)),
            pl.BlockSpec((POOL_HALO, POOL_WIDTH), lambda i, j: (jnp.maximum(i * hb - 1, 0), 0)),
            pl.BlockSpec((POOL_GROUPS, POOL_GROUP_WIDTH, POOL_GROUP_WIDTH), lambda i, j: (0, 0, 0)),
            pl.BlockSpec((1, POOL_WIDTH), lambda i, j: (0, 0)),
            pl.BlockSpec((POOL_WIDTH, tn), lambda i, j: (0, j)),
            pl.BlockSpec((tm, Q_WIDTH), lambda i, j: (i, 0)),
            pl.BlockSpec((Q_WIDTH, tn), lambda i, j: (0, j)),
            pl.BlockSpec((tm, tn), lambda i, j: (i, gp_off // tn + j)),
            pl.BlockSpec((tm, tn), lambda i, j: (i, gn_off // tn + j)),
        ],
        out_specs=pl.BlockSpec((tm, tn), lambda i, j: (i, j)),
        out_shape=jax.ShapeDtypeStruct((T, D), BF16),
        scratch_shapes=[pltpu.VMEM((tm, POOL_WIDTH), BF16)],
        compiler_params=_params("parallel", "arbitrary"),
        name="pool_merge",
    )(z, z, pool_w, pool_scale.reshape(1, POOL_WIDTH), w_up_pool, nsa_out, w_up_nsa, z, z)


def _compress_kernel(r_ref, w_ref, pos_ref, o_ref, *, n_cmp, transpose_out):
    half = CMP_STRIDE * HEAD_DIM
    r = r_ref[0]
    w = w_ref[...]
    top = jnp.dot(r, w[:half], preferred_element_type=F32)
    bot = jnp.dot(r, w[half:], preferred_element_type=F32)
    bias = jnp.dot(pos_ref[...], w, preferred_element_type=F32)
    ncp = top.shape[0]
    bot_next = jnp.concatenate([bot[1:], jnp.zeros((1, HEAD_DIM), F32)], axis=0)
    out = top + bot_next + bias
    row = lax.broadcasted_iota(jnp.int32, (ncp, 1), 0)
    out = jnp.where(row < n_cmp, out, 0.0)
    if transpose_out:
        o_ref[0] = out.T.astype(o_ref.dtype)
    else:
        o_ref[0] = out.astype(o_ref.dtype)


def _compress(r, w, pos, n_cmp, transpose_out):
    G, ncp, width = r.shape
    oshape = (G, HEAD_DIM, ncp) if transpose_out else (G, ncp, HEAD_DIM)
    return pl.pallas_call(
        functools.partial(_compress_kernel, n_cmp=n_cmp, transpose_out=transpose_out),
        grid=(G,),
        in_specs=[
            pl.BlockSpec((1, ncp, width), lambda g: (g, 0, 0)),
            pl.BlockSpec((CMP_BLOCK * HEAD_DIM, HEAD_DIM), lambda g: (0, 0)),
            pl.BlockSpec((1, CMP_BLOCK * HEAD_DIM), lambda g: (0, 0)),
        ],
        out_specs=pl.BlockSpec((1,) + oshape[1:], lambda g: (g, 0, 0)),
        out_shape=jax.ShapeDtypeStruct(oshape, BF16),
        compiler_params=_params("parallel"),
        name="nsa_compress",
    )(r, w, pos)


def _extract_topk(a, k, want_vals=False):
    n = a.shape[0]
    idx = lax.broadcasted_iota(jnp.int32, a.shape, 0).astype(F32)
    rank = jnp.full(a.shape, float(k), F32)
    vals = []
    for r in range(k):
        m = jnp.max(a, axis=0, keepdims=True)
        first = jnp.min(jnp.where(a == m, idx, float(n)), axis=0, keepdims=True)
        hit = idx == first
        rank = jnp.where(hit, float(r), rank)
        a = jnp.where(hit, -jnp.inf, a)
        vals.append(m)
    return (rank, vals) if want_vals else rank


def _cmp_select_kernel(slope_ref, q_ref, kc_ref, vct_ref, ovt_ref, o_ref, sel_ref, used_ref, *, tq):
    g = pl.program_id(0)
    i = pl.program_id(1)
    ncp = kc_ref.shape[1]
    ns = ovt_ref.shape[0]
    t = i * tq + lax.broadcasted_iota(jnp.int32, (1, tq), 1)
    n = lax.broadcasted_iota(jnp.int32, (ncp, 1), 0)
    dist = (t - (n * CMP_STRIDE + CMP_BLOCK - 1)).astype(F32)
    valid = dist >= 0.0
    kc = kc_ref[0]
    vct = vct_ref[0]
    q = q_ref[...]
    psum = jnp.zeros((ncp, tq), F32)
    for h in range(HEADS_PER_GROUP):
        qh = (q[:, h * HEAD_DIM:(h + 1) * HEAD_DIM].astype(F32) * (HEAD_DIM ** -0.5)).astype(BF16)
        s = lax.dot_general(kc, qh, _NT, preferred_element_type=F32)
        s = s - slope_ref[g * HEADS_PER_GROUP + h] * dist
        s = jnp.where(valid, s, NEG)
        m = jnp.max(s, axis=0, keepdims=True)
        e = jnp.exp(s - m)
        l = jnp.sum(e, axis=0, keepdims=True)
        p = jnp.where(valid, e / l, 0.0)
        ot = jnp.dot(vct, p.astype(BF16), preferred_element_type=F32)
        o_ref[:, h * HEAD_DIM:(h + 1) * HEAD_DIM] = ot.T.astype(o_ref.dtype)
        psum = psum + p
    p_hi = psum.astype(BF16)
    p_lo = (psum - p_hi.astype(F32)).astype(BF16)
    ovt = ovt_ref[...]
    imp = jnp.dot(ovt, p_hi, preferred_element_type=F32) + jnp.dot(ovt, p_lo, preferred_element_type=F32)
    kk = lax.broadcasted_iota(jnp.int32, (ns, 1), 0)
    cur = jnp.right_shift(t, SLC_BLOCK.bit_length() - 1)
    forced = (kk == 0) | (kk == cur) | (kk == cur - 1)
    score = jnp.where(forced, FORCE_SCORE, jnp.where(kk <= cur, imp, -1.0))
    k_sel = min(SLC_TOPK, ns)
    rank = _extract_topk(score, k_sel)
    sel = jnp.where((rank < float(k_sel)) & (kk <= cur), 1.0, 0.0)
    sel_t = sel.T
    sel_ref[0] = sel_t.astype(sel_ref.dtype)
    used_ref[0, 0] = jnp.max(sel_t, axis=0, keepdims=True)


def _cmp_select(slopes, z, q_off, kc, vct, ovt, T):
    tq = _pick(T, (256, 128))
    ncp = kc.shape[1]
    ns = ovt.shape[0]
    assert q_off % GROUP_Q_WIDTH == 0
    qb = q_off // GROUP_Q_WIDTH
    return pl.pallas_call(
        functools.partial(_cmp_select_kernel, tq=tq),
        grid=(N_KV_GROUPS, T // tq),
        in_specs=[
            pl.BlockSpec(memory_space=pltpu.SMEM),
            pl.BlockSpec((tq, GROUP_Q_WIDTH), lambda g, i: (i, qb + g)),
            pl.BlockSpec((1, ncp, HEAD_DIM), lambda g, i: (g, 0, 0)),
            pl.BlockSpec((1, HEAD_DIM, ncp), lambda g, i: (g, 0, 0)),
            pl.BlockSpec((ns, ncp), lambda g, i: (0, 0)),
        ],
        out_specs=[
            pl.BlockSpec((tq, GROUP_Q_WIDTH), lambda g, i: (i, g)),
            pl.BlockSpec((1, tq, ns), lambda g, i: (g, i, 0)),
            pl.BlockSpec((1, 1, 1, ns), lambda g, i: (g, i, 0, 0)),
        ],
        out_shape=[
            jax.ShapeDtypeStruct((T, Q_WIDTH), BF16),
            jax.ShapeDtypeStruct((N_KV_GROUPS, T, ns), BF16),
            jax.ShapeDtypeStruct((N_KV_GROUPS, T // tq, 1, ns), F32),
        ],
        compiler_params=_params("parallel", "parallel"),
        name="nsa_cmp_select",
    )(slopes, z, kc, vct, ovt)


def _nsa_attend_kernel(slope_ref, flag_ref, q_ref, ks_ref, vs_ref, kw_ref, vw_ref, ocmp_ref, sel_ref, exp_ref,
                       gate_ref, o_ref, q_scr, acc_scr, m_scr, l_scr, *, tq, tk, span):
    g = pl.program_id(0)
    i = pl.program_id(1)
    t0 = i * tq
    hg = HEADS_PER_GROUP
    q = q_ref[...]
    for h in range(hg):
        qh = q[:, h * HEAD_DIM:(h + 1) * HEAD_DIM].astype(F32) * (HEAD_DIM ** -0.5)
        q_scr[h * tq:(h + 1) * tq, :] = qh.astype(BF16)
    q_all = q_scr[...]
    slopes = [slope_ref[g * hg + h] for h in range(hg)]

    sel = sel_ref[0]
    rel = (lax.broadcasted_iota(jnp.int32, (tq, tk), 0) - lax.broadcasted_iota(jnp.int32, (tq, tk), 1)).astype(F32)
    acc_scr[...] = jnp.zeros_like(acc_scr)
    m_scr[...] = jnp.full(m_scr.shape, NEG, F32)
    l_scr[...] = jnp.zeros_like(l_scr)
    n_kv = (t0 + tq + tk - 1) // tk
    flag_base = (g * pl.num_programs(1) + i) * exp_ref.shape[0]

    def kv_step(j, carry):
        @pl.when(flag_ref[flag_base + j] != 0)
        def _():
            s0 = pl.multiple_of(j * tk, tk)
            kt = ks_ref[pl.ds(s0, tk), :]
            vt = vs_ref[pl.ds(s0, tk), :]
            d = rel + (t0 - s0).astype(F32)
            chosen = jnp.dot(sel, exp_ref[j], preferred_element_type=F32)
            allow = (chosen > 0.5) & (d >= 0.0)
            s_all = lax.dot_general(q_all, kt, _NT, preferred_element_type=F32)
            ps = []
            for h in range(hg):
                rows = slice(h * tq, (h + 1) * tq)
                s = s_all[rows] - slopes[h] * d
                s = jnp.where(allow, s, NEG)
                m_old = m_scr[rows, :]
                m_new = jnp.maximum(m_old, jnp.max(s, axis=1, keepdims=True))
                alpha = jnp.exp(m_old - m_new)
                p = jnp.exp(s - m_new)
                l_scr[rows, :] = alpha * l_scr[rows, :] + jnp.sum(p, axis=1, keepdims=True)
                m_scr[rows, :] = m_new
                acc_scr[rows, :] = alpha * acc_scr[rows, :]
                ps.append(p.astype(BF16))
            acc_scr[...] += jnp.dot(jnp.concatenate(ps, axis=0), vt, preferred_element_type=F32)
        return carry

    lax.fori_loop(0, n_kv, kv_step, 0)
    ls = [l_scr[h * tq:(h + 1) * tq, :] for h in range(hg)]

    w0 = pl.multiple_of(jnp.maximum(t0 - WINDOW, 0), 8)
    kw = kw_ref[pl.ds(w0, span), :]
    vw = vw_ref[pl.ds(w0, span), :]
    dw = ((t0 - w0) + lax.broadcasted_iota(jnp.int32, (tq, span), 0)
          - lax.broadcasted_iota(jnp.int32, (tq, span), 1)).astype(F32)
    allow_w = (dw >= 0.0) & (dw < float(WINDOW))
    sw_all = lax.dot_general(q_all, kw, _NT, preferred_element_type=F32)
    pws, lws = [], []
    for h in range(hg):
        s = sw_all[h * tq:(h + 1) * tq] - slopes[h] * dw
        s = jnp.where(allow_w, s, NEG)
        m = jnp.max(s, axis=1, keepdims=True)
        p = jnp.exp(s - m)
        lws.append(jnp.sum(p, axis=1, keepdims=True))
        pws.append(p.astype(BF16))
    ow_all = jnp.dot(jnp.concatenate(pws, axis=0), vw, preferred_element_type=F32)

    gt = jax.nn.sigmoid(gate_ref[...])
    ocmp = ocmp_ref[...].astype(F32)
    for h in range(hg):
        o_c = ocmp[:, h * HEAD_DIM:(h + 1) * HEAD_DIM]
        o_s = acc_scr[h * tq:(h + 1) * tq, :] / ls[h]
        o_w = ow_all[h * tq:(h + 1) * tq] / lws[h]
        o = gt[:, 3 * h:3 * h + 1] * o_c + gt[:, 3 * h + 1:3 * h + 2] * o_s + gt[:, 3 * h + 2:3 * h + 3] * o_w
        o_ref[:, h * HEAD_DIM:(h + 1) * HEAD_DIM] = o.astype(o_ref.dtype)


def _nsa_attend(slopes, flags, z, q_off, kv_off, o_cmp, sel, expand, gates, T):
    tq = _pick(T, (256, 128))
    n_exp, ns, tk = expand.shape
    span = WINDOW + tq
    assert T >= span and q_off % GROUP_Q_WIDTH == 0 and kv_off % HEAD_DIM == 0
    qb = q_off // GROUP_Q_WIDTH
    kvb = kv_off // HEAD_DIM
    G = N_KV_GROUPS

    def kv_spec(branch):
        return pl.BlockSpec((T, HEAD_DIM), lambda g, i: (0, kvb + branch * G + g))

    return pl.pallas_call(
        functools.partial(_nsa_attend_kernel, tq=tq, tk=tk, span=span),
        grid=(G, T // tq),
        in_specs=[
            pl.BlockSpec(memory_space=pltpu.SMEM),
            pl.BlockSpec(memory_space=pltpu.SMEM),
            pl.BlockSpec((tq, GROUP_Q_WIDTH), lambda g, i: (i, qb + g)),
            kv_spec(2), kv_spec(3), kv_spec(4), kv_spec(5),
            pl.BlockSpec((tq, GROUP_Q_WIDTH), lambda g, i: (i, g)),
            pl.BlockSpec((1, tq, ns), lambda g, i: (g, i, 0)),
            pl.BlockSpec((n_exp, ns, tk), lambda g, i: (0, 0, 0)),
            pl.BlockSpec((tq, GATE_LANES), lambda g, i: (i, g)),
        ],
        out_specs=pl.BlockSpec((tq, GROUP_Q_WIDTH), lambda g, i: (i, g)),
        out_shape=jax.ShapeDtypeStruct((T, Q_WIDTH), BF16),
        scratch_shapes=[
            pltpu.VMEM((HEADS_PER_GROUP * tq, HEAD_DIM), BF16),
            pltpu.VMEM((HEADS_PER_GROUP * tq, HEAD_DIM), F32),
            pltpu.VMEM((HEADS_PER_GROUP * tq, 1), F32),
            pltpu.VMEM((HEADS_PER_GROUP * tq, 1), F32),
        ],
        compiler_params=_params("parallel", "arbitrary"),
        name="nsa_attend",
    )(slopes, flags, z, z, z, z, z, o_cmp, sel, expand, gates)


_CAND_PIECES = [(0, 16), (1, 8), (2, 8), (3, 8), (4, 8), (5, 8), (6, 8), (7, 8)]


def _peer_select_kernel(q_ref, k1_ref, k2_ref, cnt_ref, f1_ref, rank2_ref, e2_ref):
    K = PEER_TOPK
    q = q_ref[...].astype(BF16)
    s1 = lax.dot_general(k1_ref[...], q[:, :PEER_HALF], _NT, preferred_element_type=F32)
    s2 = lax.dot_general(k2_ref[...], q[:, PEER_HALF:], _NT, preferred_element_type=F32)
    rank1, v1 = _extract_topk(s1, K, want_vals=True)
    rank2, v2 = _extract_topk(s2, K, want_vals=True)
    v1s = jnp.concatenate(v1, axis=0)
    v2s = jnp.concatenate(v2, axis=0)
    pieces = [v1[r1] + v2s[:L] for r1, L in _CAND_PIECES]
    pieces.append(v1s[8:] + v2[0])
    cand = jnp.concatenate(pieces, axis=0)
    crank = _extract_topk(cand, K)
    chosen = crank < float(K)
    top = v1[0] + v2[0]
    z = jnp.sum(jnp.where(chosen, jnp.exp(cand - top), 0.0), axis=0, keepdims=True)
    chosen_f = jnp.where(chosen, 1.0, 0.0)
    counts = []
    off = 0
    for _, L in _CAND_PIECES:
        counts.append(jnp.sum(chosen_f[off:off + L], axis=0, keepdims=True))
        off += L
    tail = chosen_f[off:off + 8]
    for r in range(8):
        counts.append(tail[r:r + 1])
    cnt = jnp.zeros_like(s1)
    for r1 in range(K):
        cnt = jnp.where(rank1 == float(r1), counts[r1], cnt)
    cnt_ref[0, 0] = cnt
    f1_ref[0, 0] = jnp.exp(s1 - v1[0]) / z
    rank2_ref[0, 0] = rank2
    e2_ref[0, 0] = jnp.exp(s2 - v2[0])


def _peer_select(qp, keys1, keys2):
    T = qp.shape[0]
    tp = _pick(T, (128,))
    shp = jax.ShapeDtypeStruct((PEER_HEADS, T // tp, PEER_NKEYS, tp), F32)
    ospec = pl.BlockSpec((1, 1, PEER_NKEYS, tp), lambda i, h: (h, i, 0, 0))
    return pl.pallas_call(
        _peer_select_kernel,
        grid=(T // tp, PEER_HEADS),
        in_specs=[
            pl.BlockSpec((tp, PEER_QDIM), lambda i, h: (i, h)),
            pl.BlockSpec((PEER_NKEYS, PEER_HALF), lambda i, h: (0, 0)),
            pl.BlockSpec((PEER_NKEYS, PEER_HALF), lambda i, h: (0, 0)),
        ],
        out_specs=[ospec, ospec, ospec, ospec],
        out_shape=[shp, shp, shp, shp],
        compiler_params=_params("parallel", "parallel"),
        name="peer_select",
    )(qp, keys1, keys2)


def _peer_dense_kernel(h_ref, u_ref, vt_ref, cnt_ref, f1_ref, rank2_ref, e2_ref, o_ref,
                       w_scr, acc_scr, *, ipc, tm, out_chunk):
    c = pl.program_id(1)
    nc = pl.num_programs(1)
    nk = PEER_NKEYS
    lt = LANES
    D = acc_scr.shape[0]

    def build_gates(chunk, k, tb):
        i1 = chunk * ipc + k
        lanes = slice(tb * lt, (tb + 1) * lt)
        w = jnp.zeros((nk, lt), F32)
        for h in range(PEER_HEADS):
            crow = cnt_ref[h, tb, pl.ds(i1, 1), :]
            frow = f1_ref[h, tb, pl.ds(i1, 1), :]
            w = w + jnp.where(rank2_ref[h, tb] < crow, e2_ref[h, tb] * frow, 0.0)
        w_scr[k * nk:(k + 1) * nk, lanes] = w

    tiles = [(k, tb) for k in range(ipc) for tb in range(tm // lt)]

    @pl.when(c == 0)
    def _():
        acc_scr[...] = jnp.zeros_like(acc_scr)
        for k, tb in tiles:
            build_gates(c, k, tb)

    at = lax.dot_general(u_ref[...], h_ref[...], _NT, preferred_element_type=F32)
    coef = (w_scr[...] * jax.nn.gelu(at)).astype(BF16)
    for d0 in range(0, D, out_chunk):
        acc_scr[d0:d0 + out_chunk, :] += jnp.dot(vt_ref[d0:d0 + out_chunk, :], coef, preferred_element_type=F32)

    @pl.when(c + 1 < nc)
    def _():
        for k, tb in tiles:
            build_gates(c + 1, k, tb)

    @pl.when(c == nc - 1)
    def _():
        for d0 in range(0, D, out_chunk):
            o_ref[:, d0:d0 + out_chunk] = acc_scr[d0:d0 + out_chunk, :].T


def _peer_dense(h2, u, vt, cnt, f1, rank2, e2):
    T, D = h2.shape
    NE = u.shape[0]
    tm = _pick(T, (512, 256, 128))
    ipc = 4
    ec = ipc * PEER_NKEYS
    once = pl.Buffered(1)
    sel_spec = pl.BlockSpec((PEER_HEADS, tm // LANES, PEER_NKEYS, LANES), lambda i, c: (0, i, 0, 0),
                            pipeline_mode=once)
    return pl.pallas_call(
        functools.partial(_peer_dense_kernel, ipc=ipc, tm=tm, out_chunk=_pick(D, (512, 256, 128))),
        grid=(T // tm, NE // ec),
        in_specs=[
            pl.BlockSpec((tm, D), lambda i, c: (i, 0), pipeline_mode=once),
            pl.BlockSpec((ec, D), lambda i, c: (c, 0)),
            pl.BlockSpec((D, ec), lambda i, c: (0, c)),
            sel_spec, sel_spec, sel_spec, sel_spec,
        ],
        out_specs=pl.BlockSpec((tm, D), lambda i, c: (i, 0), pipeline_mode=once),
        out_shape=jax.ShapeDtypeStruct((T, D), F32),
        scratch_shapes=[pltpu.VMEM((ec, tm), F32), pltpu.VMEM((D, tm), F32)],
        compiler_params=_params("parallel", "arbitrary"),
        name="peer_dense",
    )(h2, u, vt, cnt, f1, rank2, e2)


def _ple_final_kernel(x1_ref, peer_ref, gple_ref, wg_ref, p_ref, wp_ref, gfin_ref, o_ref, r_scr, *, tn, nj):
    j = pl.program_id(1)

    @pl.when(j == 0)
    def _():
        x2 = x1_ref[...] + peer_ref[...]
        o_ref[...] = x2
        ms = jnp.mean(x2 * x2, axis=-1, keepdims=True)
        r_scr[...] = (x2 * lax.rsqrt(ms + RMS_EPS) * gple_ref[...]).astype(BF16)

    gate = jax.nn.sigmoid(jnp.dot(r_scr[...], wg_ref[...], preferred_element_type=F32))
    proj = jnp.dot(p_ref[...].astype(BF16), wp_ref[...], preferred_element_type=F32)
    for jj in range(nj):
        @pl.when(j == jj)
        def _():
            o_ref[:, jj * tn:(jj + 1) * tn] += gate * proj

    @pl.when(j == nj - 1)
    def _():
        x3 = o_ref[...]
        ms = jnp.mean(x3 * x3, axis=-1, keepdims=True)
        o_ref[...] = x3 * lax.rsqrt(ms + RMS_EPS) * gfin_ref[...]


def _ple_final(x1, peer, g_ple, w_gate, p, w_proj, g_final):
    T, D = x1.shape
    tm = _pick(T, (256, 128))
    tn = _pick(D, (1024, 512, 256, 128))
    nj = D // tn
    return pl.pallas_call(
        functools.partial(_ple_final_kernel, tn=tn, nj=nj),
        grid=(T // tm, nj),
        in_specs=[
            pl.BlockSpec((tm, D), lambda i, j: (i, 0)),
            pl.BlockSpec((tm, D), lambda i, j: (i, 0)),
            pl.BlockSpec((1, D), lambda i, j: (0, 0)),
            pl.BlockSpec((D, tn), lambda i, j: (0, j)),
            pl.BlockSpec((tm, PLE_DIM), lambda i, j: (i, 0)),
            pl.BlockSpec((PLE_DIM, tn), lambda i, j: (0, j)),
            pl.BlockSpec((1, D), lambda i, j: (0, 0)),
        ],
        out_specs=pl.BlockSpec((tm, D), lambda i, j: (i, 0)),
        out_shape=jax.ShapeDtypeStruct((T, D), F32),
        scratch_shapes=[pltpu.VMEM((tm, D), BF16)],
        compiler_params=_params("parallel", "arbitrary"),
        name="ple_final",
    )(x1, peer, g_ple.reshape(1, D), w_gate, p, w_proj, g_final.reshape(1, D))


def _alibi_slopes():
    return jnp.asarray(2.0 ** (-8.0 * np.arange(1, N_HEADS + 1) / N_HEADS), dtype=F32)


def _overlap_t(T):
    ncp = T // CMP_STRIDE
    n_cmp = (T - CMP_BLOCK) // CMP_STRIDE + 1
    n_sel = T // SLC_BLOCK
    c_start = np.arange(ncp) * CMP_STRIDE
    s_start = np.arange(n_sel) * SLC_BLOCK
    ov = (c_start[None, :] < s_start[:, None] + SLC_BLOCK) & (c_start[None, :] + CMP_BLOCK > s_start[:, None])
    ov = ov & (np.arange(ncp)[None, :] < n_cmp)
    return jnp.asarray(ov.astype(np.float32), dtype=BF16)


def _expand_blocks(T, tk):
    n_sel = T // SLC_BLOCK
    key_blk = (np.arange(T) // SLC_BLOCK).reshape(T // tk, 1, tk)
    e = key_blk == np.arange(n_sel).reshape(1, n_sel, 1)
    return jnp.asarray(e.astype(np.float32), dtype=BF16)


def _tile_flags(used, n_tiles):
    G, nq, _, ns = used.shape
    return (used.reshape(G, nq, n_tiles, ns // n_tiles).max(axis=-1) > 0.0).astype(jnp.int32).reshape(-1)


def _layer(x, p, norm_mix_g, w_in, pool_w, pool_scale, cmp_pos_k, cmp_pos_v, cmp_w_k, cmp_w_v,
           w_up_pool, w_up_nsa, w_out, norm_ffn_g, peer_w_q, peer_keys1, peer_keys2, peer_u, peer_v,
           norm_ple_g, ple_w_gate, ple_w_proj, norm_final_g):
    T, D = x.shape
    G = N_KV_GROUPS
    q_off = POOL_WIDTH
    kv_off = q_off + Q_WIDTH
    gates_off = kv_off + 6 * KV_WIDTH
    gp_src = gates_off + 3 * N_HEADS
    gn_src = gp_src + D
    gp_off = gates_off
    gn_off = gp_off + D

    w_main = jnp.concatenate([w_in[:, :gates_off], w_in[:, gp_src:]], axis=1).astype(BF16)
    w_g = w_in[:, gates_off:gp_src].reshape(D, G, 3 * HEADS_PER_GROUP)
    w_g = jnp.pad(w_g, ((0, 0), (0, 0), (0, GATE_LANES - 3 * HEADS_PER_GROUP))).reshape(D, G * GATE_LANES)
    w_g = w_g.astype(BF16)

    h = _rmsnorm(x, norm_mix_g)
    z = _mm(h, w_main, BF16, name="in_proj")
    gates = _mm(h, w_g, F32, name="gate_proj")

    ncp = T // CMP_STRIDE
    n_cmp = (T - CMP_BLOCK) // CMP_STRIDE + 1

    def blocks16(col):
        a = z[:, col:col + KV_WIDTH].reshape(ncp, CMP_STRIDE, G, HEAD_DIM)
        return a.transpose(2, 0, 1, 3).reshape(G, ncp, CMP_STRIDE * HEAD_DIM)

    kc = _compress(blocks16(kv_off), cmp_w_k.astype(BF16), cmp_pos_k.reshape(1, -1).astype(BF16), n_cmp, False)
    vct = _compress(blocks16(kv_off + KV_WIDTH), cmp_w_v.astype(BF16), cmp_pos_v.reshape(1, -1).astype(BF16),
                    n_cmp, True)
    slopes = _alibi_slopes()
    o_cmp, sel, used = _cmp_select(slopes, z, q_off, kc, vct, _overlap_t(T), T)
    tk = _pick(T, (512, 256))
    flags = _tile_flags(used, T // tk)
    nsa_out = _nsa_attend(slopes, flags, z, q_off, kv_off, o_cmp, sel, _expand_blocks(T, tk), gates, T)

    merged = _pool_merge(z, gp_off, gn_off, pool_w.astype(BF16), pool_scale, w_up_pool.astype(BF16),
                         nsa_out, w_up_nsa.astype(BF16), D)
    x1 = _mm(merged, w_out.astype(BF16), F32, res=x, name="out_proj")

    h2 = _rmsnorm(x1, norm_ffn_g)
    qp = _mm(h2, peer_w_q.astype(BF16), F32, name="peer_q")
    cnt, f1, rank2, e2 = _peer_select(qp, peer_keys1.astype(BF16), peer_keys2.astype(BF16))
    peer = _peer_dense(h2, peer_u.astype(BF16), peer_v.astype(BF16).T, cnt, f1, rank2, e2)

    return _ple_final(x1, peer, norm_ple_g, ple_w_gate.astype(BF16), p, ple_w_proj.astype(BF16), norm_final_g)


def kernel(x, p, norm_mix_g, w_in, pool_w, pool_scale, cmp_pos_k, cmp_pos_v, cmp_w_k, cmp_w_v, w_up_pool, w_up_nsa, w_out, norm_ffn_g, peer_w_q, peer_keys1, peer_keys2, peer_u, peer_v, norm_ple_g, ple_w_gate, ple_w_proj, norm_final_g):
    B, T, D = x.shape
    depth = w_in.shape[0]
    assert depth == 1, "the final rmsnorm is fused into the layer's last kernel"
    outs = []
    for b in range(B):
        outs.append(_layer(
            x[b], p[0, b], norm_mix_g[0], w_in[0], pool_w[0], pool_scale[0], cmp_pos_k[0], cmp_pos_v[0],
            cmp_w_k[0], cmp_w_v[0], w_up_pool[0], w_up_nsa[0], w_out[0], norm_ffn_g[0], peer_w_q[0],
            peer_keys1[0], peer_keys2[0], peer_u[0], peer_v[0], norm_ple_g[0], ple_w_gate[0], ple_w_proj[0],
            norm_final_g))
    return jnp.stack(outs, axis=0)
```

```python
import functools

import numpy as np
import jax
import jax.numpy as jnp
from jax import lax
from jax.experimental import pallas as pl
from jax.experimental.pallas import tpu as pltpu

F32 = jnp.float32
BF16 = jnp.bfloat16

POOL_GROUPS = 4
POOL_GROUP_WIDTH = 512
POOL_WIDTH = POOL_GROUPS * POOL_GROUP_WIDTH
POOL_WINDOWS = (2, 4, 8, 16)
POOL_HALO = 16
N_HEADS = 16
N_KV_GROUPS = 4
HEADS_PER_GROUP = N_HEADS // N_KV_GROUPS
HEAD_DIM = 128
Q_WIDTH = N_HEADS * HEAD_DIM
KV_WIDTH = N_KV_GROUPS * HEAD_DIM
GROUP_Q_WIDTH = HEADS_PER_GROUP * HEAD_DIM
CMP_BLOCK = 32
CMP_STRIDE = 16
SLC_BLOCK = 64
SLC_TOPK = 16
WINDOW = 512
FORCE_SCORE = 1e4
PEER_HEADS = 8
PEER_NKEYS = 128
PEER_QDIM = 256
PEER_HALF = PEER_QDIM // 2
PEER_TOPK = 16
PEER_CHUNK_KEYS = 4
PLE_DIM = 256
RMS_EPS = 1e-6
NEG = -1e30
GATE_LANES = 128

LANES = 128
V7X_VMEM_BYTES = 64 * 1024 * 1024
VMEM_LIMIT = 56 * 1024 * 1024

_NT = (((1,), (1,)), ((), ()))


def _params(*sem):
    return pltpu.CompilerParams(dimension_semantics=sem, vmem_limit_bytes=VMEM_LIMIT)


def _pick(n, cands):
    for c in cands:
        if n % c == 0:
            return c
    raise ValueError(f"no tile in {cands} divides {n}")


def _rmsnorm_kernel(x_ref, g_ref, o_ref):
    x = x_ref[...]
    ms = jnp.mean(x * x, axis=-1, keepdims=True)
    o_ref[...] = (x * lax.rsqrt(ms + RMS_EPS) * g_ref[...]).astype(o_ref.dtype)


def _rmsnorm(x, g, out_dtype=BF16):
    T, D = x.shape
    tm = _pick(T, (256, 128, 64, 8))
    return pl.pallas_call(
        _rmsnorm_kernel,
        grid=(T // tm,),
        in_specs=[pl.BlockSpec((tm, D), lambda i: (i, 0)), pl.BlockSpec((1, D), lambda i: (0, 0))],
        out_specs=pl.BlockSpec((tm, D), lambda i: (i, 0)),
        out_shape=jax.ShapeDtypeStruct((T, D), out_dtype),
        compiler_params=_params("parallel"),
        name="rmsnorm",
    )(x, g.reshape(1, D))


def _mm_kernel(a_ref, b_ref, *rest, has_res):
    o_ref = rest[-1]
    acc = jnp.dot(a_ref[...], b_ref[...], preferred_element_type=F32)
    if has_res:
        acc = acc + rest[0][...]
    o_ref[...] = acc.astype(o_ref.dtype)


def _mm(a, b, out_dtype, res=None, name="mm"):
    M, K = a.shape
    _, N = b.shape
    tm = _pick(M, (1024, 512, 256, 128))
    tn = _pick(N, (1024, 512, 256, 128))
    in_specs = [pl.BlockSpec((tm, K), lambda j, i: (i, 0)), pl.BlockSpec((K, tn), lambda j, i: (0, j))]
    args = [a, b]
    if res is not None:
        in_specs.append(pl.BlockSpec((tm, tn), lambda j, i: (i, j)))
        args.append(res)
    return pl.pallas_call(
        functools.partial(_mm_kernel, has_res=res is not None),
        grid=(N // tn, M // tm),
        in_specs=in_specs,
        out_specs=pl.BlockSpec((tm, tn), lambda j, i: (i, j)),
        out_shape=jax.ShapeDtypeStruct((M, N), out_dtype),
        compiler_params=_params("parallel", "parallel"),
        name=name,
    )(*args)


def _pool_merge_kernel(u_ref, halo_ref, pw_ref, ps_ref, wp_ref, nsa_ref, wn_ref, gp_ref, gn_ref,
                       o_ref, pool_scr, *, tm):
    i = pl.program_id(0)
    j = pl.program_id(1)

    @pl.when(j == 0)
    def _():
        u = u_ref[...].astype(F32)
        halo = halo_ref[...].astype(F32)
        halo = jnp.where(i == 0, 0.0, halo)
        ext = jnp.concatenate([halo, u], axis=0)
        t = (i * tm + lax.broadcasted_iota(jnp.int32, (tm, 1), 0)).astype(F32)
        for g, w in enumerate(POOL_WINDOWS):
            c0, c1 = g * POOL_GROUP_WIDTH, (g + 1) * POOL_GROUP_WIDTH
            s = ext[:, c0:c1]
            k = 1
            while k < w:
                s = s[:-k] + s[k:]
                k *= 2
            wsum = s[POOL_HALO + 1 - w:]
            cnt = jnp.minimum(t + 1.0, float(w))
            pooled = wsum / cnt - u[:, c0:c1]
            mixed = jnp.dot(pooled.astype(BF16), pw_ref[g], preferred_element_type=F32)
            pool_scr[:, c0:c1] = (mixed * ps_ref[:, c0:c1]).astype(BF16)

    up_pool = jnp.dot(pool_scr[...], wp_ref[...], preferred_element_type=F32)
    up_nsa = jnp.dot(nsa_ref[...], wn_ref[...], preferred_element_type=F32)
    merged = jax.nn.sigmoid(gp_ref[...].astype(F32)) * up_pool + jax.nn.sigmoid(gn_ref[...].astype(F32)) * up_nsa
    o_ref[...] = merged.astype(o_ref.dtype)


def _pool_merge(z, gp_off, gn_off, pool_w, pool_scale, w_up_pool, nsa_out, w_up_nsa, D):
    T = z.shape[0]
    tm = _pick(T, (512, 256, 128))
    tn = _pick(D, (1024, 512, 256, 128))
    assert gp_off % tn == 0 and gn_off % tn == 0 and tm % POOL_HALO == 0
    hb = tm // POOL_HALO
    return pl.pallas_call(
        functools.partial(_pool_merge_kernel, tm=tm),
        grid=(T // tm, D // tn),
        in_specs=[
            pl.BlockSpec((tm, POOL_WIDTH), lambda i, j: (i, 0)),
            pl.BlockSpec((POOL_HALO, POOL_WIDTH), lambda i, j: (jnp.maximum(i * hb - 1, 0), 0)),
            pl.BlockSpec((POOL_GROUPS, POOL_GROUP_WIDTH, POOL_GROUP_WIDTH), lambda i, j: (0, 0, 0)),
            pl.BlockSpec((1, POOL_WIDTH), lambda i, j: (0, 0)),
            pl.BlockSpec((POOL_WIDTH, tn), lambda i, j: (0, j)),
            pl.BlockSpec((tm, Q_WIDTH), lambda i, j: (i, 0)),
            pl.BlockSpec((Q_WIDTH, tn), lambda i, j: (0, j)),
            pl.BlockSpec((tm, tn), lambda i, j: (i, gp_off // tn + j)),
            pl.BlockSpec((tm, tn), lambda i, j: (i, gn_off // tn + j)),
        ],
        out_specs=pl.BlockSpec((tm, tn), lambda i, j: (i, j)),
        out_shape=jax.ShapeDtypeStruct((T, D), BF16),
        scratch_shapes=[pltpu.VMEM((tm, POOL_WIDTH), BF16)],
        compiler_params=_params("parallel", "arbitrary"),
        name="pool_merge",
    )(z, z, pool_w, pool_scale.reshape(1, POOL_WIDTH), w_up_pool, nsa_out, w_up_nsa, z, z)


def _compress_kernel(r_ref, w_ref, pos_ref, o_ref, *, n_cmp, transpose_out):
    half = CMP_STRIDE * HEAD_DIM
    r = r_ref[0]
    w = w_ref[...]
    top = jnp.dot(r, w[:half], preferred_element_type=F32)
    bot = jnp.dot(r, w[half:], preferred_element_type=F32)
    bias = jnp.dot(pos_ref[...], w, preferred_element_type=F32)
    ncp = top.shape[0]
    bot_next = jnp.concatenate([bot[1:], jnp.zeros((1, HEAD_DIM), F32)], axis=0)
    out = top + bot_next + bias
    row = lax.broadcasted_iota(jnp.int32, (ncp, 1), 0)
    out = jnp.where(row < n_cmp, out, 0.0)
    if transpose_out:
        o_ref[0] = out.T.astype(o_ref.dtype)
    else:
        o_ref[0] = out.astype(o_ref.dtype)


def _compress(r, w, pos, n_cmp, transpose_out):
    G, ncp, width = r.shape
    oshape = (G, HEAD_DIM, ncp) if transpose_out else (G, ncp, HEAD_DIM)
    return pl.pallas_call(
        functools.partial(_compress_kernel, n_cmp=n_cmp, transpose_out=transpose_out),
        grid=(G,),
        in_specs=[
            pl.BlockSpec((1, ncp, width), lambda g: (g, 0, 0)),
            pl.BlockSpec((CMP_BLOCK * HEAD_DIM, HEAD_DIM), lambda g: (0, 0)),
            pl.BlockSpec((1, CMP_BLOCK * HEAD_DIM), lambda g: (0, 0)),
        ],
        out_specs=pl.BlockSpec((1,) + oshape[1:], lambda g: (g, 0, 0)),
        out_shape=jax.ShapeDtypeStruct(oshape, BF16),
        compiler_params=_params("parallel"),
        name="nsa_compress",
    )(r, w, pos)


def _extract_topk(a, k, want_vals=False, exact=True):
    n = a.shape[0]
    idx = lax.broadcasted_iota(jnp.int32, a.shape, 0).astype(F32)
    rank = jnp.full(a.shape, float(k), F32)
    vals = []
    for r in range(k):
        m = jnp.max(a, axis=0, keepdims=True)
        hit = a == m
        if exact:
            first = jnp.min(jnp.where(hit, idx, float(n)), axis=0, keepdims=True)
            hit = idx == first
        rank = jnp.where(hit, float(r), rank)
        a = jnp.where(hit, -jnp.inf, a)
        vals.append(m)
    return (rank, vals) if want_vals else rank


def _cmp_select_kernel(slope_ref, q_ref, kc_ref, vct_ref, ovt_ref, o_ref, sel_ref, used_ref, *, tq):
    g = pl.program_id(0)
    i = pl.program_id(1)
    ncp = kc_ref.shape[1]
    ns = ovt_ref.shape[0]
    t = i * tq + lax.broadcasted_iota(jnp.int32, (1, tq), 1)
    n = lax.broadcasted_iota(jnp.int32, (ncp, 1), 0)
    dist = (t - (n * CMP_STRIDE + CMP_BLOCK - 1)).astype(F32)
    valid = dist >= 0.0
    kc = kc_ref[0]
    vct = vct_ref[0]
    q = q_ref[...]
    psum = jnp.zeros((ncp, tq), F32)
    for h in range(HEADS_PER_GROUP):
        qh = (q[:, h * HEAD_DIM:(h + 1) * HEAD_DIM].astype(F32) * (HEAD_DIM ** -0.5)).astype(BF16)
        s = lax.dot_general(kc, qh, _NT, preferred_element_type=F32)
        s = s - slope_ref[g * HEADS_PER_GROUP + h] * dist
        s = jnp.where(valid, s, NEG)
        m = jnp.max(s, axis=0, keepdims=True)
        e = jnp.exp(s - m)
        l = jnp.sum(e, axis=0, keepdims=True)
        p = jnp.where(valid, e / l, 0.0)
        ot = jnp.dot(vct, p.astype(BF16), preferred_element_type=F32)
        o_ref[:, h * HEAD_DIM:(h + 1) * HEAD_DIM] = ot.T.astype(o_ref.dtype)
        psum = psum + p
    p_hi = psum.astype(BF16)
    p_lo = (psum - p_hi.astype(F32)).astype(BF16)
    ovt = ovt_ref[...]
    imp = jnp.dot(ovt, p_hi, preferred_element_type=F32) + jnp.dot(ovt, p_lo, preferred_element_type=F32)
    kk = lax.broadcasted_iota(jnp.int32, (ns, 1), 0)
    cur = jnp.right_shift(t, SLC_BLOCK.bit_length() - 1)
    forced = (kk == 0) | (kk == cur) | (kk == cur - 1)
    score = jnp.where(forced, FORCE_SCORE, jnp.where(kk <= cur, imp, -1.0))
    k_sel = min(SLC_TOPK, ns)
    rank = _extract_topk(score, k_sel)
    sel = jnp.where((rank < float(k_sel)) & (kk <= cur), 1.0, 0.0)
    sel_t = sel.T
    sel_ref[0] = sel_t.astype(sel_ref.dtype)
    used_ref[0, 0] = jnp.max(sel_t, axis=0, keepdims=True)


def _cmp_select(slopes, z, q_off, kc, vct, ovt, T):
    tq = _pick(T, (256, 128))
    ncp = kc.shape[1]
    ns = ovt.shape[0]
    assert q_off % GROUP_Q_WIDTH == 0
    qb = q_off // GROUP_Q_WIDTH
    return pl.pallas_call(
        functools.partial(_cmp_select_kernel, tq=tq),
        grid=(N_KV_GROUPS, T // tq),
        in_specs=[
            pl.BlockSpec(memory_space=pltpu.SMEM),
            pl.BlockSpec((tq, GROUP_Q_WIDTH), lambda g, i: (i, qb + g)),
            pl.BlockSpec((1, ncp, HEAD_DIM), lambda g, i: (g, 0, 0)),
            pl.BlockSpec((1, HEAD_DIM, ncp), lambda g, i: (g, 0, 0)),
            pl.BlockSpec((ns, ncp), lambda g, i: (0, 0)),
        ],
        out_specs=[
            pl.BlockSpec((tq, GROUP_Q_WIDTH), lambda g, i: (i, g)),
            pl.BlockSpec((1, tq, ns), lambda g, i: (g, i, 0)),
            pl.BlockSpec((1, 1, 1, ns), lambda g, i: (g, i, 0, 0)),
        ],
        out_shape=[
            jax.ShapeDtypeStruct((T, Q_WIDTH), BF16),
            jax.ShapeDtypeStruct((N_KV_GROUPS, T, ns), BF16),
            jax.ShapeDtypeStruct((N_KV_GROUPS, T // tq, 1, ns), F32),
        ],
        compiler_params=_params("parallel", "parallel"),
        name="nsa_cmp_select",
    )(slopes, z, kc, vct, ovt)


def _nsa_attend_kernel(slope_ref, tile_ref, count_ref, q_ref, ks_ref, vs_ref, kw_ref, vw_ref, ocmp_ref, sel_ref,
                       exp_ref, gate_ref, o_ref, q_scr, acc_scr, *, tq, tk, span):
    g = pl.program_id(0)
    i = pl.program_id(1)
    t0 = i * tq
    hg = HEADS_PER_GROUP
    q = q_ref[...]
    for h in range(hg):
        qh = q[:, h * HEAD_DIM:(h + 1) * HEAD_DIM].astype(F32) * (HEAD_DIM ** -0.5)
        q_scr[h * tq:(h + 1) * tq, :] = qh.astype(BF16)
    q_all = q_scr[...]
    slopes = [slope_ref[g * hg + h] for h in range(hg)]

    sel = sel_ref[0]
    rel = (lax.broadcasted_iota(jnp.int32, (tq, tk), 0) - lax.broadcasted_iota(jnp.int32, (tq, tk), 1)).astype(F32)
    acc_scr[...] = jnp.zeros_like(acc_scr)
    step = g * pl.num_programs(1) + i
    list_base = step * exp_ref.shape[0]

    def kv_step(n, carry):
        ms, ls = carry
        j = tile_ref[list_base + n]
        s0 = pl.multiple_of(j * tk, tk)
        kt = ks_ref[pl.ds(s0, tk), :]
        vt = vs_ref[pl.ds(s0, tk), :]
        d = rel + (t0 - s0).astype(F32)
        chosen = jnp.dot(sel, exp_ref[j], preferred_element_type=F32)
        allow = (chosen > 0.5) & (d >= 0.0)
        s_all = lax.dot_general(q_all, kt, _NT, preferred_element_type=F32)
        new_ms, new_ls, ps = [], [], []
        for h in range(hg):
            rows = slice(h * tq, (h + 1) * tq)
            s = s_all[rows] - slopes[h] * d
            s = jnp.where(allow, s, NEG)
            m_new = jnp.maximum(ms[h], jnp.max(s, axis=1, keepdims=True))
            alpha = jnp.exp(ms[h] - m_new)
            p = jnp.exp(s - m_new)
            new_ls.append(alpha * ls[h] + jnp.sum(p, axis=1, keepdims=True))
            new_ms.append(m_new)
            acc_scr[rows, :] = alpha * acc_scr[rows, :]
            ps.append(p.astype(BF16))
        acc_scr[...] += jnp.dot(jnp.concatenate(ps, axis=0), vt, preferred_element_type=F32)
        return tuple(new_ms), tuple(new_ls)

    init = (tuple(jnp.full((tq, 1), NEG, F32) for _ in range(hg)),
            tuple(jnp.zeros((tq, 1), F32) for _ in range(hg)))
    ms, ls = lax.fori_loop(0, count_ref[step], kv_step, init)

    w0 = pl.multiple_of(jnp.maximum(t0 - WINDOW, 0), 8)
    kw = kw_ref[pl.ds(w0, span), :]
    vw = vw_ref[pl.ds(w0, span), :]
    dw = ((t0 - w0) + lax.broadcasted_iota(jnp.int32, (tq, span), 0)
          - lax.broadcasted_iota(jnp.int32, (tq, span), 1)).astype(F32)
    allow_w = (dw >= 0.0) & (dw < float(WINDOW))
    sw_all = lax.dot_general(q_all, kw, _NT, preferred_element_type=F32)
    pws, lws = [], []
    for h in range(hg):
        s = sw_all[h * tq:(h + 1) * tq] - slopes[h] * dw
        s = jnp.where(allow_w, s, NEG)
        m = jnp.max(s, axis=1, keepdims=True)
        p = jnp.exp(s - m)
        lws.append(jnp.sum(p, axis=1, keepdims=True))
        pws.append(p.astype(BF16))
    ow_all = jnp.dot(jnp.concatenate(pws, axis=0), vw, preferred_element_type=F32)

    gt = jax.nn.sigmoid(gate_ref[...])
    ocmp = ocmp_ref[...].astype(F32)
    for h in range(hg):
        o_c = ocmp[:, h * HEAD_DIM:(h + 1) * HEAD_DIM]
        o_s = acc_scr[h * tq:(h + 1) * tq, :] / ls[h]
        o_w = ow_all[h * tq:(h + 1) * tq] / lws[h]
        o = gt[:, 3 * h:3 * h + 1] * o_c + gt[:, 3 * h + 1:3 * h + 2] * o_s + gt[:, 3 * h + 2:3 * h + 3] * o_w
        o_ref[:, h * HEAD_DIM:(h + 1) * HEAD_DIM] = o.astype(o_ref.dtype)


def _nsa_attend(slopes, tile_list, tile_count, z, q_off, kv_off, o_cmp, sel, expand, gates, T):
    tq = _pick(T, (256, 128))
    n_exp, ns, tk = expand.shape
    span = WINDOW + tq
    assert T >= span and q_off % GROUP_Q_WIDTH == 0 and kv_off % HEAD_DIM == 0
    qb = q_off // GROUP_Q_WIDTH
    kvb = kv_off // HEAD_DIM
    G = N_KV_GROUPS

    def kv_spec(branch):
        return pl.BlockSpec((T, HEAD_DIM), lambda g, i: (0, kvb + branch * G + g))

    return pl.pallas_call(
        functools.partial(_nsa_attend_kernel, tq=tq, tk=tk, span=span),
        grid=(G, T // tq),
        in_specs=[
            pl.BlockSpec(memory_space=pltpu.SMEM),
            pl.BlockSpec(memory_space=pltpu.SMEM),
            pl.BlockSpec(memory_space=pltpu.SMEM),
            pl.BlockSpec((tq, GROUP_Q_WIDTH), lambda g, i: (i, qb + g)),
            kv_spec(2), kv_spec(3), kv_spec(4), kv_spec(5),
            pl.BlockSpec((tq, GROUP_Q_WIDTH), lambda g, i: (i, g)),
            pl.BlockSpec((1, tq, ns), lambda g, i: (g, i, 0)),
            pl.BlockSpec((n_exp, ns, tk), lambda g, i: (0, 0, 0)),
            pl.BlockSpec((tq, GATE_LANES), lambda g, i: (i, g)),
        ],
        out_specs=pl.BlockSpec((tq, GROUP_Q_WIDTH), lambda g, i: (i, g)),
        out_shape=jax.ShapeDtypeStruct((T, Q_WIDTH), BF16),
        scratch_shapes=[
            pltpu.VMEM((HEADS_PER_GROUP * tq, HEAD_DIM), BF16),
            pltpu.VMEM((HEADS_PER_GROUP * tq, HEAD_DIM), F32),
        ],
        compiler_params=_params("parallel", "arbitrary"),
        name="nsa_attend",
    )(slopes, tile_list, tile_count, z, z, z, z, z, o_cmp, sel, expand, gates)


_CAND_PIECES = [(0, 16), (1, 8), (2, 8), (3, 8), (4, 8), (5, 8), (6, 8), (7, 8)]


def _peer_select_head(q, k1, k2, exact):
    K = PEER_TOPK
    s1 = lax.dot_general(k1, q[:, :PEER_HALF], _NT, preferred_element_type=F32)
    s2 = lax.dot_general(k2, q[:, PEER_HALF:], _NT, preferred_element_type=F32)
    rank1, v1 = _extract_topk(s1, K, want_vals=True, exact=exact)
    rank2, v2 = _extract_topk(s2, K, want_vals=True, exact=exact)
    v1s = jnp.concatenate(v1, axis=0)
    v2s = jnp.concatenate(v2, axis=0)
    pieces = [v1[r1] + v2s[:L] for r1, L in _CAND_PIECES]
    pieces.append(v1s[8:] + v2[0])
    cand = jnp.concatenate(pieces, axis=0)
    crank = _extract_topk(cand, K, exact=exact)
    chosen = crank < float(K)
    top = v1[0] + v2[0]
    z = jnp.sum(jnp.where(chosen, jnp.exp(cand - top), 0.0), axis=0, keepdims=True)
    chosen_f = jnp.where(chosen, 1.0, 0.0)
    counts = []
    off = 0
    for _, L in _CAND_PIECES:
        counts.append(jnp.sum(chosen_f[off:off + L], axis=0, keepdims=True))
        off += L
    tail = chosen_f[off:off + 8]
    for r in range(8):
        counts.append(tail[r:r + 1])
    cnt = jnp.zeros_like(s1)
    for r1 in range(K):
        cnt = jnp.where(rank1 == float(r1), counts[r1], cnt)

    def n_ranked(rank):
        return jnp.sum(jnp.where(rank < float(K), 1.0, 0.0), axis=0, keepdims=True)

    tied = (jnp.abs(n_ranked(rank1) - float(K)) + jnp.abs(n_ranked(rank2) - float(K))
            + jnp.abs(n_ranked(crank) - float(K)))
    return (cnt, jnp.exp(s1 - v1[0]) / z, rank2, jnp.exp(s2 - v2[0])), tied


def _peer_select_kernel(q_ref, k1_ref, k2_ref, cnt_ref, f1_ref, rank2_ref, e2_ref, *, heads):
    k1 = k1_ref[...]
    k2 = k2_ref[...]
    out_refs = (cnt_ref, f1_ref, rank2_ref, e2_ref)

    def run(exact):
        tied = None
        for hh in range(heads):
            q = q_ref[:, hh * PEER_QDIM:(hh + 1) * PEER_QDIM].astype(BF16)
            outs, t = _peer_select_head(q, k1, k2, exact)
            for ref, val in zip(out_refs, outs):
                ref[hh, 0] = val
            tied = t if tied is None else tied + t
        return jnp.max(tied)

    any_tie = run(False)

    @pl.when(any_tie > 0.0)
    def _():
        run(True)


def _peer_select(qp, keys1, keys2):
    T = qp.shape[0]
    tp = _pick(T, (128,))
    heads = 2
    shp = jax.ShapeDtypeStruct((PEER_HEADS, T // tp, PEER_NKEYS, tp), F32)
    ospec = pl.BlockSpec((heads, 1, PEER_NKEYS, tp), lambda i, h: (h, i, 0, 0))
    return pl.pallas_call(
        functools.partial(_peer_select_kernel, heads=heads),
        grid=(T // tp, PEER_HEADS // heads),
        in_specs=[
            pl.BlockSpec((tp, heads * PEER_QDIM), lambda i, h: (i, h)),
            pl.BlockSpec((PEER_NKEYS, PEER_HALF), lambda i, h: (0, 0)),
            pl.BlockSpec((PEER_NKEYS, PEER_HALF), lambda i, h: (0, 0)),
        ],
        out_specs=[ospec, ospec, ospec, ospec],
        out_shape=[shp, shp, shp, shp],
        compiler_params=_params("parallel", "parallel"),
        name="peer_select",
    )(qp, keys1, keys2)


def _peer_dense_kernel(h_ref, u_ref, vt_ref, cnt_ref, f1_ref, rank2_ref, e2_ref, o_ref,
                       w_scr, acc_scr, *, ipc, tm, out_chunk):
    c = pl.program_id(1)
    nc = pl.num_programs(1)
    nk = PEER_NKEYS
    lt = LANES
    D = acc_scr.shape[0]

    def build_gates(chunk, k, tb):
        i1 = chunk * ipc + k
        lanes = slice(tb * lt, (tb + 1) * lt)
        w = jnp.zeros((nk, lt), F32)
        for h in range(PEER_HEADS):
            crow = cnt_ref[h, tb, pl.ds(i1, 1), :]
            frow = f1_ref[h, tb, pl.ds(i1, 1), :]
            w = w + jnp.where(rank2_ref[h, tb] < crow, e2_ref[h, tb] * frow, 0.0)
        w_scr[k * nk:(k + 1) * nk, lanes] = w

    tiles = [(k, tb) for k in range(ipc) for tb in range(tm // lt)]

    @pl.when(c == 0)
    def _():
        acc_scr[...] = jnp.zeros_like(acc_scr)
        for k, tb in tiles:
            build_gates(c, k, tb)

    at = lax.dot_general(u_ref[...], h_ref[...], _NT, preferred_element_type=F32)
    coef = (w_scr[...] * jax.nn.gelu(at)).astype(BF16)
    for d0 in range(0, D, out_chunk):
        acc_scr[d0:d0 + out_chunk, :] += jnp.dot(vt_ref[0, d0:d0 + out_chunk, :], coef, preferred_element_type=F32)

    @pl.when(c + 1 < nc)
    def _():
        for k, tb in tiles:
            build_gates(c + 1, k, tb)

    @pl.when(c == nc - 1)
    def _():
        for d0 in range(0, D, out_chunk):
            o_ref[:, d0:d0 + out_chunk] = acc_scr[d0:d0 + out_chunk, :].T


def _peer_dense(h2, u, vt, cnt, f1, rank2, e2):
    T, D = h2.shape
    NE = u.shape[0]
    tm = _pick(T, (512, 256, 128))
    ipc = PEER_CHUNK_KEYS
    ec = ipc * PEER_NKEYS
    once = pl.Buffered(1)
    sel_spec = pl.BlockSpec((PEER_HEADS, tm // LANES, PEER_NKEYS, LANES), lambda i, c: (0, i, 0, 0),
                            pipeline_mode=once)
    return pl.pallas_call(
        functools.partial(_peer_dense_kernel, ipc=ipc, tm=tm, out_chunk=_pick(D, (512, 256, 128))),
        grid=(T // tm, NE // ec),
        in_specs=[
            pl.BlockSpec((tm, D), lambda i, c: (i, 0), pipeline_mode=once),
            pl.BlockSpec((ec, D), lambda i, c: (c, 0)),
            pl.BlockSpec((1, D, ec), lambda i, c: (c, 0, 0)),
            sel_spec, sel_spec, sel_spec, sel_spec,
        ],
        out_specs=pl.BlockSpec((tm, D), lambda i, c: (i, 0), pipeline_mode=once),
        out_shape=jax.ShapeDtypeStruct((T, D), F32),
        scratch_shapes=[pltpu.VMEM((ec, tm), F32), pltpu.VMEM((D, tm), F32)],
        compiler_params=_params("parallel", "arbitrary"),
        name="peer_dense",
    )(h2, u, vt, cnt, f1, rank2, e2)


def _ple_final_kernel(x1_ref, peer_ref, gple_ref, wg_ref, p_ref, wp_ref, gfin_ref, o_ref, acc_scr, ss_scr,
                      *, tk, nk, tn):
    j = pl.program_id(1)
    D = acc_scr.shape[1]
    x2 = x1_ref[...] + peer_ref[...]
    lhs = (x2 * gple_ref[...]).astype(BF16)
    ssq = jnp.sum(x2 * x2, axis=-1, keepdims=True)

    @pl.when(j == 0)
    def _():
        acc_scr[...] = jnp.zeros_like(acc_scr)
        ss_scr[...] = jnp.zeros_like(ss_scr)

    ss_scr[...] += ssq
    for n0 in range(0, D, tn):
        acc_scr[:, n0:n0 + tn] += jnp.dot(lhs, wg_ref[:, n0:n0 + tn], preferred_element_type=F32)
    for jj in range(nk):
        @pl.when(j == jj)
        def _():
            o_ref[:, jj * tk:(jj + 1) * tk] = x2

    @pl.when(j == nk - 1)
    def _():
        inv = lax.rsqrt(ss_scr[...] / float(D) + RMS_EPS)
        pb = p_ref[...].astype(BF16)
        ss3 = jnp.zeros_like(inv)
        for n0 in range(0, D, tn):
            cols = slice(n0, n0 + tn)
            gate = jax.nn.sigmoid(acc_scr[:, cols] * inv)
            proj = jnp.dot(pb, wp_ref[:, cols], preferred_element_type=F32)
            x3 = o_ref[:, cols] + gate * proj
            o_ref[:, cols] = x3
            ss3 = ss3 + jnp.sum(x3 * x3, axis=-1, keepdims=True)
        inv3 = lax.rsqrt(ss3 / float(D) + RMS_EPS)
        for n0 in range(0, D, tn):
            cols = slice(n0, n0 + tn)
            o_ref[:, cols] = o_ref[:, cols] * inv3 * gfin_ref[:, cols]


def _ple_final(x1, peer, g_ple, w_gate, p, w_proj, g_final):
    T, D = x1.shape
    tm = _pick(T, (512, 256, 128))
    tk = _pick(D, (1024, 512, 256, 128))
    nk = D // tk
    once = pl.Buffered(1)
    return pl.pallas_call(
        functools.partial(_ple_final_kernel, tk=tk, nk=nk, tn=tk),
        grid=(T // tm, nk),
        in_specs=[
            pl.BlockSpec((tm, tk), lambda i, j: (i, j)),
            pl.BlockSpec((tm, tk), lambda i, j: (i, j)),
            pl.BlockSpec((1, tk), lambda i, j: (0, j)),
            pl.BlockSpec((tk, D), lambda i, j: (j, 0)),
            pl.BlockSpec((tm, PLE_DIM), lambda i, j: (i, 0)),
            pl.BlockSpec((PLE_DIM, D), lambda i, j: (0, 0), pipeline_mode=once),
            pl.BlockSpec((1, D), lambda i, j: (0, 0)),
        ],
        out_specs=pl.BlockSpec((tm, D), lambda i, j: (i, 0), pipeline_mode=once),
        out_shape=jax.ShapeDtypeStruct((T, D), F32),
        scratch_shapes=[pltpu.VMEM((tm, D), F32), pltpu.VMEM((tm, 1), F32)],
        compiler_params=_params("parallel", "arbitrary"),
        name="ple_final",
    )(x1, peer, g_ple.reshape(1, D), w_gate, p, w_proj, g_final.reshape(1, D))


def _alibi_slopes():
    return jnp.asarray(2.0 ** (-8.0 * np.arange(1, N_HEADS + 1) / N_HEADS), dtype=F32)


def _overlap_t(T):
    ncp = T // CMP_STRIDE
    n_cmp = (T - CMP_BLOCK) // CMP_STRIDE + 1
    n_sel = T // SLC_BLOCK
    c_start = np.arange(ncp) * CMP_STRIDE
    s_start = np.arange(n_sel) * SLC_BLOCK
    ov = (c_start[None, :] < s_start[:, None] + SLC_BLOCK) & (c_start[None, :] + CMP_BLOCK > s_start[:, None])
    ov = ov & (np.arange(ncp)[None, :] < n_cmp)
    return jnp.asarray(ov.astype(np.float32), dtype=BF16)


def _expand_blocks(T, tk):
    n_sel = T // SLC_BLOCK
    key_blk = (np.arange(T) // SLC_BLOCK).reshape(T // tk, 1, tk)
    e = key_blk == np.arange(n_sel).reshape(1, n_sel, 1)
    return jnp.asarray(e.astype(np.float32), dtype=BF16)


def _tile_lists(used, n_tiles):
    G, nq, _, ns = used.shape
    flags = used.reshape(G, nq, n_tiles, ns // n_tiles).max(axis=-1) > 0.0
    order = jnp.argsort(jnp.where(flags, 0, 1), axis=-1, stable=True)
    return order.astype(jnp.int32).reshape(-1), flags.sum(axis=-1).astype(jnp.int32).reshape(-1)


def _layer(x, p, norm_mix_g, w_in, pool_w, pool_scale, cmp_pos_k, cmp_pos_v, cmp_w_k, cmp_w_v,
           w_up_pool, w_up_nsa, w_out, norm_ffn_g, peer_w_q, peer_keys1, peer_keys2, peer_u, peer_v,
           norm_ple_g, ple_w_gate, ple_w_proj, norm_final_g):
    T, D = x.shape
    G = N_KV_GROUPS
    q_off = POOL_WIDTH
    kv_off = q_off + Q_WIDTH
    gates_off = kv_off + 6 * KV_WIDTH
    gp_src = gates_off + 3 * N_HEADS
    gn_src = gp_src + D
    gp_off = gates_off
    gn_off = gp_off + D

    w_main = jnp.concatenate([w_in[:, :gates_off], w_in[:, gp_src:]], axis=1).astype(BF16)
    w_g = w_in[:, gates_off:gp_src].reshape(D, G, 3 * HEADS_PER_GROUP)
    w_g = jnp.pad(w_g, ((0, 0), (0, 0), (0, GATE_LANES - 3 * HEADS_PER_GROUP))).reshape(D, G * GATE_LANES)
    w_g = w_g.astype(BF16)

    h = _rmsnorm(x, norm_mix_g)
    z = _mm(h, w_main, BF16, name="in_proj")
    gates = _mm(h, w_g, F32, name="gate_proj")

    ncp = T // CMP_STRIDE
    n_cmp = (T - CMP_BLOCK) // CMP_STRIDE + 1

    def blocks16(col):
        a = z[:, col:col + KV_WIDTH].reshape(ncp, CMP_STRIDE, G, HEAD_DIM)
        return a.transpose(2, 0, 1, 3).reshape(G, ncp, CMP_STRIDE * HEAD_DIM)

    kc = _compress(blocks16(kv_off), cmp_w_k.astype(BF16), cmp_pos_k.reshape(1, -1).astype(BF16), n_cmp, False)
    vct = _compress(blocks16(kv_off + KV_WIDTH), cmp_w_v.astype(BF16), cmp_pos_v.reshape(1, -1).astype(BF16),
                    n_cmp, True)
    slopes = _alibi_slopes()
    o_cmp, sel, used = _cmp_select(slopes, z, q_off, kc, vct, _overlap_t(T), T)
    tk = _pick(T, (512, 256))
    tile_list, tile_count = _tile_lists(used, T // tk)
    nsa_out = _nsa_attend(slopes, tile_list, tile_count, z, q_off, kv_off, o_cmp, sel, _expand_blocks(T, tk), gates, T)

    merged = _pool_merge(z, gp_off, gn_off, pool_w.astype(BF16), pool_scale, w_up_pool.astype(BF16),
                         nsa_out, w_up_nsa.astype(BF16), D)
    x1 = _mm(merged, w_out.astype(BF16), F32, res=x, name="out_proj")

    h2 = _rmsnorm(x1, norm_ffn_g)
    qp = _mm(h2, peer_w_q.astype(BF16), F32, name="peer_q")
    cnt, f1, rank2, e2 = _peer_select(qp, peer_keys1.astype(BF16), peer_keys2.astype(BF16))
    ec = PEER_CHUNK_KEYS * PEER_NKEYS
    vt = peer_v.astype(BF16).reshape(-1, ec, D).transpose(0, 2, 1)
    peer = _peer_dense(h2, peer_u.astype(BF16), vt, cnt, f1, rank2, e2)

    return _ple_final(x1, peer, norm_ple_g, ple_w_gate.astype(BF16), p, ple_w_proj.astype(BF16), norm_final_g)


def kernel(x, p, norm_mix_g, w_in, pool_w, pool_scale, cmp_pos_k, cmp_pos_v, cmp_w_k, cmp_w_v, w_up_pool, w_up_nsa, w_out, norm_ffn_g, peer_w_q, peer_keys1, peer_keys2, peer_u, peer_v, norm_ple_g, ple_w_gate, ple_w_proj, norm_final_g):
    B, T, D = x.shape
    depth = w_in.shape[0]
    assert depth == 1, "the final rmsnorm is fused into the layer's last kernel"
    outs = []
    for b in range(B):
        outs.append(_layer(
            x[b], p[0, b], norm_mix_g[0], w_in[0], pool_w[0], pool_scale[0], cmp_pos_k[0], cmp_pos_v[0],
            cmp_w_k[0], cmp_w_v[0], w_up_pool[0], w_up_nsa[0], w_out[0], norm_ffn_g[0], peer_w_q[0],
            peer_keys1[0], peer_keys2[0], peer_u[0], peer_v[0], norm_ple_g[0], ple_w_gate[0], ple_w_proj[0],
            norm_final_g))
    return jnp.stack(outs, axis=0)
```

```python
import functools

import numpy as np
import jax
import jax.numpy as jnp
from jax import lax
from jax.experimental import pallas as pl
from jax.experimental.pallas import tpu as pltpu

F32 = jnp.float32
BF16 = jnp.bfloat16

POOL_GROUPS = 4
POOL_GROUP_WIDTH = 512
POOL_WIDTH = POOL_GROUPS * POOL_GROUP_WIDTH
POOL_WINDOWS = (2, 4, 8, 16)
POOL_HALO = 16
N_HEADS = 16
N_KV_GROUPS = 4
HEADS_PER_GROUP = N_HEADS // N_KV_GROUPS
HEAD_DIM = 128
Q_WIDTH = N_HEADS * HEAD_DIM
KV_WIDTH = N_KV_GROUPS * HEAD_DIM
GROUP_Q_WIDTH = HEADS_PER_GROUP * HEAD_DIM
CMP_BLOCK = 32
CMP_STRIDE = 16
SLC_BLOCK = 64
SLC_TOPK = 16
WINDOW = 512
FORCE_SCORE = 1e4
PEER_HEADS = 8
PEER_NKEYS = 128
PEER_QDIM = 256
PEER_HALF = PEER_QDIM // 2
PEER_TOPK = 16
PEER_CHUNK_KEYS = 4
PLE_DIM = 256
RMS_EPS = 1e-6
NEG = -1e30
GATE_LANES = 128

LANES = 128
V7X_VMEM_BYTES = 64 * 1024 * 1024
VMEM_LIMIT = 56 * 1024 * 1024

_NT = (((1,), (1,)), ((), ()))


def _params(*sem):
    return pltpu.CompilerParams(dimension_semantics=sem, vmem_limit_bytes=VMEM_LIMIT)


def _pick(n, cands):
    for c in cands:
        if n % c == 0:
            return c
    raise ValueError(f"no tile in {cands} divides {n}")


def _rmsnorm_kernel(x_ref, g_ref, o_ref):
    x = x_ref[...]
    ms = jnp.mean(x * x, axis=-1, keepdims=True)
    o_ref[...] = (x * lax.rsqrt(ms + RMS_EPS) * g_ref[...]).astype(o_ref.dtype)


def _rmsnorm(x, g, out_dtype=BF16):
    T, D = x.shape
    tm = _pick(T, (256, 128, 64, 8))
    return pl.pallas_call(
        _rmsnorm_kernel,
        grid=(T // tm,),
        in_specs=[pl.BlockSpec((tm, D), lambda i: (i, 0)), pl.BlockSpec((1, D), lambda i: (0, 0))],
        out_specs=pl.BlockSpec((tm, D), lambda i: (i, 0)),
        out_shape=jax.ShapeDtypeStruct((T, D), out_dtype),
        compiler_params=_params("parallel"),
        name="rmsnorm",
    )(x, g.reshape(1, D))


def _mm_kernel(a_ref, b_ref, *rest, has_res, cast_b):
    if cast_b:
        o_ref, b_scr = rest[-2:]

        @pl.when(pl.program_id(1) == 0)
        def _():
            b_scr[...] = b_ref[...].astype(BF16)

        b = b_scr[...]
    else:
        o_ref = rest[-1]
        b = b_ref[...]
    acc = jnp.dot(a_ref[...], b, preferred_element_type=F32)
    if has_res:
        acc = acc + rest[0][...]
    o_ref[...] = acc.astype(o_ref.dtype)


def _mm(a, b, out_dtype, res=None, n_cols=None, name="mm"):
    M, K = a.shape
    N = b.shape[1] if n_cols is None else n_cols
    cast_b = b.dtype != BF16
    tm = _pick(M, (1024, 512, 256, 128))
    tn = _pick(N, (512, 256, 128) if cast_b else (1024, 512, 256, 128))
    in_specs = [pl.BlockSpec((tm, K), lambda j, i: (i, 0)), pl.BlockSpec((K, tn), lambda j, i: (0, j))]
    args = [a, b]
    if res is not None:
        in_specs.append(pl.BlockSpec((tm, tn), lambda j, i: (i, j)))
        args.append(res)
    return pl.pallas_call(
        functools.partial(_mm_kernel, has_res=res is not None, cast_b=cast_b),
        grid=(N // tn, M // tm),
        in_specs=in_specs,
        out_specs=pl.BlockSpec((tm, tn), lambda j, i: (i, j)),
        out_shape=jax.ShapeDtypeStruct((M, N), out_dtype),
        scratch_shapes=[pltpu.VMEM((K, tn), BF16)] if cast_b else [],
        compiler_params=_params("parallel", "arbitrary" if cast_b else "parallel"),
        name=name,
    )(*args)


def _pool_merge_kernel(u_ref, halo_ref, pw_ref, ps_ref, wp_ref, nsa_ref, wn_ref, gp_ref, gn_ref,
                       o_ref, pool_scr, *, tm):
    i = pl.program_id(0)
    j = pl.program_id(1)

    @pl.when(j == 0)
    def _():
        u = u_ref[...].astype(F32)
        halo = halo_ref[...].astype(F32)
        halo = jnp.where(i == 0, 0.0, halo)
        ext = jnp.concatenate([halo, u], axis=0)
        t = (i * tm + lax.broadcasted_iota(jnp.int32, (tm, 1), 0)).astype(F32)
        for g, w in enumerate(POOL_WINDOWS):
            c0, c1 = g * POOL_GROUP_WIDTH, (g + 1) * POOL_GROUP_WIDTH
            s = ext[:, c0:c1]
            k = 1
            while k < w:
                s = s[:-k] + s[k:]
                k *= 2
            wsum = s[POOL_HALO + 1 - w:]
            cnt = jnp.minimum(t + 1.0, float(w))
            pooled = wsum / cnt - u[:, c0:c1]
            mixed = jnp.dot(pooled.astype(BF16), pw_ref[g], preferred_element_type=F32)
            pool_scr[:, c0:c1] = (mixed * ps_ref[:, c0:c1]).astype(BF16)

    up_pool = jnp.dot(pool_scr[...], wp_ref[...], preferred_element_type=F32)
    up_nsa = jnp.dot(nsa_ref[...], wn_ref[...], preferred_element_type=F32)
    merged = jax.nn.sigmoid(gp_ref[...].astype(F32)) * up_pool + jax.nn.sigmoid(gn_ref[...].astype(F32)) * up_nsa
    o_ref[...] = merged.astype(o_ref.dtype)


def _pool_merge(z, zg, pool_w, pool_scale, w_up_pool, nsa_out, w_up_nsa, D):
    T = z.shape[0]
    tm = _pick(T, (512, 256, 128))
    tn = _pick(D, (1024, 512, 256, 128))
    assert tm % POOL_HALO == 0
    hb = tm // POOL_HALO
    return pl.pallas_call(
        functools.partial(_pool_merge_kernel, tm=tm),
        grid=(T // tm, D // tn),
        in_specs=[
            pl.BlockSpec((tm, POOL_WIDTH), lambda i, j: (i, 0)),
            pl.BlockSpec((POOL_HALO, POOL_WIDTH), lambda i, j: (jnp.maximum(i * hb - 1, 0), 0)),
            pl.BlockSpec((POOL_GROUPS, POOL_GROUP_WIDTH, POOL_GROUP_WIDTH), lambda i, j: (0, 0, 0)),
            pl.BlockSpec((1, POOL_WIDTH), lambda i, j: (0, 0)),
            pl.BlockSpec((POOL_WIDTH, tn), lambda i, j: (0, j)),
            pl.BlockSpec((tm, Q_WIDTH), lambda i, j: (i, 0)),
            pl.BlockSpec((Q_WIDTH, tn), lambda i, j: (0, j)),
            pl.BlockSpec((tm, tn), lambda i, j: (i, j)),
            pl.BlockSpec((tm, tn), lambda i, j: (i, D // tn + j)),
        ],
        out_specs=pl.BlockSpec((tm, tn), lambda i, j: (i, j)),
        out_shape=jax.ShapeDtypeStruct((T, D), BF16),
        scratch_shapes=[pltpu.VMEM((tm, POOL_WIDTH), BF16)],
        compiler_params=_params("parallel", "arbitrary"),
        name="pool_merge",
    )(z, z, pool_w, pool_scale.reshape(1, POOL_WIDTH), w_up_pool, nsa_out, w_up_nsa, zg, zg)


def _compress_kernel(r_ref, w_ref, pos_ref, o_ref, *, n_cmp, transpose_out):
    half = CMP_STRIDE * HEAD_DIM
    r = r_ref[0]
    w = w_ref[...]
    top = jnp.dot(r, w[:half], preferred_element_type=F32)
    bot = jnp.dot(r, w[half:], preferred_element_type=F32)
    bias = jnp.dot(pos_ref[...], w, preferred_element_type=F32)
    ncp = top.shape[0]
    bot_next = jnp.concatenate([bot[1:], jnp.zeros((1, HEAD_DIM), F32)], axis=0)
    out = top + bot_next + bias
    row = lax.broadcasted_iota(jnp.int32, (ncp, 1), 0)
    out = jnp.where(row < n_cmp, out, 0.0)
    if transpose_out:
        o_ref[0] = out.T.astype(o_ref.dtype)
    else:
        o_ref[0] = out.astype(o_ref.dtype)


def _compress(r, w, pos, n_cmp, transpose_out):
    G, ncp, width = r.shape
    oshape = (G, HEAD_DIM, ncp) if transpose_out else (G, ncp, HEAD_DIM)
    return pl.pallas_call(
        functools.partial(_compress_kernel, n_cmp=n_cmp, transpose_out=transpose_out),
        grid=(G,),
        in_specs=[
            pl.BlockSpec((1, ncp, width), lambda g: (g, 0, 0)),
            pl.BlockSpec((CMP_BLOCK * HEAD_DIM, HEAD_DIM), lambda g: (0, 0)),
            pl.BlockSpec((1, CMP_BLOCK * HEAD_DIM), lambda g: (0, 0)),
        ],
        out_specs=pl.BlockSpec((1,) + oshape[1:], lambda g: (g, 0, 0)),
        out_shape=jax.ShapeDtypeStruct(oshape, BF16),
        compiler_params=_params("parallel"),
        name="nsa_compress",
    )(r, w, pos)


def _extract_topk(a, k, want_vals=False, exact=True):
    n = a.shape[0]
    idx = lax.broadcasted_iota(jnp.int32, a.shape, 0).astype(F32)
    rank = jnp.full(a.shape, float(k), F32)
    vals = []
    for r in range(k):
        m = jnp.max(a, axis=0, keepdims=True)
        hit = a == m
        if exact:
            first = jnp.min(jnp.where(hit, idx, float(n)), axis=0, keepdims=True)
            hit = idx == first
        rank = jnp.where(hit, float(r), rank)
        a = jnp.where(hit, -jnp.inf, a)
        vals.append(m)
    return (rank, vals) if want_vals else rank


def _cmp_select_kernel(slope_ref, q_ref, kc_ref, vct_ref, ovt_ref, o_ref, sel_ref, used_ref, *, tq):
    g = pl.program_id(0)
    i = pl.program_id(1)
    ncp = kc_ref.shape[1]
    ns = ovt_ref.shape[0]
    t = i * tq + lax.broadcasted_iota(jnp.int32, (1, tq), 1)
    n = lax.broadcasted_iota(jnp.int32, (ncp, 1), 0)
    dist = (t - (n * CMP_STRIDE + CMP_BLOCK - 1)).astype(F32)
    valid = dist >= 0.0
    kc = kc_ref[0]
    vct = vct_ref[0]
    q = q_ref[...]
    psum = jnp.zeros((ncp, tq), F32)
    for h in range(HEADS_PER_GROUP):
        qh = (q[:, h * HEAD_DIM:(h + 1) * HEAD_DIM].astype(F32) * (HEAD_DIM ** -0.5)).astype(BF16)
        s = lax.dot_general(kc, qh, _NT, preferred_element_type=F32)
        s = s - slope_ref[g * HEADS_PER_GROUP + h] * dist
        s = jnp.where(valid, s, NEG)
        m = jnp.max(s, axis=0, keepdims=True)
        e = jnp.exp(s - m)
        l = jnp.sum(e, axis=0, keepdims=True)
        p = jnp.where(valid, e / l, 0.0)
        ot = jnp.dot(vct, p.astype(BF16), preferred_element_type=F32)
        o_ref[:, h * HEAD_DIM:(h + 1) * HEAD_DIM] = ot.T.astype(o_ref.dtype)
        psum = psum + p
    p_hi = psum.astype(BF16)
    p_lo = (psum - p_hi.astype(F32)).astype(BF16)
    ovt = ovt_ref[...]
    imp = jnp.dot(ovt, p_hi, preferred_element_type=F32) + jnp.dot(ovt, p_lo, preferred_element_type=F32)
    kk = lax.broadcasted_iota(jnp.int32, (ns, 1), 0)
    cur = jnp.right_shift(t, SLC_BLOCK.bit_length() - 1)
    forced = (kk == 0) | (kk == cur) | (kk == cur - 1)
    score = jnp.where(forced, FORCE_SCORE, jnp.where(kk <= cur, imp, -1.0))
    k_sel = min(SLC_TOPK, ns)
    rank = _extract_topk(score, k_sel)
    sel = jnp.where((rank < float(k_sel)) & (kk <= cur), 1.0, 0.0)
    sel_t = sel.T
    sel_ref[0] = sel_t.astype(sel_ref.dtype)
    used_ref[0, 0] = jnp.max(sel_t, axis=0, keepdims=True)


def _cmp_select(slopes, z, q_off, kc, vct, ovt, T):
    tq = _pick(T, (256, 128))
    ncp = kc.shape[1]
    ns = ovt.shape[0]
    assert q_off % GROUP_Q_WIDTH == 0
    qb = q_off // GROUP_Q_WIDTH
    return pl.pallas_call(
        functools.partial(_cmp_select_kernel, tq=tq),
        grid=(N_KV_GROUPS, T // tq),
        in_specs=[
            pl.BlockSpec(memory_space=pltpu.SMEM),
            pl.BlockSpec((tq, GROUP_Q_WIDTH), lambda g, i: (i, qb + g)),
            pl.BlockSpec((1, ncp, HEAD_DIM), lambda g, i: (g, 0, 0)),
            pl.BlockSpec((1, HEAD_DIM, ncp), lambda g, i: (g, 0, 0)),
            pl.BlockSpec((ns, ncp), lambda g, i: (0, 0)),
        ],
        out_specs=[
            pl.BlockSpec((tq, GROUP_Q_WIDTH), lambda g, i: (i, g)),
            pl.BlockSpec((1, tq, ns), lambda g, i: (g, i, 0)),
            pl.BlockSpec((1, 1, 1, ns), lambda g, i: (g, i, 0, 0)),
        ],
        out_shape=[
            jax.ShapeDtypeStruct((T, Q_WIDTH), BF16),
            jax.ShapeDtypeStruct((N_KV_GROUPS, T, ns), BF16),
            jax.ShapeDtypeStruct((N_KV_GROUPS, T // tq, 1, ns), F32),
        ],
        compiler_params=_params("parallel", "parallel"),
        name="nsa_cmp_select",
    )(slopes, z, kc, vct, ovt)


def _nsa_attend_kernel(slope_ref, tile_ref, count_ref, q_ref, ks_ref, vs_ref, kw_ref, vw_ref, ocmp_ref, sel_ref,
                       exp_ref, gate_ref, o_ref, q_scr, acc_scr, *, tq, tk, span):
    g = pl.program_id(0)
    i = pl.program_id(1)
    t0 = i * tq
    hg = HEADS_PER_GROUP
    q = q_ref[...]
    for h in range(hg):
        qh = q[:, h * HEAD_DIM:(h + 1) * HEAD_DIM].astype(F32) * (HEAD_DIM ** -0.5)
        q_scr[h * tq:(h + 1) * tq, :] = qh.astype(BF16)
    q_all = q_scr[...]
    slopes = [slope_ref[g * hg + h] for h in range(hg)]

    sel = sel_ref[0]
    rel = (lax.broadcasted_iota(jnp.int32, (tq, tk), 0) - lax.broadcasted_iota(jnp.int32, (tq, tk), 1)).astype(F32)
    acc_scr[...] = jnp.zeros_like(acc_scr)
    step = g * pl.num_programs(1) + i
    list_base = step * exp_ref.shape[0]

    def kv_step(n, carry):
        ms, ls = carry
        j = tile_ref[list_base + n]
        s0 = pl.multiple_of(j * tk, tk)
        kt = ks_ref[pl.ds(s0, tk), :]
        vt = vs_ref[pl.ds(s0, tk), :]
        d = rel + (t0 - s0).astype(F32)
        chosen = jnp.dot(sel, exp_ref[j], preferred_element_type=F32)
        allow = (chosen > 0.5) & (d >= 0.0)
        s_all = lax.dot_general(q_all, kt, _NT, preferred_element_type=F32)
        kcol = (lax.broadcasted_iota(jnp.int32, (1, tk), 1) + (s0 - t0)).astype(F32)
        new_ms, new_ls, ps = [], [], []
        for h in range(hg):
            rows = slice(h * tq, (h + 1) * tq)
            s = s_all[rows] + slopes[h] * kcol
            s = jnp.where(allow, s, NEG)
            m_new = jnp.maximum(ms[h], jnp.max(s, axis=1, keepdims=True))
            alpha = jnp.exp(ms[h] - m_new)
            p = jnp.exp(s - m_new)
            new_ls.append(alpha * ls[h] + jnp.sum(p, axis=1, keepdims=True))
            new_ms.append(m_new)
            acc_scr[rows, :] = alpha * acc_scr[rows, :]
            ps.append(p.astype(BF16))
        acc_scr[...] += jnp.dot(jnp.concatenate(ps, axis=0), vt, preferred_element_type=F32)
        return tuple(new_ms), tuple(new_ls)

    init = (tuple(jnp.full((tq, 1), NEG, F32) for _ in range(hg)),
            tuple(jnp.zeros((tq, 1), F32) for _ in range(hg)))
    ms, ls = lax.fori_loop(0, count_ref[step], kv_step, init)

    w0 = pl.multiple_of(jnp.maximum(t0 - WINDOW, 0), 8)
    kw = kw_ref[pl.ds(w0, span), :]
    vw = vw_ref[pl.ds(w0, span), :]
    dw = ((t0 - w0) + lax.broadcasted_iota(jnp.int32, (tq, span), 0)
          - lax.broadcasted_iota(jnp.int32, (tq, span), 1)).astype(F32)
    allow_w = (dw >= 0.0) & (dw < float(WINDOW))
    sw_all = lax.dot_general(q_all, kw, _NT, preferred_element_type=F32)
    kcol_w = (lax.broadcasted_iota(jnp.int32, (1, span), 1) + (w0 - t0)).astype(F32)
    pws, lws = [], []
    for h in range(hg):
        s = sw_all[h * tq:(h + 1) * tq] + slopes[h] * kcol_w
        s = jnp.where(allow_w, s, NEG)
        m = jnp.max(s, axis=1, keepdims=True)
        p = jnp.exp(s - m)
        lws.append(jnp.sum(p, axis=1, keepdims=True))
        pws.append(p.astype(BF16))
    ow_all = jnp.dot(jnp.concatenate(pws, axis=0), vw, preferred_element_type=F32)

    gt = jax.nn.sigmoid(gate_ref[...])
    ocmp = ocmp_ref[...].astype(F32)
    for h in range(hg):
        o_c = ocmp[:, h * HEAD_DIM:(h + 1) * HEAD_DIM]
        o_s = acc_scr[h * tq:(h + 1) * tq, :] / ls[h]
        o_w = ow_all[h * tq:(h + 1) * tq] / lws[h]
        o = gt[:, 3 * h:3 * h + 1] * o_c + gt[:, 3 * h + 1:3 * h + 2] * o_s + gt[:, 3 * h + 2:3 * h + 3] * o_w
        o_ref[:, h * HEAD_DIM:(h + 1) * HEAD_DIM] = o.astype(o_ref.dtype)


def _nsa_attend(slopes, tile_list, tile_count, z, q_off, kv_off, o_cmp, sel, expand, gates, T):
    tq = _pick(T, (256, 128))
    n_exp, ns, tk = expand.shape
    span = WINDOW + tq
    assert T >= span and q_off % GROUP_Q_WIDTH == 0 and kv_off % HEAD_DIM == 0
    qb = q_off // GROUP_Q_WIDTH
    kvb = kv_off // HEAD_DIM
    G = N_KV_GROUPS

    def kv_spec(branch):
        return pl.BlockSpec((T, HEAD_DIM), lambda g, i: (0, kvb + branch * G + g))

    return pl.pallas_call(
        functools.partial(_nsa_attend_kernel, tq=tq, tk=tk, span=span),
        grid=(G, T // tq),
        in_specs=[
            pl.BlockSpec(memory_space=pltpu.SMEM),
            pl.BlockSpec(memory_space=pltpu.SMEM),
            pl.BlockSpec(memory_space=pltpu.SMEM),
            pl.BlockSpec((tq, GROUP_Q_WIDTH), lambda g, i: (i, qb + g)),
            kv_spec(2), kv_spec(3), kv_spec(4), kv_spec(5),
            pl.BlockSpec((tq, GROUP_Q_WIDTH), lambda g, i: (i, g)),
            pl.BlockSpec((1, tq, ns), lambda g, i: (g, i, 0)),
            pl.BlockSpec((n_exp, ns, tk), lambda g, i: (0, 0, 0)),
            pl.BlockSpec((tq, GATE_LANES), lambda g, i: (i, g)),
        ],
        out_specs=pl.BlockSpec((tq, GROUP_Q_WIDTH), lambda g, i: (i, g)),
        out_shape=jax.ShapeDtypeStruct((T, Q_WIDTH), BF16),
        scratch_shapes=[
            pltpu.VMEM((HEADS_PER_GROUP * tq, HEAD_DIM), BF16),
            pltpu.VMEM((HEADS_PER_GROUP * tq, HEAD_DIM), F32),
        ],
        compiler_params=_params("parallel", "arbitrary"),
        name="nsa_attend",
    )(slopes, tile_list, tile_count, z, z, z, z, z, o_cmp, sel, expand, gates)


_CAND_PIECES = [(0, 16), (1, 8), (2, 8), (3, 8), (4, 8), (5, 8), (6, 8), (7, 8)]


def _peer_select_head(q, k1, k2, exact):
    K = PEER_TOPK
    s1 = lax.dot_general(k1, q[:, :PEER_HALF], _NT, preferred_element_type=F32)
    s2 = lax.dot_general(k2, q[:, PEER_HALF:], _NT, preferred_element_type=F32)
    rank1, v1 = _extract_topk(s1, K, want_vals=True, exact=exact)
    rank2, v2 = _extract_topk(s2, K, want_vals=True, exact=exact)
    v1s = jnp.concatenate(v1, axis=0)
    v2s = jnp.concatenate(v2, axis=0)
    pieces = [v1[r1] + v2s[:L] for r1, L in _CAND_PIECES]
    pieces.append(v1s[8:] + v2[0])
    cand = jnp.concatenate(pieces, axis=0)
    crank = _extract_topk(cand, K, exact=exact)
    chosen = crank < float(K)
    top = v1[0] + v2[0]
    z = jnp.sum(jnp.where(chosen, jnp.exp(cand - top), 0.0), axis=0, keepdims=True)
    chosen_f = jnp.where(chosen, 1.0, 0.0)
    counts = []
    off = 0
    for _, L in _CAND_PIECES:
        counts.append(jnp.sum(chosen_f[off:off + L], axis=0, keepdims=True))
        off += L
    tail = chosen_f[off:off + 8]
    for r in range(8):
        counts.append(tail[r:r + 1])
    cnt = jnp.zeros_like(s1)
    for r1 in range(K):
        cnt = jnp.where(rank1 == float(r1), counts[r1], cnt)

    def n_ranked(rank):
        return jnp.sum(jnp.where(rank < float(K), 1.0, 0.0), axis=0, keepdims=True)

    tied = (jnp.abs(n_ranked(rank1) - float(K)) + jnp.abs(n_ranked(rank2) - float(K))
            + jnp.abs(n_ranked(crank) - float(K)))
    return (cnt, jnp.exp(s1 - v1[0]) / z, rank2, jnp.exp(s2 - v2[0])), tied


def _peer_select_kernel(q_ref, k1_ref, k2_ref, cnt_ref, f1_ref, rank2_ref, e2_ref, *, heads):
    k1 = k1_ref[...]
    k2 = k2_ref[...]
    out_refs = (cnt_ref, f1_ref, rank2_ref, e2_ref)

    def run(exact):
        tied = None
        for hh in range(heads):
            q = q_ref[:, hh * PEER_QDIM:(hh + 1) * PEER_QDIM].astype(BF16)
            outs, t = _peer_select_head(q, k1, k2, exact)
            for ref, val in zip(out_refs, outs):
                ref[hh, 0] = val
            tied = t if tied is None else tied + t
        return jnp.max(tied)

    any_tie = run(False)

    @pl.when(any_tie > 0.0)
    def _():
        run(True)


def _peer_select(qp, keys1, keys2):
    T = qp.shape[0]
    tp = _pick(T, (128,))
    heads = 2
    shp = jax.ShapeDtypeStruct((PEER_HEADS, T // tp, PEER_NKEYS, tp), F32)
    ospec = pl.BlockSpec((heads, 1, PEER_NKEYS, tp), lambda i, h: (h, i, 0, 0))
    return pl.pallas_call(
        functools.partial(_peer_select_kernel, heads=heads),
        grid=(T // tp, PEER_HEADS // heads),
        in_specs=[
            pl.BlockSpec((tp, heads * PEER_QDIM), lambda i, h: (i, h)),
            pl.BlockSpec((PEER_NKEYS, PEER_HALF), lambda i, h: (0, 0)),
            pl.BlockSpec((PEER_NKEYS, PEER_HALF), lambda i, h: (0, 0)),
        ],
        out_specs=[ospec, ospec, ospec, ospec],
        out_shape=[shp, shp, shp, shp],
        compiler_params=_params("parallel", "parallel"),
        name="peer_select",
    )(qp, keys1, keys2)


def _peer_dense_kernel(h_ref, u_ref, vt_ref, cnt_ref, f1_ref, rank2_ref, e2_ref, o_ref,
                       w_scr, acc_scr, *, ipc, tm, out_chunk):
    c = pl.program_id(1)
    nc = pl.num_programs(1)
    nk = PEER_NKEYS
    lt = LANES
    D = acc_scr.shape[0]

    def build_gates(chunk, k, tb):
        i1 = chunk * ipc + k
        lanes = slice(tb * lt, (tb + 1) * lt)
        w = jnp.zeros((nk, lt), F32)
        for h in range(PEER_HEADS):
            crow = cnt_ref[h, tb, pl.ds(i1, 1), :]
            frow = f1_ref[h, tb, pl.ds(i1, 1), :]
            w = w + jnp.where(rank2_ref[h, tb] < crow, e2_ref[h, tb] * frow, 0.0)
        w_scr[k * nk:(k + 1) * nk, lanes] = w

    tiles = [(k, tb) for k in range(ipc) for tb in range(tm // lt)]

    @pl.when(c == 0)
    def _():
        acc_scr[...] = jnp.zeros_like(acc_scr)
        for k, tb in tiles:
            build_gates(c, k, tb)

    at = lax.dot_general(u_ref[...], h_ref[...], _NT, preferred_element_type=F32)
    coef = (w_scr[...] * jax.nn.gelu(at)).astype(BF16)
    for d0 in range(0, D, out_chunk):
        acc_scr[d0:d0 + out_chunk, :] += jnp.dot(vt_ref[0, d0:d0 + out_chunk, :], coef, preferred_element_type=F32)

    @pl.when(c + 1 < nc)
    def _():
        for k, tb in tiles:
            build_gates(c + 1, k, tb)

    @pl.when(c == nc - 1)
    def _():
        for d0 in range(0, D, out_chunk):
            o_ref[:, d0:d0 + out_chunk] = acc_scr[d0:d0 + out_chunk, :].T


def _peer_dense(h2, u, vt, cnt, f1, rank2, e2):
    T, D = h2.shape
    NE = u.shape[0]
    tm = _pick(T, (512, 256, 128))
    ipc = PEER_CHUNK_KEYS
    ec = ipc * PEER_NKEYS
    once = pl.Buffered(1)
    sel_spec = pl.BlockSpec((PEER_HEADS, tm // LANES, PEER_NKEYS, LANES), lambda i, c: (0, i, 0, 0),
                            pipeline_mode=once)
    return pl.pallas_call(
        functools.partial(_peer_dense_kernel, ipc=ipc, tm=tm, out_chunk=_pick(D, (512, 256, 128))),
        grid=(T // tm, NE // ec),
        in_specs=[
            pl.BlockSpec((tm, D), lambda i, c: (i, 0), pipeline_mode=once),
            pl.BlockSpec((ec, D), lambda i, c: (c, 0)),
            pl.BlockSpec((1, D, ec), lambda i, c: (c, 0, 0)),
            sel_spec, sel_spec, sel_spec, sel_spec,
        ],
        out_specs=pl.BlockSpec((tm, D), lambda i, c: (i, 0), pipeline_mode=once),
        out_shape=jax.ShapeDtypeStruct((T, D), F32),
        scratch_shapes=[pltpu.VMEM((ec, tm), F32), pltpu.VMEM((D, tm), F32)],
        compiler_params=_params("parallel", "arbitrary"),
        name="peer_dense",
    )(h2, u, vt, cnt, f1, rank2, e2)


def _ple_final_kernel(x1_ref, peer_ref, gple_ref, wg_ref, p_ref, wp_ref, gfin_ref, o_ref, acc_scr, ss_scr,
                      *, tk, nk, tn):
    j = pl.program_id(1)
    D = acc_scr.shape[1]
    x2 = x1_ref[...] + peer_ref[...]
    lhs = (x2 * gple_ref[...]).astype(BF16)
    ssq = jnp.sum(x2 * x2, axis=-1, keepdims=True)

    @pl.when(j == 0)
    def _():
        acc_scr[...] = jnp.zeros_like(acc_scr)
        ss_scr[...] = jnp.zeros_like(ss_scr)

    ss_scr[...] += ssq
    for n0 in range(0, D, tn):
        acc_scr[:, n0:n0 + tn] += jnp.dot(lhs, wg_ref[:, n0:n0 + tn], preferred_element_type=F32)
    for jj in range(nk):
        @pl.when(j == jj)
        def _():
            o_ref[:, jj * tk:(jj + 1) * tk] = x2

    @pl.when(j == nk - 1)
    def _():
        inv = lax.rsqrt(ss_scr[...] / float(D) + RMS_EPS)
        pb = p_ref[...].astype(BF16)
        ss3 = jnp.zeros_like(inv)
        for n0 in range(0, D, tn):
            cols = slice(n0, n0 + tn)
            gate = jax.nn.sigmoid(acc_scr[:, cols] * inv)
            proj = jnp.dot(pb, wp_ref[:, cols], preferred_element_type=F32)
            x3 = o_ref[:, cols] + gate * proj
            o_ref[:, cols] = x3
            ss3 = ss3 + jnp.sum(x3 * x3, axis=-1, keepdims=True)
        inv3 = lax.rsqrt(ss3 / float(D) + RMS_EPS)
        for n0 in range(0, D, tn):
            cols = slice(n0, n0 + tn)
            o_ref[:, cols] = o_ref[:, cols] * inv3 * gfin_ref[:, cols]


def _ple_final(x1, peer, g_ple, w_gate, p, w_proj, g_final):
    T, D = x1.shape
    tm = _pick(T, (512, 256, 128))
    tk = _pick(D, (1024, 512, 256, 128))
    nk = D // tk
    once = pl.Buffered(1)
    return pl.pallas_call(
        functools.partial(_ple_final_kernel, tk=tk, nk=nk, tn=tk),
        grid=(T // tm, nk),
        in_specs=[
            pl.BlockSpec((tm, tk), lambda i, j: (i, j)),
            pl.BlockSpec((tm, tk), lambda i, j: (i, j)),
            pl.BlockSpec((1, tk), lambda i, j: (0, j)),
            pl.BlockSpec((tk, D), lambda i, j: (j, 0)),
            pl.BlockSpec((tm, PLE_DIM), lambda i, j: (i, 0)),
            pl.BlockSpec((PLE_DIM, D), lambda i, j: (0, 0), pipeline_mode=once),
            pl.BlockSpec((1, D), lambda i, j: (0, 0)),
        ],
        out_specs=pl.BlockSpec((tm, D), lambda i, j: (i, 0), pipeline_mode=once),
        out_shape=jax.ShapeDtypeStruct((T, D), F32),
        scratch_shapes=[pltpu.VMEM((tm, D), F32), pltpu.VMEM((tm, 1), F32)],
        compiler_params=_params("parallel", "arbitrary"),
        name="ple_final",
    )(x1, peer, g_ple.reshape(1, D), w_gate, p, w_proj, g_final.reshape(1, D))


def _alibi_slopes():
    return jnp.asarray(2.0 ** (-8.0 * np.arange(1, N_HEADS + 1) / N_HEADS), dtype=F32)


def _overlap_t(T):
    ncp = T // CMP_STRIDE
    n_cmp = (T - CMP_BLOCK) // CMP_STRIDE + 1
    n_sel = T // SLC_BLOCK
    c_start = np.arange(ncp) * CMP_STRIDE
    s_start = np.arange(n_sel) * SLC_BLOCK
    ov = (c_start[None, :] < s_start[:, None] + SLC_BLOCK) & (c_start[None, :] + CMP_BLOCK > s_start[:, None])
    ov = ov & (np.arange(ncp)[None, :] < n_cmp)
    return jnp.asarray(ov.astype(np.float32), dtype=BF16)


def _expand_blocks(T, tk):
    n_sel = T // SLC_BLOCK
    key_blk = (np.arange(T) // SLC_BLOCK).reshape(T // tk, 1, tk)
    e = key_blk == np.arange(n_sel).reshape(1, n_sel, 1)
    return jnp.asarray(e.astype(np.float32), dtype=BF16)


def _tile_lists(used, n_tiles):
    G, nq, _, ns = used.shape
    flags = used.reshape(G, nq, n_tiles, ns // n_tiles).max(axis=-1) > 0.0
    order = jnp.argsort(jnp.where(flags, 0, 1), axis=-1, stable=True)
    return order.astype(jnp.int32).reshape(-1), flags.sum(axis=-1).astype(jnp.int32).reshape(-1)


def _layer(x, p, norm_mix_g, w_in, pool_w, pool_scale, cmp_pos_k, cmp_pos_v, cmp_w_k, cmp_w_v,
           w_up_pool, w_up_nsa, w_out, norm_ffn_g, peer_w_q, peer_keys1, peer_keys2, peer_u, peer_v,
           norm_ple_g, ple_w_gate, ple_w_proj, norm_final_g):
    T, D = x.shape
    G = N_KV_GROUPS
    q_off = POOL_WIDTH
    kv_off = q_off + Q_WIDTH
    gates_off = kv_off + 6 * KV_WIDTH
    gp_src = gates_off + 3 * N_HEADS

    w_merge_gates = w_in[:, gp_src:].astype(BF16)
    w_g = w_in[:, gates_off:gp_src].reshape(D, G, 3 * HEADS_PER_GROUP)
    w_g = jnp.pad(w_g, ((0, 0), (0, 0), (0, GATE_LANES - 3 * HEADS_PER_GROUP))).reshape(D, G * GATE_LANES)
    w_g = w_g.astype(BF16)

    h = _rmsnorm(x, norm_mix_g)
    z = _mm(h, w_in, BF16, n_cols=gates_off, name="in_proj")
    zg = _mm(h, w_merge_gates, BF16, name="merge_gate_proj")
    gates = _mm(h, w_g, F32, name="gate_proj")

    ncp = T // CMP_STRIDE
    n_cmp = (T - CMP_BLOCK) // CMP_STRIDE + 1

    def blocks16(col):
        a = z[:, col:col + KV_WIDTH].reshape(ncp, CMP_STRIDE, G, HEAD_DIM)
        return a.transpose(2, 0, 1, 3).reshape(G, ncp, CMP_STRIDE * HEAD_DIM)

    kc = _compress(blocks16(kv_off), cmp_w_k.astype(BF16), cmp_pos_k.reshape(1, -1).astype(BF16), n_cmp, False)
    vct = _compress(blocks16(kv_off + KV_WIDTH), cmp_w_v.astype(BF16), cmp_pos_v.reshape(1, -1).astype(BF16),
                    n_cmp, True)
    slopes = _alibi_slopes()
    o_cmp, sel, used = _cmp_select(slopes, z, q_off, kc, vct, _overlap_t(T), T)
    tk = _pick(T, (512, 256))
    tile_list, tile_count = _tile_lists(used, T // tk)
    nsa_out = _nsa_attend(slopes, tile_list, tile_count, z, q_off, kv_off, o_cmp, sel, _expand_blocks(T, tk), gates, T)

    merged = _pool_merge(z, zg, pool_w.astype(BF16), pool_scale, w_up_pool.astype(BF16),
                         nsa_out, w_up_nsa.astype(BF16), D)
    x1 = _mm(merged, w_out, F32, res=x, name="out_proj")

    h2 = _rmsnorm(x1, norm_ffn_g)
    qp = _mm(h2, peer_w_q, F32, name="peer_q")
    cnt, f1, rank2, e2 = _peer_select(qp, peer_keys1.astype(BF16), peer_keys2.astype(BF16))
    ec = PEER_CHUNK_KEYS * PEER_NKEYS
    vt = peer_v.astype(BF16).reshape(-1, ec, D).transpose(0, 2, 1)
    peer = _peer_dense(h2, peer_u.astype(BF16), vt, cnt, f1, rank2, e2)

    return _ple_final(x1, peer, norm_ple_g, ple_w_gate.astype(BF16), p, ple_w_proj.astype(BF16), norm_final_g)


def kernel(x, p, norm_mix_g, w_in, pool_w, pool_scale, cmp_pos_k, cmp_pos_v, cmp_w_k, cmp_w_v, w_up_pool, w_up_nsa, w_out, norm_ffn_g, peer_w_q, peer_keys1, peer_keys2, peer_u, peer_v, norm_ple_g, ple_w_gate, ple_w_proj, norm_final_g):
    B, T, D = x.shape
    depth = w_in.shape[0]
    assert depth == 1, "the final rmsnorm is fused into the layer's last kernel"
    outs = []
    for b in range(B):
        outs.append(_layer(
            x[b], p[0, b], norm_mix_g[0], w_in[0], pool_w[0], pool_scale[0], cmp_pos_k[0], cmp_pos_v[0],
            cmp_w_k[0], cmp_w_v[0], w_up_pool[0], w_up_nsa[0], w_out[0], norm_ffn_g[0], peer_w_q[0],
            peer_keys1[0], peer_keys2[0], peer_u[0], peer_v[0], norm_ple_g[0], ple_w_gate[0], ple_w_proj[0],
            norm_final_g))
    return outs[0][None] if B == 1 else jnp.stack(outs, axis=0)
```

```python
import functools

import numpy as np
import jax
import jax.numpy as jnp
from jax import lax
from jax.experimental import pallas as pl
from jax.experimental.pallas import tpu as pltpu

F32 = jnp.float32
BF16 = jnp.bfloat16

POOL_GROUPS = 4
POOL_GROUP_WIDTH = 512
POOL_WIDTH = POOL_GROUPS * POOL_GROUP_WIDTH
POOL_WINDOWS = (2, 4, 8, 16)
POOL_HALO = 16
N_HEADS = 16
N_KV_GROUPS = 4
HEADS_PER_GROUP = N_HEADS // N_KV_GROUPS
HEAD_DIM = 128
Q_WIDTH = N_HEADS * HEAD_DIM
KV_WIDTH = N_KV_GROUPS * HEAD_DIM
GROUP_Q_WIDTH = HEADS_PER_GROUP * HEAD_DIM
CMP_BLOCK = 32
CMP_STRIDE = 16
SLC_BLOCK = 64
SLC_TOPK = 16
WINDOW = 512
FORCE_SCORE = 1e4
PEER_HEADS = 8
PEER_NKEYS = 128
PEER_QDIM = 256
PEER_HALF = PEER_QDIM // 2
PEER_TOPK = 16
PEER_CHUNK_KEYS = 4
PLE_DIM = 256
RMS_EPS = 1e-6
NEG = -1e30
GATE_LANES = 128

LANES = 128
V7X_VMEM_BYTES = 64 * 1024 * 1024
VMEM_LIMIT = V7X_VMEM_BYTES * 7 // 8

_NT = (((1,), (1,)), ((), ()))
_TN = (((0,), (0,)), ((), ()))


def _params(*sem):
    return pltpu.CompilerParams(dimension_semantics=sem, vmem_limit_bytes=VMEM_LIMIT)


def _pick(n, cands):
    for c in cands:
        if n % c == 0:
            return c
    raise ValueError(f"no tile in {cands} divides {n}")


def _rmsnorm_kernel(x_ref, g_ref, o_ref):
    x = x_ref[...]
    ms = jnp.mean(x * x, axis=-1, keepdims=True)
    o_ref[...] = (x * lax.rsqrt(ms + RMS_EPS) * g_ref[...]).astype(o_ref.dtype)


def _rmsnorm(x, g, out_dtype=BF16):
    T, D = x.shape
    tm = _pick(T, (256, 128, 64, 8))
    return pl.pallas_call(
        _rmsnorm_kernel,
        grid=(T // tm,),
        in_specs=[pl.BlockSpec((tm, D), lambda i: (i, 0)), pl.BlockSpec((1, D), lambda i: (0, 0))],
        out_specs=pl.BlockSpec((tm, D), lambda i: (i, 0)),
        out_shape=jax.ShapeDtypeStruct((T, D), out_dtype),
        compiler_params=_params("parallel"),
        name="rmsnorm",
    )(x, g.reshape(1, D))


def _mm_kernel(a_ref, b_ref, *rest, has_res):
    o_ref = rest[-1]
    acc = jnp.dot(a_ref[...], b_ref[...], preferred_element_type=F32)
    if has_res:
        acc = acc + rest[0][...]
    o_ref[...] = acc.astype(o_ref.dtype)


def _mm(a, b, out_dtype, res=None, name="mm"):
    M, K = a.shape
    _, N = b.shape
    tm = _pick(M, (1024, 512, 256, 128))
    tn = _pick(N, (1024, 512, 256, 128))
    in_specs = [pl.BlockSpec((tm, K), lambda j, i: (i, 0)), pl.BlockSpec((K, tn), lambda j, i: (0, j))]
    args = [a, b]
    if res is not None:
        in_specs.append(pl.BlockSpec((tm, tn), lambda j, i: (i, j)))
        args.append(res)
    return pl.pallas_call(
        functools.partial(_mm_kernel, has_res=res is not None),
        grid=(N // tn, M // tm),
        in_specs=in_specs,
        out_specs=pl.BlockSpec((tm, tn), lambda j, i: (i, j)),
        out_shape=jax.ShapeDtypeStruct((M, N), out_dtype),
        compiler_params=_params("parallel", "parallel"),
        name=name,
    )(*args)


def _pool_merge_kernel(u_ref, halo_ref, pw_ref, ps_ref, wp_ref, nsa_ref, wn_ref, gp_ref, gn_ref,
                       o_ref, pool_scr, *, tm):
    i = pl.program_id(0)
    j = pl.program_id(1)

    @pl.when(j == 0)
    def _():
        u = u_ref[...].astype(F32)
        halo = halo_ref[...].astype(F32)
        halo = jnp.where(i == 0, 0.0, halo)
        ext = jnp.concatenate([halo, u], axis=0)
        t = (i * tm + lax.broadcasted_iota(jnp.int32, (tm, 1), 0)).astype(F32)
        for g, w in enumerate(POOL_WINDOWS):
            c0, c1 = g * POOL_GROUP_WIDTH, (g + 1) * POOL_GROUP_WIDTH
            s = ext[:, c0:c1]
            k = 1
            while k < w:
                s = s[:-k] + s[k:]
                k *= 2
            wsum = s[POOL_HALO + 1 - w:]
            cnt = jnp.minimum(t + 1.0, float(w))
            pooled = wsum / cnt - u[:, c0:c1]
            mixed = jnp.dot(pooled.astype(BF16), pw_ref[g], preferred_element_type=F32)
            pool_scr[:, c0:c1] = (mixed * ps_ref[:, c0:c1]).astype(BF16)

    up_pool = jnp.dot(pool_scr[...], wp_ref[...], preferred_element_type=F32)
    up_nsa = jnp.dot(nsa_ref[...], wn_ref[...], preferred_element_type=F32)
    merged = jax.nn.sigmoid(gp_ref[...].astype(F32)) * up_pool + jax.nn.sigmoid(gn_ref[...].astype(F32)) * up_nsa
    o_ref[...] = merged.astype(o_ref.dtype)


def _pool_merge(z, gp_off, gn_off, pool_w, pool_scale, w_up_pool, nsa_out, w_up_nsa, D):
    T = z.shape[0]
    tm = _pick(T, (512, 256, 128))
    tn = _pick(D, (1024, 512, 256, 128))
    assert gp_off % tn == 0 and gn_off % tn == 0 and tm % POOL_HALO == 0
    hb = tm // POOL_HALO
    return pl.pallas_call(
        functools.partial(_pool_merge_kernel, tm=tm),
        grid=(T // tm, D // tn),
        in_specs=[
            pl.BlockSpec((tm, POOL_WIDTH), lambda i, j: (i, 0)),
            pl.BlockSpec((POOL_HALO, POOL_WIDTH), lambda i, j: (jnp.maximum(i * hb - 1, 0), 0)),
            pl.BlockSpec((POOL_GROUPS, POOL_GROUP_WIDTH, POOL_GROUP_WIDTH), lambda i, j: (0, 0, 0)),
            pl.BlockSpec((1, POOL_WIDTH), lambda i, j: (0, 0)),
            pl.BlockSpec((POOL_WIDTH, tn), lambda i, j: (0, j)),
            pl.BlockSpec((tm, Q_WIDTH), lambda i, j: (i, 0)),
            pl.BlockSpec((Q_WIDTH, tn), lambda i, j: (0, j)),
            pl.BlockSpec((tm, tn), lambda i, j: (i, gp_off // tn + j)),
            pl.BlockSpec((tm, tn), lambda i, j: (i, gn_off // tn + j)),
        ],
        out_specs=pl.BlockSpec((tm, tn), lambda i, j: (i, j)),
        out_shape=jax.ShapeDtypeStruct((T, D), BF16),
        scratch_shapes=[pltpu.VMEM((tm, POOL_WIDTH), BF16)],
        compiler_params=_params("parallel", "arbitrary"),
        name="pool_merge",
    )(z, z, pool_w, pool_scale.reshape(1, POOL_WIDTH), w_up_pool, nsa_out, w_up_nsa, z, z)


def _compress_kernel(r_ref, w_ref, pos_ref, o_ref, *, n_cmp, transpose_out):
    half = CMP_STRIDE * HEAD_DIM
    r = r_ref[0]
    w = w_ref[...]
    top = jnp.dot(r, w[:half], preferred_element_type=F32)
    bot = jnp.dot(r, w[half:], preferred_element_type=F32)
    bias = jnp.dot(pos_ref[...], w, preferred_element_type=F32)
    ncp = top.shape[0]
    bot_next = jnp.concatenate([bot[1:], jnp.zeros((1, HEAD_DIM), F32)], axis=0)
    out = top + bot_next + bias
    row = lax.broadcasted_iota(jnp.int32, (ncp, 1), 0)
    out = jnp.where(row < n_cmp, out, 0.0)
    if transpose_out:
        o_ref[0] = out.T.astype(o_ref.dtype)
    else:
        o_ref[0] = out.astype(o_ref.dtype)


def _compress(r, w, pos, n_cmp, transpose_out):
    G, ncp, width = r.shape
    oshape = (G, HEAD_DIM, ncp) if transpose_out else (G, ncp, HEAD_DIM)
    return pl.pallas_call(
        functools.partial(_compress_kernel, n_cmp=n_cmp, transpose_out=transpose_out),
        grid=(G,),
        in_specs=[
            pl.BlockSpec((1, ncp, width), lambda g: (g, 0, 0)),
            pl.BlockSpec((CMP_BLOCK * HEAD_DIM, HEAD_DIM), lambda g: (0, 0)),
            pl.BlockSpec((1, CMP_BLOCK * HEAD_DIM), lambda g: (0, 0)),
        ],
        out_specs=pl.BlockSpec((1,) + oshape[1:], lambda g: (g, 0, 0)),
        out_shape=jax.ShapeDtypeStruct(oshape, BF16),
        compiler_params=_params("parallel"),
        name="nsa_compress",
    )(r, w, pos)


def _extract_topk(a, k, want_vals=False, exact=True):
    n = a.shape[0]
    idx = lax.broadcasted_iota(jnp.int32, a.shape, 0).astype(F32)
    rank = jnp.full(a.shape, float(k), F32)
    vals = []
    for r in range(k):
        m = jnp.max(a, axis=0, keepdims=True)
        hit = a == m
        if exact:
            first = jnp.min(jnp.where(hit, idx, float(n)), axis=0, keepdims=True)
            hit = idx == first
        rank = jnp.where(hit, float(r), rank)
        a = jnp.where(hit, -jnp.inf, a)
        vals.append(m)
    return (rank, vals) if want_vals else rank


def _cmp_select_kernel(slope_ref, q_ref, kc_ref, vct_ref, ovt_ref, o_ref, sel_ref, used_ref, *, tq):
    g = pl.program_id(0)
    i = pl.program_id(1)
    ncp = kc_ref.shape[1]
    ns = ovt_ref.shape[0]
    t = i * tq + lax.broadcasted_iota(jnp.int32, (1, tq), 1)
    n = lax.broadcasted_iota(jnp.int32, (ncp, 1), 0)
    dist = (t - (n * CMP_STRIDE + CMP_BLOCK - 1)).astype(F32)
    valid = dist >= 0.0
    kc = kc_ref[0]
    vct = vct_ref[0]
    q = q_ref[...]
    psum = jnp.zeros((ncp, tq), F32)
    for h in range(HEADS_PER_GROUP):
        qh = (q[:, h * HEAD_DIM:(h + 1) * HEAD_DIM].astype(F32) * (HEAD_DIM ** -0.5)).astype(BF16)
        s = lax.dot_general(kc, qh, _NT, preferred_element_type=F32)
        s = s - slope_ref[g * HEADS_PER_GROUP + h] * dist
        s = jnp.where(valid, s, NEG)
        m = jnp.max(s, axis=0, keepdims=True)
        e = jnp.exp(s - m)
        l = jnp.sum(e, axis=0, keepdims=True)
        p = jnp.where(valid, e / l, 0.0)
        ot = jnp.dot(vct, p.astype(BF16), preferred_element_type=F32)
        o_ref[:, h * HEAD_DIM:(h + 1) * HEAD_DIM] = ot.T.astype(o_ref.dtype)
        psum = psum + p
    p_hi = psum.astype(BF16)
    p_lo = (psum - p_hi.astype(F32)).astype(BF16)
    ovt = ovt_ref[...]
    imp = jnp.dot(ovt, p_hi, preferred_element_type=F32) + jnp.dot(ovt, p_lo, preferred_element_type=F32)
    kk = lax.broadcasted_iota(jnp.int32, (ns, 1), 0)
    cur = jnp.right_shift(t, SLC_BLOCK.bit_length() - 1)
    forced = (kk == 0) | (kk == cur) | (kk == cur - 1)
    score = jnp.where(forced, FORCE_SCORE, jnp.where(kk <= cur, imp, -1.0))
    k_sel = min(SLC_TOPK, ns)
    rank = _extract_topk(score, k_sel)
    sel = jnp.where((rank < float(k_sel)) & (kk <= cur), 1.0, 0.0)
    sel_t = sel.T
    sel_ref[0] = sel_t.astype(sel_ref.dtype)
    used_ref[0, 0] = jnp.max(sel_t, axis=0, keepdims=True)


def _cmp_select(slopes, z, q_off, kc, vct, ovt, T):
    tq = _pick(T, (256, 128))
    ncp = kc.shape[1]
    ns = ovt.shape[0]
    assert q_off % GROUP_Q_WIDTH == 0
    qb = q_off // GROUP_Q_WIDTH
    return pl.pallas_call(
        functools.partial(_cmp_select_kernel, tq=tq),
        grid=(N_KV_GROUPS, T // tq),
        in_specs=[
            pl.BlockSpec(memory_space=pltpu.SMEM),
            pl.BlockSpec((tq, GROUP_Q_WIDTH), lambda g, i: (i, qb + g)),
            pl.BlockSpec((1, ncp, HEAD_DIM), lambda g, i: (g, 0, 0)),
            pl.BlockSpec((1, HEAD_DIM, ncp), lambda g, i: (g, 0, 0)),
            pl.BlockSpec((ns, ncp), lambda g, i: (0, 0)),
        ],
        out_specs=[
            pl.BlockSpec((tq, GROUP_Q_WIDTH), lambda g, i: (i, g)),
            pl.BlockSpec((1, tq, ns), lambda g, i: (g, i, 0)),
            pl.BlockSpec((1, 1, 1, ns), lambda g, i: (g, i, 0, 0)),
        ],
        out_shape=[
            jax.ShapeDtypeStruct((T, Q_WIDTH), BF16),
            jax.ShapeDtypeStruct((N_KV_GROUPS, T, ns), BF16),
            jax.ShapeDtypeStruct((N_KV_GROUPS, T // tq, 1, ns), F32),
        ],
        compiler_params=_params("parallel", "parallel"),
        name="nsa_cmp_select",
    )(slopes, z, kc, vct, ovt)


def _nsa_attend_kernel(slope_ref, tile_ref, count_ref, q_ref, ks_ref, vs_ref, kw_ref, vw_ref, ocmp_ref, sel_ref,
                       exp_ref, gate_ref, o_ref, q_scr, acc_scr, *, tq, tk, span):
    g = pl.program_id(0)
    i = pl.program_id(1)
    t0 = i * tq
    hg = HEADS_PER_GROUP
    q = q_ref[...]
    for h in range(hg):
        qh = q[:, h * HEAD_DIM:(h + 1) * HEAD_DIM].astype(F32) * (HEAD_DIM ** -0.5)
        q_scr[h * tq:(h + 1) * tq, :] = qh.astype(BF16)
    q_all = q_scr[...]
    slopes = [slope_ref[g * hg + h] for h in range(hg)]

    sel = sel_ref[0]
    rel = (lax.broadcasted_iota(jnp.int32, (tq, tk), 0) - lax.broadcasted_iota(jnp.int32, (tq, tk), 1)).astype(F32)
    acc_scr[...] = jnp.zeros_like(acc_scr)
    step = g * pl.num_programs(1) + i
    list_base = step * exp_ref.shape[0]

    def kv_step(n, carry):
        ms, ls = carry
        j = tile_ref[list_base + n]
        s0 = pl.multiple_of(j * tk, tk)
        kt = ks_ref[pl.ds(s0, tk), :]
        vt = vs_ref[pl.ds(s0, tk), :]
        d = rel + (t0 - s0).astype(F32)
        chosen = jnp.dot(sel, exp_ref[j], preferred_element_type=F32)
        allow = (chosen > 0.5) & (d >= 0.0)
        s_all = lax.dot_general(q_all, kt, _NT, preferred_element_type=F32)
        kcol = (lax.broadcasted_iota(jnp.int32, (1, tk), 1) + (s0 - t0)).astype(F32)
        new_ms, new_ls, ps = [], [], []
        for h in range(hg):
            rows = slice(h * tq, (h + 1) * tq)
            s = s_all[rows] + slopes[h] * kcol
            s = jnp.where(allow, s, NEG)
            m_new = jnp.maximum(ms[h], jnp.max(s, axis=1, keepdims=True))
            alpha = jnp.exp(ms[h] - m_new)
            p = jnp.exp(s - m_new)
            new_ls.append(alpha * ls[h] + jnp.sum(p, axis=1, keepdims=True))
            new_ms.append(m_new)
            acc_scr[rows, :] = alpha * acc_scr[rows, :]
            ps.append(p.astype(BF16))
        acc_scr[...] += jnp.dot(jnp.concatenate(ps, axis=0), vt, preferred_element_type=F32)
        return tuple(new_ms), tuple(new_ls)

    init = (tuple(jnp.full((tq, 1), NEG, F32) for _ in range(hg)),
            tuple(jnp.zeros((tq, 1), F32) for _ in range(hg)))
    ms, ls = lax.fori_loop(0, count_ref[step], kv_step, init)

    w0 = pl.multiple_of(jnp.maximum(t0 - WINDOW, 0), 8)
    kw = kw_ref[pl.ds(w0, span), :]
    vw = vw_ref[pl.ds(w0, span), :]
    dw = ((t0 - w0) + lax.broadcasted_iota(jnp.int32, (tq, span), 0)
          - lax.broadcasted_iota(jnp.int32, (tq, span), 1)).astype(F32)
    allow_w = (dw >= 0.0) & (dw < float(WINDOW))
    sw_all = lax.dot_general(q_all, kw, _NT, preferred_element_type=F32)
    kcol_w = (lax.broadcasted_iota(jnp.int32, (1, span), 1) + (w0 - t0)).astype(F32)
    pws, lws = [], []
    for h in range(hg):
        s = sw_all[h * tq:(h + 1) * tq] + slopes[h] * kcol_w
        s = jnp.where(allow_w, s, NEG)
        m = jnp.max(s, axis=1, keepdims=True)
        p = jnp.exp(s - m)
        lws.append(jnp.sum(p, axis=1, keepdims=True))
        pws.append(p.astype(BF16))
    ow_all = jnp.dot(jnp.concatenate(pws, axis=0), vw, preferred_element_type=F32)

    gt = jax.nn.sigmoid(gate_ref[...])
    ocmp = ocmp_ref[...].astype(F32)
    for h in range(hg):
        o_c = ocmp[:, h * HEAD_DIM:(h + 1) * HEAD_DIM]
        o_s = acc_scr[h * tq:(h + 1) * tq, :] / ls[h]
        o_w = ow_all[h * tq:(h + 1) * tq] / lws[h]
        o = gt[:, 3 * h:3 * h + 1] * o_c + gt[:, 3 * h + 1:3 * h + 2] * o_s + gt[:, 3 * h + 2:3 * h + 3] * o_w
        o_ref[:, h * HEAD_DIM:(h + 1) * HEAD_DIM] = o.astype(o_ref.dtype)


def _nsa_attend(slopes, tile_list, tile_count, z, q_off, kv_off, o_cmp, sel, expand, gates, T):
    tq = _pick(T, (256, 128))
    n_exp, ns, tk = expand.shape
    span = WINDOW + tq
    assert T >= span and q_off % GROUP_Q_WIDTH == 0 and kv_off % HEAD_DIM == 0
    qb = q_off // GROUP_Q_WIDTH
    kvb = kv_off // HEAD_DIM
    G = N_KV_GROUPS

    def kv_spec(branch):
        return pl.BlockSpec((T, HEAD_DIM), lambda g, i: (0, kvb + branch * G + g))

    return pl.pallas_call(
        functools.partial(_nsa_attend_kernel, tq=tq, tk=tk, span=span),
        grid=(G, T // tq),
        in_specs=[
            pl.BlockSpec(memory_space=pltpu.SMEM),
            pl.BlockSpec(memory_space=pltpu.SMEM),
            pl.BlockSpec(memory_space=pltpu.SMEM),
            pl.BlockSpec((tq, GROUP_Q_WIDTH), lambda g, i: (i, qb + g)),
            kv_spec(2), kv_spec(3), kv_spec(4), kv_spec(5),
            pl.BlockSpec((tq, GROUP_Q_WIDTH), lambda g, i: (i, g)),
            pl.BlockSpec((1, tq, ns), lambda g, i: (g, i, 0)),
            pl.BlockSpec((n_exp, ns, tk), lambda g, i: (0, 0, 0)),
            pl.BlockSpec((tq, GATE_LANES), lambda g, i: (i, g)),
        ],
        out_specs=pl.BlockSpec((tq, GROUP_Q_WIDTH), lambda g, i: (i, g)),
        out_shape=jax.ShapeDtypeStruct((T, Q_WIDTH), BF16),
        scratch_shapes=[
            pltpu.VMEM((HEADS_PER_GROUP * tq, HEAD_DIM), BF16),
            pltpu.VMEM((HEADS_PER_GROUP * tq, HEAD_DIM), F32),
        ],
        compiler_params=_params("parallel", "arbitrary"),
        name="nsa_attend",
    )(slopes, tile_list, tile_count, z, z, z, z, z, o_cmp, sel, expand, gates)


_CAND_PIECES = [(0, 16), (1, 8), (2, 8), (3, 8), (4, 8), (5, 8), (6, 8), (7, 8)]


def _peer_select_head(q, k1, k2, exact):
    K = PEER_TOPK
    s1 = lax.dot_general(k1, q[:, :PEER_HALF], _NT, preferred_element_type=F32)
    s2 = lax.dot_general(k2, q[:, PEER_HALF:], _NT, preferred_element_type=F32)
    rank1, v1 = _extract_topk(s1, K, want_vals=True, exact=exact)
    rank2, v2 = _extract_topk(s2, K, want_vals=True, exact=exact)
    v1s = jnp.concatenate(v1, axis=0)
    v2s = jnp.concatenate(v2, axis=0)
    pieces = [v1[r1] + v2s[:L] for r1, L in _CAND_PIECES]
    pieces.append(v1s[8:] + v2[0])
    cand = jnp.concatenate(pieces, axis=0)
    crank = _extract_topk(cand, K, exact=exact)
    chosen = crank < float(K)
    top = v1[0] + v2[0]
    z = jnp.sum(jnp.where(chosen, jnp.exp(cand - top), 0.0), axis=0, keepdims=True)
    chosen_f = jnp.where(chosen, 1.0, 0.0)
    counts = []
    off = 0
    for _, L in _CAND_PIECES:
        counts.append(jnp.sum(chosen_f[off:off + L], axis=0, keepdims=True))
        off += L
    tail = chosen_f[off:off + 8]
    for r in range(8):
        counts.append(tail[r:r + 1])
    cnt = jnp.zeros_like(s1)
    for r1 in range(K):
        cnt = jnp.where(rank1 == float(r1), counts[r1], cnt)

    def n_ranked(rank):
        return jnp.sum(jnp.where(rank < float(K), 1.0, 0.0), axis=0, keepdims=True)

    tied = (jnp.abs(n_ranked(rank1) - float(K)) + jnp.abs(n_ranked(rank2) - float(K))
            + jnp.abs(n_ranked(crank) - float(K)))
    return (cnt, jnp.exp(s1 - v1[0]) / z, rank2, jnp.exp(s2 - v2[0])), tied


def _peer_select_kernel(q_ref, k1_ref, k2_ref, cnt_ref, f1_ref, rank2_ref, e2_ref, *, heads):
    k1 = k1_ref[...]
    k2 = k2_ref[...]
    out_refs = (cnt_ref, f1_ref, rank2_ref, e2_ref)

    def run(exact):
        tied = None
        for hh in range(heads):
            q = q_ref[:, hh * PEER_QDIM:(hh + 1) * PEER_QDIM].astype(BF16)
            outs, t = _peer_select_head(q, k1, k2, exact)
            for ref, val in zip(out_refs, outs):
                ref[hh, 0] = val
            tied = t if tied is None else tied + t
        return jnp.max(tied)

    any_tie = run(False)

    @pl.when(any_tie > 0.0)
    def _():
        run(True)


def _peer_select(qp, keys1, keys2):
    T = qp.shape[0]
    tp = _pick(T, (128,))
    heads = 2
    shp = jax.ShapeDtypeStruct((PEER_HEADS, T // tp, PEER_NKEYS, tp), F32)
    ospec = pl.BlockSpec((heads, 1, PEER_NKEYS, tp), lambda i, h: (h, i, 0, 0))
    return pl.pallas_call(
        functools.partial(_peer_select_kernel, heads=heads),
        grid=(T // tp, PEER_HEADS // heads),
        in_specs=[
            pl.BlockSpec((tp, heads * PEER_QDIM), lambda i, h: (i, h)),
            pl.BlockSpec((PEER_NKEYS, PEER_HALF), lambda i, h: (0, 0)),
            pl.BlockSpec((PEER_NKEYS, PEER_HALF), lambda i, h: (0, 0)),
        ],
        out_specs=[ospec, ospec, ospec, ospec],
        out_shape=[shp, shp, shp, shp],
        compiler_params=_params("parallel", "parallel"),
        name="peer_select",
    )(qp, keys1, keys2)


def _peer_dense_kernel(h_ref, u_ref, v_ref, cnt_ref, f1_ref, rank2_ref, e2_ref, o_ref,
                       w_scr, acc_scr, at_scr, *, ipc, tm, out_chunk):
    c = pl.program_id(1)
    nc = pl.num_programs(1)
    nk = PEER_NKEYS
    lt = LANES
    D = acc_scr.shape[0]

    def build_gates(chunk, k, tb, tie):
        i1 = chunk * ipc + k
        lanes = slice(tb * lt, (tb + 1) * lt)
        w = jnp.zeros((nk, lt), F32)
        for h in range(PEER_HEADS):
            crow = cnt_ref[h, tb, pl.ds(i1, 1), :]
            if h == 0:
                crow = crow + tie[:, lanes]
            frow = f1_ref[h, tb, pl.ds(i1, 1), :]
            w = w + jnp.where(rank2_ref[h, tb] < crow, e2_ref[h, tb] * frow, 0.0)
        w_scr[k * nk:(k + 1) * nk, lanes] = w

    tiles = [(k, tb) for k in range(ipc) for tb in range(tm // lt)]
    early, late = tiles[:len(tiles) // 2], tiles[len(tiles) // 2:]
    zero_row = jnp.zeros((1, tm), F32)

    @pl.when(c == 0)
    def _():
        acc_scr[...] = jnp.zeros_like(acc_scr)
        for k, tb in early:
            build_gates(c, k, tb, zero_row)

    half = (ipc * nk) // 2
    h = h_ref[...]
    at_scr[:half, :] = lax.dot_general(u_ref[:half, :], h, _NT, preferred_element_type=F32)
    tie = at_scr[0:1, :] * 0.0
    at_scr[half:, :] = lax.dot_general(u_ref[half:, :], h, _NT, preferred_element_type=F32)
    for k, tb in late:
        build_gates(c, k, tb, tie)
    coef = (w_scr[...] * jax.nn.gelu(at_scr[...])).astype(BF16)
    nxt = jnp.minimum(c + 1, nc - 1)
    n_out = D // out_chunk
    for j in range(n_out):
        d0 = j * out_chunk
        acc_scr[d0:d0 + out_chunk, :] += lax.dot_general(v_ref[:, d0:d0 + out_chunk], coef, _TN,
                                                         preferred_element_type=F32)
        tie2 = acc_scr[d0:d0 + 1, :] * 0.0
        for k, tb in early[j::n_out]:
            build_gates(nxt, k, tb, tie2)

    @pl.when(c == nc - 1)
    def _():
        for d0 in range(0, D, out_chunk):
            o_ref[:, d0:d0 + out_chunk] = acc_scr[d0:d0 + out_chunk, :].T


def _peer_dense(h2, u, v, cnt, f1, rank2, e2):
    T, D = h2.shape
    NE = u.shape[0]
    tm = _pick(T, (512, 256, 128))
    ipc = PEER_CHUNK_KEYS
    ec = ipc * PEER_NKEYS
    once = pl.Buffered(1)
    sel_spec = pl.BlockSpec((PEER_HEADS, tm // LANES, PEER_NKEYS, LANES), lambda i, c: (0, i, 0, 0),
                            pipeline_mode=once)
    return pl.pallas_call(
        functools.partial(_peer_dense_kernel, ipc=ipc, tm=tm, out_chunk=_pick(D, (512, 256, 128))),
        grid=(T // tm, NE // ec),
        in_specs=[
            pl.BlockSpec((tm, D), lambda i, c: (i, 0), pipeline_mode=once),
            pl.BlockSpec((ec, D), lambda i, c: (c, 0)),
            pl.BlockSpec((ec, D), lambda i, c: (c, 0)),
            sel_spec, sel_spec, sel_spec, sel_spec,
        ],
        out_specs=pl.BlockSpec((tm, D), lambda i, c: (i, 0), pipeline_mode=once),
        out_shape=jax.ShapeDtypeStruct((T, D), F32),
        scratch_shapes=[pltpu.VMEM((ec, tm), F32), pltpu.VMEM((D, tm), F32), pltpu.VMEM((ec, tm), F32)],
        compiler_params=_params("parallel", "arbitrary"),
        name="peer_dense",
    )(h2, u, v, cnt, f1, rank2, e2)


def _ple_final_kernel(x1_ref, peer_ref, gple_ref, wg_ref, p_ref, wp_ref, gfin_ref, o_ref, acc_scr, ss_scr,
                      *, tk, nk, tn):
    j = pl.program_id(1)
    D = acc_scr.shape[1]
    x2 = x1_ref[...] + peer_ref[...]
    lhs = (x2 * gple_ref[...]).astype(BF16)
    ssq = jnp.sum(x2 * x2, axis=-1, keepdims=True)

    @pl.when(j == 0)
    def _():
        acc_scr[...] = jnp.zeros_like(acc_scr)
        ss_scr[...] = jnp.zeros_like(ss_scr)

    ss_scr[...] += ssq
    for n0 in range(0, D, tn):
        acc_scr[:, n0:n0 + tn] += jnp.dot(lhs, wg_ref[:, n0:n0 + tn], preferred_element_type=F32)
    for jj in range(nk):
        @pl.when(j == jj)
        def _():
            o_ref[:, jj * tk:(jj + 1) * tk] = x2

    @pl.when(j == nk - 1)
    def _():
        inv = lax.rsqrt(ss_scr[...] / float(D) + RMS_EPS)
        pb = p_ref[...].astype(BF16)
        ss3 = jnp.zeros_like(inv)
        for n0 in range(0, D, tn):
            cols = slice(n0, n0 + tn)
            gate = jax.nn.sigmoid(acc_scr[:, cols] * inv)
            proj = jnp.dot(pb, wp_ref[:, cols], preferred_element_type=F32)
            x3 = o_ref[:, cols] + gate * proj
            o_ref[:, cols] = x3
            ss3 = ss3 + jnp.sum(x3 * x3, axis=-1, keepdims=True)
        inv3 = lax.rsqrt(ss3 / float(D) + RMS_EPS)
        for n0 in range(0, D, tn):
            cols = slice(n0, n0 + tn)
            o_ref[:, cols] = o_ref[:, cols] * inv3 * gfin_ref[:, cols]


def _ple_final(x1, peer, g_ple, w_gate, p, w_proj, g_final):
    T, D = x1.shape
    tm = _pick(T, (512, 256, 128))
    tk = _pick(D, (1024, 512, 256, 128))
    nk = D // tk
    once = pl.Buffered(1)
    return pl.pallas_call(
        functools.partial(_ple_final_kernel, tk=tk, nk=nk, tn=tk),
        grid=(T // tm, nk),
        in_specs=[
            pl.BlockSpec((tm, tk), lambda i, j: (i, j)),
            pl.BlockSpec((tm, tk), lambda i, j: (i, j)),
            pl.BlockSpec((1, tk), lambda i, j: (0, j)),
            pl.BlockSpec((tk, D), lambda i, j: (j, 0)),
            pl.BlockSpec((tm, PLE_DIM), lambda i, j: (i, 0)),
            pl.BlockSpec((PLE_DIM, D), lambda i, j: (0, 0), pipeline_mode=once),
            pl.BlockSpec((1, D), lambda i, j: (0, 0)),
        ],
        out_specs=pl.BlockSpec((tm, D), lambda i, j: (i, 0), pipeline_mode=once),
        out_shape=jax.ShapeDtypeStruct((T, D), F32),
        scratch_shapes=[pltpu.VMEM((tm, D), F32), pltpu.VMEM((tm, 1), F32)],
        compiler_params=_params("parallel", "arbitrary"),
        name="ple_final",
    )(x1, peer, g_ple.reshape(1, D), w_gate, p, w_proj, g_final.reshape(1, D))


def _alibi_slopes():
    return jnp.asarray(2.0 ** (-8.0 * np.arange(1, N_HEADS + 1) / N_HEADS), dtype=F32)


def _overlap_t(T):
    ncp = T // CMP_STRIDE
    n_cmp = (T - CMP_BLOCK) // CMP_STRIDE + 1
    n_sel = T // SLC_BLOCK
    c_start = np.arange(ncp) * CMP_STRIDE
    s_start = np.arange(n_sel) * SLC_BLOCK
    ov = (c_start[None, :] < s_start[:, None] + SLC_BLOCK) & (c_start[None, :] + CMP_BLOCK > s_start[:, None])
    ov = ov & (np.arange(ncp)[None, :] < n_cmp)
    return jnp.asarray(ov.astype(np.float32), dtype=BF16)


def _expand_blocks(T, tk):
    n_sel = T // SLC_BLOCK
    key_blk = (np.arange(T) // SLC_BLOCK).reshape(T // tk, 1, tk)
    e = key_blk == np.arange(n_sel).reshape(1, n_sel, 1)
    return jnp.asarray(e.astype(np.float32), dtype=BF16)


def _tile_lists(used, n_tiles):
    G, nq, _, ns = used.shape
    flags = used.reshape(G, nq, n_tiles, ns // n_tiles).max(axis=-1) > 0.0
    order = jnp.argsort(jnp.where(flags, 0, 1), axis=-1, stable=True)
    return order.astype(jnp.int32).reshape(-1), flags.sum(axis=-1).astype(jnp.int32).reshape(-1)


def _layer(x, p, norm_mix_g, w_in, pool_w, pool_scale, cmp_pos_k, cmp_pos_v, cmp_w_k, cmp_w_v,
           w_up_pool, w_up_nsa, w_out, norm_ffn_g, peer_w_q, peer_keys1, peer_keys2, peer_u, peer_v,
           norm_ple_g, ple_w_gate, ple_w_proj, norm_final_g):
    T, D = x.shape
    G = N_KV_GROUPS
    q_off = POOL_WIDTH
    kv_off = q_off + Q_WIDTH
    gates_off = kv_off + 6 * KV_WIDTH
    gp_src = gates_off + 3 * N_HEADS
    gp_off = gates_off
    gn_off = gp_off + D

    w_main = jnp.concatenate([w_in[:, :gates_off], w_in[:, gp_src:]], axis=1).astype(BF16)
    w_g = w_in[:, gates_off:gp_src].reshape(D, G, 3 * HEADS_PER_GROUP)
    w_g = jnp.pad(w_g, ((0, 0), (0, 0), (0, GATE_LANES - 3 * HEADS_PER_GROUP))).reshape(D, G * GATE_LANES)
    w_g = w_g.astype(BF16)

    h = _rmsnorm(x, norm_mix_g)
    z = _mm(h, w_main, BF16, name="in_proj")
    gates = _mm(h, w_g, F32, name="gate_proj")

    ncp = T // CMP_STRIDE
    n_cmp = (T - CMP_BLOCK) // CMP_STRIDE + 1

    def blocks16(col):
        a = z[:, col:col + KV_WIDTH].reshape(ncp, CMP_STRIDE, G, HEAD_DIM)
        return a.transpose(2, 0, 1, 3).reshape(G, ncp, CMP_STRIDE * HEAD_DIM)

    kc = _compress(blocks16(kv_off), cmp_w_k.astype(BF16), cmp_pos_k.reshape(1, -1).astype(BF16), n_cmp, False)
    vct = _compress(blocks16(kv_off + KV_WIDTH), cmp_w_v.astype(BF16), cmp_pos_v.reshape(1, -1).astype(BF16),
                    n_cmp, True)
    slopes = _alibi_slopes()
    o_cmp, sel, used = _cmp_select(slopes, z, q_off, kc, vct, _overlap_t(T), T)
    tk = _pick(T, (512, 256))
    tile_list, tile_count = _tile_lists(used, T // tk)
    nsa_out = _nsa_attend(slopes, tile_list, tile_count, z, q_off, kv_off, o_cmp, sel, _expand_blocks(T, tk), gates, T)

    merged = _pool_merge(z, gp_off, gn_off, pool_w.astype(BF16), pool_scale, w_up_pool.astype(BF16),
                         nsa_out, w_up_nsa.astype(BF16), D)
    x1 = _mm(merged, w_out.astype(BF16), F32, res=x, name="out_proj")

    h2 = _rmsnorm(x1, norm_ffn_g)
    qp = _mm(h2, peer_w_q.astype(BF16), F32, name="peer_q")
    cnt, f1, rank2, e2 = _peer_select(qp, peer_keys1.astype(BF16), peer_keys2.astype(BF16))
    peer = _peer_dense(h2, peer_u.astype(BF16), peer_v.astype(BF16), cnt, f1, rank2, e2)

    return _ple_final(x1, peer, norm_ple_g, ple_w_gate.astype(BF16), p, ple_w_proj.astype(BF16), norm_final_g)


def kernel(x, p, norm_mix_g, w_in, pool_w, pool_scale, cmp_pos_k, cmp_pos_v, cmp_w_k, cmp_w_v, w_up_pool, w_up_nsa, w_out, norm_ffn_g, peer_w_q, peer_keys1, peer_keys2, peer_u, peer_v, norm_ple_g, ple_w_gate, ple_w_proj, norm_final_g):
    B, T, D = x.shape
    depth = w_in.shape[0]
    assert depth == 1, "the final rmsnorm is fused into the layer's last kernel"
    outs = []
    for b in range(B):
        outs.append(_layer(
            x[b], p[0, b], norm_mix_g[0], w_in[0], pool_w[0], pool_scale[0], cmp_pos_k[0], cmp_pos_v[0],
            cmp_w_k[0], cmp_w_v[0], w_up_pool[0], w_up_nsa[0], w_out[0], norm_ffn_g[0], peer_w_q[0],
            peer_keys1[0], peer_keys2[0], peer_u[0], peer_v[0], norm_ple_g[0], ple_w_gate[0], ple_w_proj[0],
            norm_final_g))
    return outs[0][None] if B == 1 else jnp.stack(outs, axis=0)
```

```python
import functools

import numpy as np
import jax
import jax.numpy as jnp
from jax import lax
from jax.experimental import pallas as pl
from jax.experimental.pallas import tpu as pltpu

F32 = jnp.float32
BF16 = jnp.bfloat16

POOL_GROUPS = 4
POOL_GROUP_WIDTH = 512
POOL_WIDTH = POOL_GROUPS * POOL_GROUP_WIDTH
POOL_WINDOWS = (2, 4, 8, 16)
POOL_HALO = 16
N_HEADS = 16
N_KV_GROUPS = 4
HEADS_PER_GROUP = N_HEADS // N_KV_GROUPS
HEAD_DIM = 128
Q_WIDTH = N_HEADS * HEAD_DIM
KV_WIDTH = N_KV_GROUPS * HEAD_DIM
GROUP_Q_WIDTH = HEADS_PER_GROUP * HEAD_DIM
CMP_BLOCK = 32
CMP_STRIDE = 16
SLC_BLOCK = 64
SLC_TOPK = 16
WINDOW = 512
FORCE_SCORE = 1e4
PEER_HEADS = 8
PEER_NKEYS = 128
PEER_QDIM = 256
PEER_HALF = PEER_QDIM // 2
PEER_TOPK = 16
PEER_CHUNK_KEYS = 4
PLE_DIM = 256
RMS_EPS = 1e-6
NEG = -1e30
GATE_LANES = 128

LANES = 128
V7X_VMEM_BYTES = 64 * 1024 * 1024
VMEM_LIMIT = V7X_VMEM_BYTES * 7 // 8

_NT = (((1,), (1,)), ((), ()))
_TN = (((0,), (0,)), ((), ()))


def _params(*sem):
    return pltpu.CompilerParams(dimension_semantics=sem, vmem_limit_bytes=VMEM_LIMIT)


def _pick(n, cands):
    for c in cands:
        if n % c == 0:
            return c
    raise ValueError(f"no tile in {cands} divides {n}")


def _rmsnorm_kernel(x_ref, g_ref, o_ref):
    x = x_ref[...]
    ms = jnp.mean(x * x, axis=-1, keepdims=True)
    o_ref[...] = (x * lax.rsqrt(ms + RMS_EPS) * g_ref[...]).astype(o_ref.dtype)


def _rmsnorm(x, g, out_dtype=BF16):
    T, D = x.shape
    tm = _pick(T, (256, 128, 64, 8))
    return pl.pallas_call(
        _rmsnorm_kernel,
        grid=(T // tm,),
        in_specs=[pl.BlockSpec((tm, D), lambda i: (i, 0)), pl.BlockSpec((1, D), lambda i: (0, 0))],
        out_specs=pl.BlockSpec((tm, D), lambda i: (i, 0)),
        out_shape=jax.ShapeDtypeStruct((T, D), out_dtype),
        compiler_params=_params("parallel"),
        name="rmsnorm",
    )(x, g.reshape(1, D))


def _mm_kernel(a_ref, b_ref, *rest, has_res):
    o_ref = rest[-1]
    acc = jnp.dot(a_ref[...], b_ref[...], preferred_element_type=F32)
    if has_res:
        acc = acc + rest[0][...]
    o_ref[...] = acc.astype(o_ref.dtype)


def _mm(a, b, out_dtype, res=None, name="mm"):
    M, K = a.shape
    _, N = b.shape
    tm = _pick(M, (1024, 512, 256, 128))
    tn = _pick(N, (1024, 512, 256, 128))
    in_specs = [pl.BlockSpec((tm, K), lambda j, i: (i, 0)), pl.BlockSpec((K, tn), lambda j, i: (0, j))]
    args = [a, b]
    if res is not None:
        in_specs.append(pl.BlockSpec((tm, tn), lambda j, i: (i, j)))
        args.append(res)
    return pl.pallas_call(
        functools.partial(_mm_kernel, has_res=res is not None),
        grid=(N // tn, M // tm),
        in_specs=in_specs,
        out_specs=pl.BlockSpec((tm, tn), lambda j, i: (i, j)),
        out_shape=jax.ShapeDtypeStruct((M, N), out_dtype),
        compiler_params=_params("parallel", "parallel"),
        name=name,
    )(*args)


def _prep_w_in_kernel(w_ref, main_ref, gate_ref, *, head, n_gate, tail):
    x = w_ref[...]
    main_ref[:, :head] = x[:, :head].astype(BF16)
    main_ref[:, head:] = x[:, head + n_gate:].astype(BF16)
    per = n_gate // N_KV_GROUPS
    pad = jnp.zeros((x.shape[0], GATE_LANES - per), BF16)
    for g in range(N_KV_GROUPS):
        cols = x[:, head + g * per:head + (g + 1) * per].astype(BF16)
        gate_ref[:, g * GATE_LANES:(g + 1) * GATE_LANES] = jnp.concatenate([cols, pad], axis=1)


def _prep_w_in(w_in, head, n_gate):
    K, width = w_in.shape
    tail = width - head - n_gate
    tr = _pick(K, (128, 64, 8))
    return pl.pallas_call(
        functools.partial(_prep_w_in_kernel, head=head, n_gate=n_gate, tail=tail),
        grid=(K // tr,),
        in_specs=[pl.BlockSpec((tr, width), lambda i: (i, 0))],
        out_specs=[pl.BlockSpec((tr, head + tail), lambda i: (i, 0)),
                   pl.BlockSpec((tr, N_KV_GROUPS * GATE_LANES), lambda i: (i, 0))],
        out_shape=[jax.ShapeDtypeStruct((K, head + tail), BF16),
                   jax.ShapeDtypeStruct((K, N_KV_GROUPS * GATE_LANES), BF16)],
        compiler_params=_params("parallel"),
        name="prep_w_in",
    )(w_in)


def _pool_merge_kernel(u_ref, halo_ref, pw_ref, ps_ref, wp_ref, nsa_ref, wn_ref, gp_ref, gn_ref,
                       o_ref, pool_scr, *, tm):
    i = pl.program_id(0)
    j = pl.program_id(1)

    @pl.when(j == 0)
    def _():
        u = u_ref[...].astype(F32)
        halo = halo_ref[...].astype(F32)
        halo = jnp.where(i == 0, 0.0, halo)
        ext = jnp.concatenate([halo, u], axis=0)
        t = (i * tm + lax.broadcasted_iota(jnp.int32, (tm, 1), 0)).astype(F32)
        for g, w in enumerate(POOL_WINDOWS):
            c0, c1 = g * POOL_GROUP_WIDTH, (g + 1) * POOL_GROUP_WIDTH
            s = ext[:, c0:c1]
            k = 1
            while k < w:
                s = s[:-k] + s[k:]
                k *= 2
            wsum = s[POOL_HALO + 1 - w:]
            cnt = jnp.minimum(t + 1.0, float(w))
            pooled = wsum / cnt - u[:, c0:c1]
            mixed = jnp.dot(pooled.astype(BF16), pw_ref[g], preferred_element_type=F32)
            pool_scr[:, c0:c1] = (mixed * ps_ref[:, c0:c1]).astype(BF16)

    up_pool = jnp.dot(pool_scr[...], wp_ref[...], preferred_element_type=F32)
    up_nsa = jnp.dot(nsa_ref[...], wn_ref[...], preferred_element_type=F32)
    merged = jax.nn.sigmoid(gp_ref[...].astype(F32)) * up_pool + jax.nn.sigmoid(gn_ref[...].astype(F32)) * up_nsa
    o_ref[...] = merged.astype(o_ref.dtype)


def _pool_merge(z, gp_off, gn_off, pool_w, pool_scale, w_up_pool, nsa_out, w_up_nsa, D):
    T = z.shape[0]
    tm = _pick(T, (512, 256, 128))
    tn = _pick(D, (1024, 512, 256, 128))
    assert gp_off % tn == 0 and gn_off % tn == 0 and tm % POOL_HALO == 0
    hb = tm // POOL_HALO
    return pl.pallas_call(
        functools.partial(_pool_merge_kernel, tm=tm),
        grid=(T // tm, D // tn),
        in_specs=[
            pl.BlockSpec((tm, POOL_WIDTH), lambda i, j: (i, 0)),
            pl.BlockSpec((POOL_HALO, POOL_WIDTH), lambda i, j: (jnp.maximum(i * hb - 1, 0), 0)),
            pl.BlockSpec((POOL_GROUPS, POOL_GROUP_WIDTH, POOL_GROUP_WIDTH), lambda i, j: (0, 0, 0)),
            pl.BlockSpec((1, POOL_WIDTH), lambda i, j: (0, 0)),
            pl.BlockSpec((POOL_WIDTH, tn), lambda i, j: (0, j)),
            pl.BlockSpec((tm, Q_WIDTH), lambda i, j: (i, 0)),
            pl.BlockSpec((Q_WIDTH, tn), lambda i, j: (0, j)),
            pl.BlockSpec((tm, tn), lambda i, j: (i, gp_off // tn + j)),
            pl.BlockSpec((tm, tn), lambda i, j: (i, gn_off // tn + j)),
        ],
        out_specs=pl.BlockSpec((tm, tn), lambda i, j: (i, j)),
        out_shape=jax.ShapeDtypeStruct((T, D), BF16),
        scratch_shapes=[pltpu.VMEM((tm, POOL_WIDTH), BF16)],
        compiler_params=_params("parallel", "arbitrary"),
        name="pool_merge",
    )(z, z, pool_w, pool_scale.reshape(1, POOL_WIDTH), w_up_pool, nsa_out, w_up_nsa, z, z)


def _compress_kernel(r_ref, w_ref, pos_ref, o_ref, *, n_cmp, transpose_out):
    half = CMP_STRIDE * HEAD_DIM
    r = r_ref[0]
    w = w_ref[...]
    top = jnp.dot(r, w[:half], preferred_element_type=F32)
    bot = jnp.dot(r, w[half:], preferred_element_type=F32)
    bias = jnp.dot(pos_ref[...], w, preferred_element_type=F32)
    ncp = top.shape[0]
    bot_next = jnp.concatenate([bot[1:], jnp.zeros((1, HEAD_DIM), F32)], axis=0)
    out = top + bot_next + bias
    row = lax.broadcasted_iota(jnp.int32, (ncp, 1), 0)
    out = jnp.where(row < n_cmp, out, 0.0)
    if transpose_out:
        o_ref[0] = out.T.astype(o_ref.dtype)
    else:
        o_ref[0] = out.astype(o_ref.dtype)


def _compress(r, w, pos, n_cmp, transpose_out):
    G, ncp, width = r.shape
    oshape = (G, HEAD_DIM, ncp) if transpose_out else (G, ncp, HEAD_DIM)
    return pl.pallas_call(
        functools.partial(_compress_kernel, n_cmp=n_cmp, transpose_out=transpose_out),
        grid=(G,),
        in_specs=[
            pl.BlockSpec((1, ncp, width), lambda g: (g, 0, 0)),
            pl.BlockSpec((CMP_BLOCK * HEAD_DIM, HEAD_DIM), lambda g: (0, 0)),
            pl.BlockSpec((1, CMP_BLOCK * HEAD_DIM), lambda g: (0, 0)),
        ],
        out_specs=pl.BlockSpec((1,) + oshape[1:], lambda g: (g, 0, 0)),
        out_shape=jax.ShapeDtypeStruct(oshape, BF16),
        compiler_params=_params("parallel"),
        name="nsa_compress",
    )(r, w, pos)


def _extract_topk(a, k, want_vals=False, exact=True):
    n = a.shape[0]
    idx = lax.broadcasted_iota(jnp.int32, a.shape, 0).astype(F32)
    rank = jnp.full(a.shape, float(k), F32)
    vals = []
    for r in range(k):
        m = jnp.max(a, axis=0, keepdims=True)
        hit = a == m
        if exact:
            first = jnp.min(jnp.where(hit, idx, float(n)), axis=0, keepdims=True)
            hit = idx == first
        rank = jnp.where(hit, float(r), rank)
        a = jnp.where(hit, -jnp.inf, a)
        vals.append(m)
    return (rank, vals) if want_vals else rank


def _cmp_select_kernel(slope_ref, q_ref, kc_ref, vct_ref, ovt_ref, o_ref, sel_ref, used_ref, *, tq):
    g = pl.program_id(0)
    i = pl.program_id(1)
    ncp = kc_ref.shape[1]
    ns = ovt_ref.shape[0]
    t = i * tq + lax.broadcasted_iota(jnp.int32, (1, tq), 1)
    n = lax.broadcasted_iota(jnp.int32, (ncp, 1), 0)
    dist = (t - (n * CMP_STRIDE + CMP_BLOCK - 1)).astype(F32)
    valid = dist >= 0.0
    kc = kc_ref[0]
    vct = vct_ref[0]
    q = q_ref[...]
    psum = jnp.zeros((ncp, tq), F32)
    for h in range(HEADS_PER_GROUP):
        qh = (q[:, h * HEAD_DIM:(h + 1) * HEAD_DIM].astype(F32) * (HEAD_DIM ** -0.5)).astype(BF16)
        s = lax.dot_general(kc, qh, _NT, preferred_element_type=F32)
        s = s - slope_ref[g * HEADS_PER_GROUP + h] * dist
        s = jnp.where(valid, s, NEG)
        m = jnp.max(s, axis=0, keepdims=True)
        e = jnp.exp(s - m)
        l = jnp.sum(e, axis=0, keepdims=True)
        p = jnp.where(valid, e / l, 0.0)
        ot = jnp.dot(vct, p.astype(BF16), preferred_element_type=F32)
        o_ref[:, h * HEAD_DIM:(h + 1) * HEAD_DIM] = ot.T.astype(o_ref.dtype)
        psum = psum + p
    p_hi = psum.astype(BF16)
    p_lo = (psum - p_hi.astype(F32)).astype(BF16)
    ovt = ovt_ref[...]
    imp = jnp.dot(ovt, p_hi, preferred_element_type=F32) + jnp.dot(ovt, p_lo, preferred_element_type=F32)
    kk = lax.broadcasted_iota(jnp.int32, (ns, 1), 0)
    cur = jnp.right_shift(t, SLC_BLOCK.bit_length() - 1)
    forced = (kk == 0) | (kk == cur) | (kk == cur - 1)
    score = jnp.where(forced, FORCE_SCORE, jnp.where(kk <= cur, imp, -1.0))
    k_sel = min(SLC_TOPK, ns)
    rank = _extract_topk(score, k_sel)
    sel = jnp.where((rank < float(k_sel)) & (kk <= cur), 1.0, 0.0)
    sel_t = sel.T
    sel_ref[0] = sel_t.astype(sel_ref.dtype)
    used_ref[0, 0] = jnp.max(sel_t, axis=0, keepdims=True)


def _cmp_select(slopes, z, q_off, kc, vct, ovt, T):
    tq = _pick(T, (256, 128))
    ncp = kc.shape[1]
    ns = ovt.shape[0]
    assert q_off % GROUP_Q_WIDTH == 0
    qb = q_off // GROUP_Q_WIDTH
    return pl.pallas_call(
        functools.partial(_cmp_select_kernel, tq=tq),
        grid=(N_KV_GROUPS, T // tq),
        in_specs=[
            pl.BlockSpec(memory_space=pltpu.SMEM),
            pl.BlockSpec((tq, GROUP_Q_WIDTH), lambda g, i: (i, qb + g)),
            pl.BlockSpec((1, ncp, HEAD_DIM), lambda g, i: (g, 0, 0)),
            pl.BlockSpec((1, HEAD_DIM, ncp), lambda g, i: (g, 0, 0)),
            pl.BlockSpec((ns, ncp), lambda g, i: (0, 0)),
        ],
        out_specs=[
            pl.BlockSpec((tq, GROUP_Q_WIDTH), lambda g, i: (i, g)),
            pl.BlockSpec((1, tq, ns), lambda g, i: (g, i, 0)),
            pl.BlockSpec((1, 1, 1, ns), lambda g, i: (g, i, 0, 0)),
        ],
        out_shape=[
            jax.ShapeDtypeStruct((T, Q_WIDTH), BF16),
            jax.ShapeDtypeStruct((N_KV_GROUPS, T, ns), BF16),
            jax.ShapeDtypeStruct((N_KV_GROUPS, T // tq, 1, ns), F32),
        ],
        compiler_params=_params("parallel", "parallel"),
        name="nsa_cmp_select",
    )(slopes, z, kc, vct, ovt)


def _nsa_attend_kernel(slope_ref, tile_ref, count_ref, q_ref, ks_ref, vs_ref, kw_ref, vw_ref, ocmp_ref, sel_ref,
                       exp_ref, gate_ref, o_ref, q_scr, acc_scr, *, tq, tk, span):
    g = pl.program_id(0)
    i = pl.program_id(1)
    t0 = i * tq
    hg = HEADS_PER_GROUP
    q = q_ref[...]
    for h in range(hg):
        qh = q[:, h * HEAD_DIM:(h + 1) * HEAD_DIM].astype(F32) * (HEAD_DIM ** -0.5)
        q_scr[h * tq:(h + 1) * tq, :] = qh.astype(BF16)
    q_all = q_scr[...]
    slopes = [slope_ref[g * hg + h] for h in range(hg)]

    sel = sel_ref[0]
    rel = (lax.broadcasted_iota(jnp.int32, (tq, tk), 0) - lax.broadcasted_iota(jnp.int32, (tq, tk), 1)).astype(F32)
    acc_scr[...] = jnp.zeros_like(acc_scr)
    step = g * pl.num_programs(1) + i
    list_base = step * exp_ref.shape[0]

    def kv_step(n, carry):
        ms, ls = carry
        j = tile_ref[list_base + n]
        s0 = pl.multiple_of(j * tk, tk)
        kt = ks_ref[pl.ds(s0, tk), :]
        vt = vs_ref[pl.ds(s0, tk), :]
        d = rel + (t0 - s0).astype(F32)
        chosen = jnp.dot(sel, exp_ref[j], preferred_element_type=F32)
        allow = (chosen > 0.5) & (d >= 0.0)
        s_all = lax.dot_general(q_all, kt, _NT, preferred_element_type=F32)
        kcol = (lax.broadcasted_iota(jnp.int32, (1, tk), 1) + (s0 - t0)).astype(F32)
        new_ms, new_ls, ps = [], [], []
        for h in range(hg):
            rows = slice(h * tq, (h + 1) * tq)
            s = s_all[rows] + slopes[h] * kcol
            s = jnp.where(allow, s, NEG)
            m_new = jnp.maximum(ms[h], jnp.max(s, axis=1, keepdims=True))
            alpha = jnp.exp(ms[h] - m_new)
            p = jnp.exp(s - m_new)
            new_ls.append(alpha * ls[h] + jnp.sum(p, axis=1, keepdims=True))
            new_ms.append(m_new)
            acc_scr[rows, :] = alpha * acc_scr[rows, :]
            ps.append(p.astype(BF16))
        acc_scr[...] += jnp.dot(jnp.concatenate(ps, axis=0), vt, preferred_element_type=F32)
        return tuple(new_ms), tuple(new_ls)

    init = (tuple(jnp.full((tq, 1), NEG, F32) for _ in range(hg)),
            tuple(jnp.zeros((tq, 1), F32) for _ in range(hg)))
    ms, ls = lax.fori_loop(0, count_ref[step], kv_step, init)

    w0 = pl.multiple_of(jnp.maximum(t0 - WINDOW, 0), 8)
    kw = kw_ref[pl.ds(w0, span), :]
    vw = vw_ref[pl.ds(w0, span), :]
    dw = ((t0 - w0) + lax.broadcasted_iota(jnp.int32, (tq, span), 0)
          - lax.broadcasted_iota(jnp.int32, (tq, span), 1)).astype(F32)
    allow_w = (dw >= 0.0) & (dw < float(WINDOW))
    sw_all = lax.dot_general(q_all, kw, _NT, preferred_element_type=F32)
    kcol_w = (lax.broadcasted_iota(jnp.int32, (1, span), 1) + (w0 - t0)).astype(F32)
    pws, lws = [], []
    for h in range(hg):
        s = sw_all[h * tq:(h + 1) * tq] + slopes[h] * kcol_w
        s = jnp.where(allow_w, s, NEG)
        m = jnp.max(s, axis=1, keepdims=True)
        p = jnp.exp(s - m)
        lws.append(jnp.sum(p, axis=1, keepdims=True))
        pws.append(p.astype(BF16))
    ow_all = jnp.dot(jnp.concatenate(pws, axis=0), vw, preferred_element_type=F32)

    gt = jax.nn.sigmoid(gate_ref[...])
    ocmp = ocmp_ref[...].astype(F32)
    for h in range(hg):
        o_c = ocmp[:, h * HEAD_DIM:(h + 1) * HEAD_DIM]
        o_s = acc_scr[h * tq:(h + 1) * tq, :] / ls[h]
        o_w = ow_all[h * tq:(h + 1) * tq] / lws[h]
        o = gt[:, 3 * h:3 * h + 1] * o_c + gt[:, 3 * h + 1:3 * h + 2] * o_s + gt[:, 3 * h + 2:3 * h + 3] * o_w
        o_ref[:, h * HEAD_DIM:(h + 1) * HEAD_DIM] = o.astype(o_ref.dtype)


def _nsa_attend(slopes, tile_list, tile_count, z, q_off, kv_off, o_cmp, sel, expand, gates, T):
    tq = _pick(T, (256, 128))
    n_exp, ns, tk = expand.shape
    span = WINDOW + tq
    assert T >= span and q_off % GROUP_Q_WIDTH == 0 and kv_off % HEAD_DIM == 0
    qb = q_off // GROUP_Q_WIDTH
    kvb = kv_off // HEAD_DIM
    G = N_KV_GROUPS

    def kv_spec(branch):
        return pl.BlockSpec((T, HEAD_DIM), lambda g, i: (0, kvb + branch * G + g))

    return pl.pallas_call(
        functools.partial(_nsa_attend_kernel, tq=tq, tk=tk, span=span),
        grid=(G, T // tq),
        in_specs=[
            pl.BlockSpec(memory_space=pltpu.SMEM),
            pl.BlockSpec(memory_space=pltpu.SMEM),
            pl.BlockSpec(memory_space=pltpu.SMEM),
            pl.BlockSpec((tq, GROUP_Q_WIDTH), lambda g, i: (i, qb + g)),
            kv_spec(2), kv_spec(3), kv_spec(4), kv_spec(5),
            pl.BlockSpec((tq, GROUP_Q_WIDTH), lambda g, i: (i, g)),
            pl.BlockSpec((1, tq, ns), lambda g, i: (g, i, 0)),
            pl.BlockSpec((n_exp, ns, tk), lambda g, i: (0, 0, 0)),
            pl.BlockSpec((tq, GATE_LANES), lambda g, i: (i, g)),
        ],
        out_specs=pl.BlockSpec((tq, GROUP_Q_WIDTH), lambda g, i: (i, g)),
        out_shape=jax.ShapeDtypeStruct((T, Q_WIDTH), BF16),
        scratch_shapes=[
            pltpu.VMEM((HEADS_PER_GROUP * tq, HEAD_DIM), BF16),
            pltpu.VMEM((HEADS_PER_GROUP * tq, HEAD_DIM), F32),
        ],
        compiler_params=_params("parallel", "arbitrary"),
        name="nsa_attend",
    )(slopes, tile_list, tile_count, z, z, z, z, z, o_cmp, sel, expand, gates)


_CAND_PIECES = [(0, 16), (1, 8), (2, 8), (3, 8), (4, 8), (5, 8), (6, 8), (7, 8)]


def _peer_select_head(q, k1, k2, exact):
    K = PEER_TOPK
    s1 = lax.dot_general(k1, q[:, :PEER_HALF], _NT, preferred_element_type=F32)
    s2 = lax.dot_general(k2, q[:, PEER_HALF:], _NT, preferred_element_type=F32)
    rank1, v1 = _extract_topk(s1, K, want_vals=True, exact=exact)
    rank2, v2 = _extract_topk(s2, K, want_vals=True, exact=exact)
    v1s = jnp.concatenate(v1, axis=0)
    v2s = jnp.concatenate(v2, axis=0)
    pieces = [v1[r1] + v2s[:L] for r1, L in _CAND_PIECES]
    pieces.append(v1s[8:] + v2[0])
    cand = jnp.concatenate(pieces, axis=0)
    crank = _extract_topk(cand, K, exact=exact)
    chosen = crank < float(K)
    top = v1[0] + v2[0]
    z = jnp.sum(jnp.where(chosen, jnp.exp(cand - top), 0.0), axis=0, keepdims=True)
    chosen_f = jnp.where(chosen, 1.0, 0.0)
    counts = []
    off = 0
    for _, L in _CAND_PIECES:
        counts.append(jnp.sum(chosen_f[off:off + L], axis=0, keepdims=True))
        off += L
    tail = chosen_f[off:off + 8]
    for r in range(8):
        counts.append(tail[r:r + 1])
    cnt = jnp.zeros_like(s1)
    for r1 in range(K):
        cnt = jnp.where(rank1 == float(r1), counts[r1], cnt)

    def n_ranked(rank):
        return jnp.sum(jnp.where(rank < float(K), 1.0, 0.0), axis=0, keepdims=True)

    tied = (jnp.abs(n_ranked(rank1) - float(K)) + jnp.abs(n_ranked(rank2) - float(K))
            + jnp.abs(n_ranked(crank) - float(K)))
    return (cnt, jnp.exp(s1 - v1[0]) / z, rank2, jnp.exp(s2 - v2[0])), tied


def _peer_select_kernel(q_ref, k1_ref, k2_ref, cnt_ref, f1_ref, rank2_ref, e2_ref, *, heads):
    k1 = k1_ref[...]
    k2 = k2_ref[...]
    out_refs = (cnt_ref, f1_ref, rank2_ref, e2_ref)

    def run(exact):
        tied = None
        for hh in range(heads):
            q = q_ref[:, hh * PEER_QDIM:(hh + 1) * PEER_QDIM].astype(BF16)
            outs, t = _peer_select_head(q, k1, k2, exact)
            for ref, val in zip(out_refs, outs):
                ref[hh, 0] = val
            tied = t if tied is None else tied + t
        return jnp.max(tied)

    any_tie = run(False)

    @pl.when(any_tie > 0.0)
    def _():
        run(True)


def _peer_select(qp, keys1, keys2):
    T = qp.shape[0]
    tp = _pick(T, (128,))
    heads = 2
    shp = jax.ShapeDtypeStruct((PEER_HEADS, T // tp, PEER_NKEYS, tp), F32)
    ospec = pl.BlockSpec((heads, 1, PEER_NKEYS, tp), lambda i, h: (h, i, 0, 0))
    return pl.pallas_call(
        functools.partial(_peer_select_kernel, heads=heads),
        grid=(T // tp, PEER_HEADS // heads),
        in_specs=[
            pl.BlockSpec((tp, heads * PEER_QDIM), lambda i, h: (i, h)),
            pl.BlockSpec((PEER_NKEYS, PEER_HALF), lambda i, h: (0, 0)),
            pl.BlockSpec((PEER_NKEYS, PEER_HALF), lambda i, h: (0, 0)),
        ],
        out_specs=[ospec, ospec, ospec, ospec],
        out_shape=[shp, shp, shp, shp],
        compiler_params=_params("parallel", "parallel"),
        name="peer_select",
    )(qp, keys1, keys2)


def _peer_dense_kernel(h_ref, u_ref, v_ref, cnt_ref, f1_ref, rank2_ref, e2_ref, o_ref,
                       w_scr, acc_scr, at_scr, *, ipc, tm, out_chunk):
    c = pl.program_id(1)
    nc = pl.num_programs(1)
    nk = PEER_NKEYS
    lt = LANES
    D = acc_scr.shape[0]

    def build_gates(chunk, k, tb, tie):
        i1 = chunk * ipc + k
        lanes = slice(tb * lt, (tb + 1) * lt)
        w = jnp.zeros((nk, lt), F32)
        for h in range(PEER_HEADS):
            crow = cnt_ref[h, tb, pl.ds(i1, 1), :]
            if h == 0:
                crow = crow + tie[:, lanes]
            frow = f1_ref[h, tb, pl.ds(i1, 1), :]
            w = w + jnp.where(rank2_ref[h, tb] < crow, e2_ref[h, tb] * frow, 0.0)
        w_scr[k * nk:(k + 1) * nk, lanes] = w

    tiles = [(k, tb) for k in range(ipc) for tb in range(tm // lt)]
    early, late = tiles[:len(tiles) // 2], tiles[len(tiles) // 2:]
    zero_row = jnp.zeros((1, tm), F32)

    @pl.when(c == 0)
    def _():
        acc_scr[...] = jnp.zeros_like(acc_scr)
        for k, tb in early:
            build_gates(c, k, tb, zero_row)

    half = (ipc * nk) // 2
    h = h_ref[...]
    at_scr[:half, :] = lax.dot_general(u_ref[:half, :], h, _NT, preferred_element_type=F32)
    tie = at_scr[0:1, :] * 0.0
    at_scr[half:, :] = lax.dot_general(u_ref[half:, :], h, _NT, preferred_element_type=F32)
    for k, tb in late:
        build_gates(c, k, tb, tie)
    coef = (w_scr[...] * jax.nn.gelu(at_scr[...])).astype(BF16)
    nxt = jnp.minimum(c + 1, nc - 1)
    n_out = D // out_chunk
    for j in range(n_out):
        d0 = j * out_chunk
        acc_scr[d0:d0 + out_chunk, :] += lax.dot_general(v_ref[:, d0:d0 + out_chunk], coef, _TN,
                                                         preferred_element_type=F32)
        tie2 = acc_scr[d0:d0 + 1, :] * 0.0
        for k, tb in early[j::n_out]:
            build_gates(nxt, k, tb, tie2)

    @pl.when(c == nc - 1)
    def _():
        for d0 in range(0, D, out_chunk):
            o_ref[:, d0:d0 + out_chunk] = acc_scr[d0:d0 + out_chunk, :].T


def _peer_dense(h2, u, v, cnt, f1, rank2, e2):
    T, D = h2.shape
    NE = u.shape[0]
    tm = _pick(T, (512, 256, 128))
    ipc = PEER_CHUNK_KEYS
    ec = ipc * PEER_NKEYS
    once = pl.Buffered(1)
    sel_spec = pl.BlockSpec((PEER_HEADS, tm // LANES, PEER_NKEYS, LANES), lambda i, c: (0, i, 0, 0),
                            pipeline_mode=once)
    return pl.pallas_call(
        functools.partial(_peer_dense_kernel, ipc=ipc, tm=tm, out_chunk=_pick(D, (512, 256, 128))),
        grid=(T // tm, NE // ec),
        in_specs=[
            pl.BlockSpec((tm, D), lambda i, c: (i, 0), pipeline_mode=once),
            pl.BlockSpec((ec, D), lambda i, c: (c, 0)),
            pl.BlockSpec((ec, D), lambda i, c: (c, 0)),
            sel_spec, sel_spec, sel_spec, sel_spec,
        ],
        out_specs=pl.BlockSpec((tm, D), lambda i, c: (i, 0), pipeline_mode=once),
        out_shape=jax.ShapeDtypeStruct((T, D), F32),
        scratch_shapes=[pltpu.VMEM((ec, tm), F32), pltpu.VMEM((D, tm), F32), pltpu.VMEM((ec, tm), F32)],
        compiler_params=_params("parallel", "arbitrary"),
        name="peer_dense",
    )(h2, u, v, cnt, f1, rank2, e2)


def _ple_final_kernel(x1_ref, peer_ref, gple_ref, wg_ref, p_ref, wp_ref, gfin_ref, o_ref, acc_scr, ss_scr,
                      *, tk, nk, tn):
    j = pl.program_id(1)
    D = acc_scr.shape[1]
    x2 = x1_ref[...] + peer_ref[...]
    lhs = (x2 * gple_ref[...]).astype(BF16)
    ssq = jnp.sum(x2 * x2, axis=-1, keepdims=True)

    @pl.when(j == 0)
    def _():
        acc_scr[...] = jnp.zeros_like(acc_scr)
        ss_scr[...] = jnp.zeros_like(ss_scr)

    ss_scr[...] += ssq
    for n0 in range(0, D, tn):
        acc_scr[:, n0:n0 + tn] += jnp.dot(lhs, wg_ref[:, n0:n0 + tn], preferred_element_type=F32)
    for jj in range(nk):
        @pl.when(j == jj)
        def _():
            o_ref[:, jj * tk:(jj + 1) * tk] = x2

    @pl.when(j == nk - 1)
    def _():
        inv = lax.rsqrt(ss_scr[...] / float(D) + RMS_EPS)
        pb = p_ref[...].astype(BF16)
        ss3 = jnp.zeros_like(inv)
        for n0 in range(0, D, tn):
            cols = slice(n0, n0 + tn)
            gate = jax.nn.sigmoid(acc_scr[:, cols] * inv)
            proj = jnp.dot(pb, wp_ref[:, cols], preferred_element_type=F32)
            x3 = o_ref[:, cols] + gate * proj
            o_ref[:, cols] = x3
            ss3 = ss3 + jnp.sum(x3 * x3, axis=-1, keepdims=True)
        inv3 = lax.rsqrt(ss3 / float(D) + RMS_EPS)
        for n0 in range(0, D, tn):
            cols = slice(n0, n0 + tn)
            o_ref[:, cols] = o_ref[:, cols] * inv3 * gfin_ref[:, cols]


def _ple_final(x1, peer, g_ple, w_gate, p, w_proj, g_final):
    T, D = x1.shape
    tm = _pick(T, (512, 256, 128))
    tk = _pick(D, (1024, 512, 256, 128))
    nk = D // tk
    once = pl.Buffered(1)
    return pl.pallas_call(
        functools.partial(_ple_final_kernel, tk=tk, nk=nk, tn=tk),
        grid=(T // tm, nk),
        in_specs=[
            pl.BlockSpec((tm, tk), lambda i, j: (i, j)),
            pl.BlockSpec((tm, tk), lambda i, j: (i, j)),
            pl.BlockSpec((1, tk), lambda i, j: (0, j)),
            pl.BlockSpec((tk, D), lambda i, j: (j, 0)),
            pl.BlockSpec((tm, PLE_DIM), lambda i, j: (i, 0)),
            pl.BlockSpec((PLE_DIM, D), lambda i, j: (0, 0), pipeline_mode=once),
            pl.BlockSpec((1, D), lambda i, j: (0, 0)),
        ],
        out_specs=pl.BlockSpec((tm, D), lambda i, j: (i, 0), pipeline_mode=once),
        out_shape=jax.ShapeDtypeStruct((T, D), F32),
        scratch_shapes=[pltpu.VMEM((tm, D), F32), pltpu.VMEM((tm, 1), F32)],
        compiler_params=_params("parallel", "arbitrary"),
        name="ple_final",
    )(x1, peer, g_ple.reshape(1, D), w_gate, p, w_proj, g_final.reshape(1, D))


def _alibi_slopes():
    return jnp.asarray(2.0 ** (-8.0 * np.arange(1, N_HEADS + 1) / N_HEADS), dtype=F32)


def _overlap_t(T):
    ncp = T // CMP_STRIDE
    n_cmp = (T - CMP_BLOCK) // CMP_STRIDE + 1
    n_sel = T // SLC_BLOCK
    c_start = np.arange(ncp) * CMP_STRIDE
    s_start = np.arange(n_sel) * SLC_BLOCK
    ov = (c_start[None, :] < s_start[:, None] + SLC_BLOCK) & (c_start[None, :] + CMP_BLOCK > s_start[:, None])
    ov = ov & (np.arange(ncp)[None, :] < n_cmp)
    return jnp.asarray(ov.astype(np.float32), dtype=BF16)


def _expand_blocks(T, tk):
    n_sel = T // SLC_BLOCK
    key_blk = (np.arange(T) // SLC_BLOCK).reshape(T // tk, 1, tk)
    e = key_blk == np.arange(n_sel).reshape(1, n_sel, 1)
    return jnp.asarray(e.astype(np.float32), dtype=BF16)


def _tile_lists(used, n_tiles):
    G, nq, _, ns = used.shape
    flags = used.reshape(G, nq, n_tiles, ns // n_tiles).max(axis=-1) > 0.0
    order = jnp.argsort(jnp.where(flags, 0, 1), axis=-1, stable=True)
    return order.astype(jnp.int32).reshape(-1), flags.sum(axis=-1).astype(jnp.int32).reshape(-1)


def _layer(x, p, norm_mix_g, w_in, pool_w, pool_scale, cmp_pos_k, cmp_pos_v, cmp_w_k, cmp_w_v,
           w_up_pool, w_up_nsa, w_out, norm_ffn_g, peer_w_q, peer_keys1, peer_keys2, peer_u, peer_v,
           norm_ple_g, ple_w_gate, ple_w_proj, norm_final_g):
    T, D = x.shape
    G = N_KV_GROUPS
    q_off = POOL_WIDTH
    kv_off = q_off + Q_WIDTH
    gates_off = kv_off + 6 * KV_WIDTH
    gp_off = gates_off
    gn_off = gp_off + D

    w_main, w_g = _prep_w_in(w_in, gates_off, 3 * N_HEADS)

    h = _rmsnorm(x, norm_mix_g)
    z = _mm(h, w_main, BF16, name="in_proj")
    gates = _mm(h, w_g, F32, name="gate_proj")

    ncp = T // CMP_STRIDE
    n_cmp = (T - CMP_BLOCK) // CMP_STRIDE + 1

    def blocks16(col):
        a = z[:, col:col + KV_WIDTH].reshape(ncp, CMP_STRIDE, G, HEAD_DIM)
        return a.transpose(2, 0, 1, 3).reshape(G, ncp, CMP_STRIDE * HEAD_DIM)

    kc = _compress(blocks16(kv_off), cmp_w_k.astype(BF16), cmp_pos_k.reshape(1, -1).astype(BF16), n_cmp, False)
    vct = _compress(blocks16(kv_off + KV_WIDTH), cmp_w_v.astype(BF16), cmp_pos_v.reshape(1, -1).astype(BF16),
                    n_cmp, True)
    slopes = _alibi_slopes()
    o_cmp, sel, used = _cmp_select(slopes, z, q_off, kc, vct, _overlap_t(T), T)
    tk = _pick(T, (512, 256))
    tile_list, tile_count = _tile_lists(used, T // tk)
    nsa_out = _nsa_attend(slopes, tile_list, tile_count, z, q_off, kv_off, o_cmp, sel, _expand_blocks(T, tk), gates, T)

    merged = _pool_merge(z, gp_off, gn_off, pool_w.astype(BF16), pool_scale, w_up_pool.astype(BF16),
                         nsa_out, w_up_nsa.astype(BF16), D)
    x1 = _mm(merged, w_out.astype(BF16), F32, res=x, name="out_proj")

    h2 = _rmsnorm(x1, norm_ffn_g)
    qp = _mm(h2, peer_w_q.astype(BF16), F32, name="peer_q")
    cnt, f1, rank2, e2 = _peer_select(qp, peer_keys1.astype(BF16), peer_keys2.astype(BF16))
    peer = _peer_dense(h2, peer_u.astype(BF16), peer_v.astype(BF16), cnt, f1, rank2, e2)

    return _ple_final(x1, peer, norm_ple_g, ple_w_gate.astype(BF16), p, ple_w_proj.astype(BF16), norm_final_g)


def kernel(x, p, norm_mix_g, w_in, pool_w, pool_scale, cmp_pos_k, cmp_pos_v, cmp_w_k, cmp_w_v, w_up_pool, w_up_nsa, w_out, norm_ffn_g, peer_w_q, peer_keys1, peer_keys2, peer_u, peer_v, norm_ple_g, ple_w_gate, ple_w_proj, norm_final_g):
    B, T, D = x.shape
    depth = w_in.shape[0]
    assert depth == 1, "the final rmsnorm is fused into the layer's last kernel"
    outs = []
    for b in range(B):
        outs.append(_layer(
            x[b], p[0, b], norm_mix_g[0], w_in[0], pool_w[0], pool_scale[0], cmp_pos_k[0], cmp_pos_v[0],
            cmp_w_k[0], cmp_w_v[0], w_up_pool[0], w_up_nsa[0], w_out[0], norm_ffn_g[0], peer_w_q[0],
            peer_keys1[0], peer_keys2[0], peer_u[0], peer_v[0], norm_ple_g[0], ple_w_gate[0], ple_w_proj[0],
            norm_final_g))
    return outs[0][None] if B == 1 else jnp.stack(outs, axis=0)
```

```python
import functools

import numpy as np
import jax
import jax.numpy as jnp
from jax import lax
from jax.experimental import pallas as pl
from jax.experimental.pallas import tpu as pltpu

F32 = jnp.float32
BF16 = jnp.bfloat16

POOL_GROUPS = 4
POOL_GROUP_WIDTH = 512
POOL_WIDTH = POOL_GROUPS * POOL_GROUP_WIDTH
POOL_WINDOWS = (2, 4, 8, 16)
POOL_HALO = 16
N_HEADS = 16
N_KV_GROUPS = 4
HEADS_PER_GROUP = N_HEADS // N_KV_GROUPS
HEAD_DIM = 128
Q_WIDTH = N_HEADS * HEAD_DIM
KV_WIDTH = N_KV_GROUPS * HEAD_DIM
GROUP_Q_WIDTH = HEADS_PER_GROUP * HEAD_DIM
CMP_BLOCK = 32
CMP_STRIDE = 16
SLC_BLOCK = 64
SLC_TOPK = 16
WINDOW = 512
FORCE_SCORE = 1e4
PEER_HEADS = 8
PEER_NKEYS = 128
PEER_QDIM = 256
PEER_HALF = PEER_QDIM // 2
PEER_TOPK = 16
PEER_CHUNK_KEYS = 4
PLE_DIM = 256
RMS_EPS = 1e-6
NEG = -1e30
GATE_LANES = 128

LANES = 128
V7X_VMEM_BYTES = 64 * 1024 * 1024
VMEM_LIMIT = V7X_VMEM_BYTES * 7 // 8

_NT = (((1,), (1,)), ((), ()))


def _params(*sem):
    return pltpu.CompilerParams(dimension_semantics=sem, vmem_limit_bytes=VMEM_LIMIT)


def _pick(n, cands):
    for c in cands:
        if n % c == 0:
            return c
    raise ValueError(f"no tile in {cands} divides {n}")


def _rmsnorm_kernel(x_ref, g_ref, o_ref):
    x = x_ref[...]
    ms = jnp.mean(x * x, axis=-1, keepdims=True)
    o_ref[...] = (x * lax.rsqrt(ms + RMS_EPS) * g_ref[...]).astype(o_ref.dtype)


def _rmsnorm(x, g, out_dtype=BF16):
    T, D = x.shape
    tm = _pick(T, (256, 128, 64, 8))
    return pl.pallas_call(
        _rmsnorm_kernel,
        grid=(T // tm,),
        in_specs=[pl.BlockSpec((tm, D), lambda i: (i, 0)), pl.BlockSpec((1, D), lambda i: (0, 0))],
        out_specs=pl.BlockSpec((tm, D), lambda i: (i, 0)),
        out_shape=jax.ShapeDtypeStruct((T, D), out_dtype),
        compiler_params=_params("parallel"),
        name="rmsnorm",
    )(x, g.reshape(1, D))


def _mm_kernel(a_ref, b_ref, *rest, has_res, b_is_nk):
    o_ref = rest[-1]
    if b_is_nk:
        acc = lax.dot_general(a_ref[...], b_ref[...], _NT, preferred_element_type=F32)
    else:
        acc = jnp.dot(a_ref[...], b_ref[...], preferred_element_type=F32)
    if has_res:
        acc = acc + rest[0][...]
    o_ref[...] = acc.astype(o_ref.dtype)


def _mm(a, b, out_dtype, res=None, b_is_nk=False, name="mm"):
    M, K = a.shape
    N = b.shape[0] if b_is_nk else b.shape[1]
    tm = _pick(M, (1024, 512, 256, 128))
    tn = _pick(N, (1024, 512, 256, 128))
    b_spec = pl.BlockSpec((tn, K), lambda j, i: (j, 0)) if b_is_nk else pl.BlockSpec((K, tn), lambda j, i: (0, j))
    in_specs = [pl.BlockSpec((tm, K), lambda j, i: (i, 0)), b_spec]
    args = [a, b]
    if res is not None:
        in_specs.append(pl.BlockSpec((tm, tn), lambda j, i: (i, j)))
        args.append(res)
    return pl.pallas_call(
        functools.partial(_mm_kernel, has_res=res is not None, b_is_nk=b_is_nk),
        grid=(N // tn, M // tm),
        in_specs=in_specs,
        out_specs=pl.BlockSpec((tm, tn), lambda j, i: (i, j)),
        out_shape=jax.ShapeDtypeStruct((M, N), out_dtype),
        compiler_params=_params("parallel", "parallel"),
        name=name,
    )(*args)


def _pool_merge_kernel(u_ref, halo_ref, pw_ref, ps_ref, wp_ref, nsa_ref, wn_ref, gp_ref, gn_ref,
                       o_ref, pool_scr, *, tm):
    i = pl.program_id(0)
    j = pl.program_id(1)

    @pl.when(j == 0)
    def _():
        u = u_ref[...].astype(F32)
        halo = halo_ref[...].astype(F32)
        halo = jnp.where(i == 0, 0.0, halo)
        ext = jnp.concatenate([halo, u], axis=0)
        t = (i * tm + lax.broadcasted_iota(jnp.int32, (tm, 1), 0)).astype(F32)
        for g, w in enumerate(POOL_WINDOWS):
            c0, c1 = g * POOL_GROUP_WIDTH, (g + 1) * POOL_GROUP_WIDTH
            s = ext[:, c0:c1]
            k = 1
            while k < w:
                s = s[:-k] + s[k:]
                k *= 2
            wsum = s[POOL_HALO + 1 - w:]
            cnt = jnp.minimum(t + 1.0, float(w))
            pooled = wsum / cnt - u[:, c0:c1]
            mixed = jnp.dot(pooled.astype(BF16), pw_ref[g], preferred_element_type=F32)
            pool_scr[:, c0:c1] = (mixed * ps_ref[:, c0:c1]).astype(BF16)

    up_pool = jnp.dot(pool_scr[...], wp_ref[...], preferred_element_type=F32)
    up_nsa = jnp.dot(nsa_ref[...], wn_ref[...], preferred_element_type=F32)
    merged = jax.nn.sigmoid(gp_ref[...].astype(F32)) * up_pool + jax.nn.sigmoid(gn_ref[...].astype(F32)) * up_nsa
    o_ref[...] = merged.astype(o_ref.dtype)


def _pool_merge(z, gp_off, gn_off, pool_w, pool_scale, w_up_pool, nsa_out, w_up_nsa, D):
    T = z.shape[0]
    tm = _pick(T, (512, 256, 128))
    tn = _pick(D, (1024, 512, 256, 128))
    assert gp_off % tn == 0 and gn_off % tn == 0 and tm % POOL_HALO == 0
    hb = tm // POOL_HALO
    return pl.pallas_call(
        functools.partial(_pool_merge_kernel, tm=tm),
        grid=(T // tm, D // tn),
        in_specs=[
            pl.BlockSpec((tm, POOL_WIDTH), lambda i, j: (i, 0)),
            pl.BlockSpec((POOL_HALO, POOL_WIDTH), lambda i, j: (jnp.maximum(i * hb - 1, 0), 0)),
            pl.BlockSpec((POOL_GROUPS, POOL_GROUP_WIDTH, POOL_GROUP_WIDTH), lambda i, j: (0, 0, 0)),
            pl.BlockSpec((1, POOL_WIDTH), lambda i, j: (0, 0)),
            pl.BlockSpec((POOL_WIDTH, tn), lambda i, j: (0, j)),
            pl.BlockSpec((tm, Q_WIDTH), lambda i, j: (i, 0)),
            pl.BlockSpec((Q_WIDTH, tn), lambda i, j: (0, j)),
            pl.BlockSpec((tm, tn), lambda i, j: (i, gp_off // tn + j)),
            pl.BlockSpec((tm, tn), lambda i, j: (i, gn_off // tn + j)),
        ],
        out_specs=pl.BlockSpec((tm, tn), lambda i, j: (i, j)),
        out_shape=jax.ShapeDtypeStruct((T, D), BF16),
        scratch_shapes=[pltpu.VMEM((tm, POOL_WIDTH), BF16)],
        compiler_params=_params("parallel", "arbitrary"),
        name="pool_merge",
    )(z, z, pool_w, pool_scale.reshape(1, POOL_WIDTH), w_up_pool, nsa_out, w_up_nsa, z, z)


def _compress_kernel(r_ref, w_ref, pos_ref, o_ref, *, n_cmp, transpose_out):
    half = CMP_STRIDE * HEAD_DIM
    r = r_ref[0]
    w = w_ref[...]
    top = jnp.dot(r, w[:half], preferred_element_type=F32)
    bot = jnp.dot(r, w[half:], preferred_element_type=F32)
    bias = jnp.dot(pos_ref[...], w, preferred_element_type=F32)
    ncp = top.shape[0]
    bot_next = jnp.concatenate([bot[1:], jnp.zeros((1, HEAD_DIM), F32)], axis=0)
    out = top + bot_next + bias
    row = lax.broadcasted_iota(jnp.int32, (ncp, 1), 0)
    out = jnp.where(row < n_cmp, out, 0.0)
    if transpose_out:
        o_ref[0] = out.T.astype(o_ref.dtype)
    else:
        o_ref[0] = out.astype(o_ref.dtype)


def _compress(r, w, pos, n_cmp, transpose_out):
    G, ncp, width = r.shape
    oshape = (G, HEAD_DIM, ncp) if transpose_out else (G, ncp, HEAD_DIM)
    return pl.pallas_call(
        functools.partial(_compress_kernel, n_cmp=n_cmp, transpose_out=transpose_out),
        grid=(G,),
        in_specs=[
            pl.BlockSpec((1, ncp, width), lambda g: (g, 0, 0)),
            pl.BlockSpec((CMP_BLOCK * HEAD_DIM, HEAD_DIM), lambda g: (0, 0)),
            pl.BlockSpec((1, CMP_BLOCK * HEAD_DIM), lambda g: (0, 0)),
        ],
        out_specs=pl.BlockSpec((1,) + oshape[1:], lambda g: (g, 0, 0)),
        out_shape=jax.ShapeDtypeStruct(oshape, BF16),
        compiler_params=_params("parallel"),
        name="nsa_compress",
    )(r, w, pos)


def _extract_topk(a, k, want_vals=False, exact=True):
    n = a.shape[0]
    idx = lax.broadcasted_iota(jnp.int32, a.shape, 0).astype(F32)
    rank = jnp.full(a.shape, float(k), F32)
    vals = []
    for r in range(k):
        m = jnp.max(a, axis=0, keepdims=True)
        hit = a == m
        if exact:
            first = jnp.min(jnp.where(hit, idx, float(n)), axis=0, keepdims=True)
            hit = idx == first
        rank = jnp.where(hit, float(r), rank)
        a = jnp.where(hit, -jnp.inf, a)
        vals.append(m)
    return (rank, vals) if want_vals else rank


def _cmp_select_kernel(slope_ref, q_ref, kc_ref, vct_ref, ovt_ref, o_ref, sel_ref, used_ref, *, tq):
    g = pl.program_id(0)
    i = pl.program_id(1)
    ncp = kc_ref.shape[1]
    ns = ovt_ref.shape[0]
    t = i * tq + lax.broadcasted_iota(jnp.int32, (1, tq), 1)
    n = lax.broadcasted_iota(jnp.int32, (ncp, 1), 0)
    dist = (t - (n * CMP_STRIDE + CMP_BLOCK - 1)).astype(F32)
    valid = dist >= 0.0
    kc = kc_ref[0]
    vct = vct_ref[0]
    q = q_ref[...]
    psum = jnp.zeros((ncp, tq), F32)
    for h in range(HEADS_PER_GROUP):
        qh = (q[:, h * HEAD_DIM:(h + 1) * HEAD_DIM].astype(F32) * (HEAD_DIM ** -0.5)).astype(BF16)
        s = lax.dot_general(kc, qh, _NT, preferred_element_type=F32)
        s = s - slope_ref[g * HEADS_PER_GROUP + h] * dist
        s = jnp.where(valid, s, NEG)
        m = jnp.max(s, axis=0, keepdims=True)
        e = jnp.exp(s - m)
        l = jnp.sum(e, axis=0, keepdims=True)
        p = jnp.where(valid, e / l, 0.0)
        ot = jnp.dot(vct, p.astype(BF16), preferred_element_type=F32)
        o_ref[:, h * HEAD_DIM:(h + 1) * HEAD_DIM] = ot.T.astype(o_ref.dtype)
        psum = psum + p
    p_hi = psum.astype(BF16)
    p_lo = (psum - p_hi.astype(F32)).astype(BF16)
    ovt = ovt_ref[...]
    imp = jnp.dot(ovt, p_hi, preferred_element_type=F32) + jnp.dot(ovt, p_lo, preferred_element_type=F32)
    kk = lax.broadcasted_iota(jnp.int32, (ns, 1), 0)
    cur = jnp.right_shift(t, SLC_BLOCK.bit_length() - 1)
    forced = (kk == 0) | (kk == cur) | (kk == cur - 1)
    score = jnp.where(forced, FORCE_SCORE, jnp.where(kk <= cur, imp, -1.0))
    k_sel = min(SLC_TOPK, ns)
    rank = _extract_topk(score, k_sel)
    sel = jnp.where((rank < float(k_sel)) & (kk <= cur), 1.0, 0.0)
    sel_t = sel.T
    sel_ref[0] = sel_t.astype(sel_ref.dtype)
    used_ref[0, 0] = jnp.max(sel_t, axis=0, keepdims=True)


def _cmp_select(slopes, z, q_off, kc, vct, ovt, T):
    tq = _pick(T, (256, 128))
    ncp = kc.shape[1]
    ns = ovt.shape[0]
    assert q_off % GROUP_Q_WIDTH == 0
    qb = q_off // GROUP_Q_WIDTH
    return pl.pallas_call(
        functools.partial(_cmp_select_kernel, tq=tq),
        grid=(N_KV_GROUPS, T // tq),
        in_specs=[
            pl.BlockSpec(memory_space=pltpu.SMEM),
            pl.BlockSpec((tq, GROUP_Q_WIDTH), lambda g, i: (i, qb + g)),
            pl.BlockSpec((1, ncp, HEAD_DIM), lambda g, i: (g, 0, 0)),
            pl.BlockSpec((1, HEAD_DIM, ncp), lambda g, i: (g, 0, 0)),
            pl.BlockSpec((ns, ncp), lambda g, i: (0, 0)),
        ],
        out_specs=[
            pl.BlockSpec((tq, GROUP_Q_WIDTH), lambda g, i: (i, g)),
            pl.BlockSpec((1, tq, ns), lambda g, i: (g, i, 0)),
            pl.BlockSpec((1, 1, 1, ns), lambda g, i: (g, i, 0, 0)),
        ],
        out_shape=[
            jax.ShapeDtypeStruct((T, Q_WIDTH), BF16),
            jax.ShapeDtypeStruct((N_KV_GROUPS, T, ns), BF16),
            jax.ShapeDtypeStruct((N_KV_GROUPS, T // tq, 1, ns), F32),
        ],
        compiler_params=_params("parallel", "parallel"),
        name="nsa_cmp_select",
    )(slopes, z, kc, vct, ovt)


def _nsa_attend_kernel(slope_ref, tile_ref, count_ref, q_ref, ks_ref, vs_ref, kw_ref, vw_ref, ocmp_ref, sel_ref,
                       exp_ref, gate_ref, o_ref, q_scr, acc_scr, *, tq, tk, span):
    g = pl.program_id(0)
    i = pl.program_id(1)
    t0 = i * tq
    hg = HEADS_PER_GROUP
    q = q_ref[...]
    for h in range(hg):
        qh = q[:, h * HEAD_DIM:(h + 1) * HEAD_DIM].astype(F32) * (HEAD_DIM ** -0.5)
        q_scr[h * tq:(h + 1) * tq, :] = qh.astype(BF16)
    q_all = q_scr[...]
    slopes = [slope_ref[g * hg + h] for h in range(hg)]

    sel = sel_ref[0]
    rel = (lax.broadcasted_iota(jnp.int32, (tq, tk), 0) - lax.broadcasted_iota(jnp.int32, (tq, tk), 1)).astype(F32)
    acc_scr[...] = jnp.zeros_like(acc_scr)
    step = g * pl.num_programs(1) + i
    list_base = step * exp_ref.shape[0]

    def kv_step(n, carry):
        ms, ls = carry
        j = tile_ref[list_base + n]
        s0 = pl.multiple_of(j * tk, tk)
        kt = ks_ref[pl.ds(s0, tk), :]
        vt = vs_ref[pl.ds(s0, tk), :]
        d = rel + (t0 - s0).astype(F32)
        chosen = jnp.dot(sel, exp_ref[j], preferred_element_type=F32)
        allow = (chosen > 0.5) & (d >= 0.0)
        s_all = lax.dot_general(q_all, kt, _NT, preferred_element_type=F32)
        kcol = (lax.broadcasted_iota(jnp.int32, (1, tk), 1) + (s0 - t0)).astype(F32)
        new_ms, new_ls, ps = [], [], []
        for h in range(hg):
            rows = slice(h * tq, (h + 1) * tq)
            s = s_all[rows] + slopes[h] * kcol
            s = jnp.where(allow, s, NEG)
            m_new = jnp.maximum(ms[h], jnp.max(s, axis=1, keepdims=True))
            alpha = jnp.exp(ms[h] - m_new)
            p = jnp.exp(s - m_new)
            new_ls.append(alpha * ls[h] + jnp.sum(p, axis=1, keepdims=True))
            new_ms.append(m_new)
            acc_scr[rows, :] = alpha * acc_scr[rows, :]
            ps.append(p.astype(BF16))
        acc_scr[...] += jnp.dot(jnp.concatenate(ps, axis=0), vt, preferred_element_type=F32)
        return tuple(new_ms), tuple(new_ls)

    init = (tuple(jnp.full((tq, 1), NEG, F32) for _ in range(hg)),
            tuple(jnp.zeros((tq, 1), F32) for _ in range(hg)))
    ms, ls = lax.fori_loop(0, count_ref[step], kv_step, init)

    w0 = pl.multiple_of(jnp.maximum(t0 - WINDOW, 0), 8)
    kw = kw_ref[pl.ds(w0, span), :]
    vw = vw_ref[pl.ds(w0, span), :]
    dw = ((t0 - w0) + lax.broadcasted_iota(jnp.int32, (tq, span), 0)
          - lax.broadcasted_iota(jnp.int32, (tq, span), 1)).astype(F32)
    allow_w = (dw >= 0.0) & (dw < float(WINDOW))
    sw_all = lax.dot_general(q_all, kw, _NT, preferred_element_type=F32)
    kcol_w = (lax.broadcasted_iota(jnp.int32, (1, span), 1) + (w0 - t0)).astype(F32)
    pws, lws = [], []
    for h in range(hg):
        s = sw_all[h * tq:(h + 1) * tq] + slopes[h] * kcol_w
        s = jnp.where(allow_w, s, NEG)
        m = jnp.max(s, axis=1, keepdims=True)
        p = jnp.exp(s - m)
        lws.append(jnp.sum(p, axis=1, keepdims=True))
        pws.append(p.astype(BF16))
    ow_all = jnp.dot(jnp.concatenate(pws, axis=0), vw, preferred_element_type=F32)

    gt = jax.nn.sigmoid(gate_ref[...])
    ocmp = ocmp_ref[...].astype(F32)
    for h in range(hg):
        o_c = ocmp[:, h * HEAD_DIM:(h + 1) * HEAD_DIM]
        o_s = acc_scr[h * tq:(h + 1) * tq, :] / ls[h]
        o_w = ow_all[h * tq:(h + 1) * tq] / lws[h]
        o = gt[:, 3 * h:3 * h + 1] * o_c + gt[:, 3 * h + 1:3 * h + 2] * o_s + gt[:, 3 * h + 2:3 * h + 3] * o_w
        o_ref[:, h * HEAD_DIM:(h + 1) * HEAD_DIM] = o.astype(o_ref.dtype)


def _nsa_attend(slopes, tile_list, tile_count, z, q_off, kv_off, o_cmp, sel, expand, gates, T):
    tq = _pick(T, (256, 128))
    n_exp, ns, tk = expand.shape
    span = WINDOW + tq
    assert T >= span and q_off % GROUP_Q_WIDTH == 0 and kv_off % HEAD_DIM == 0
    qb = q_off // GROUP_Q_WIDTH
    kvb = kv_off // HEAD_DIM
    G = N_KV_GROUPS

    def kv_spec(branch):
        return pl.BlockSpec((T, HEAD_DIM), lambda g, i: (0, kvb + branch * G + g))

    return pl.pallas_call(
        functools.partial(_nsa_attend_kernel, tq=tq, tk=tk, span=span),
        grid=(G, T // tq),
        in_specs=[
            pl.BlockSpec(memory_space=pltpu.SMEM),
            pl.BlockSpec(memory_space=pltpu.SMEM),
            pl.BlockSpec(memory_space=pltpu.SMEM),
            pl.BlockSpec((tq, GROUP_Q_WIDTH), lambda g, i: (i, qb + g)),
            kv_spec(2), kv_spec(3), kv_spec(4), kv_spec(5),
            pl.BlockSpec((tq, GROUP_Q_WIDTH), lambda g, i: (i, g)),
            pl.BlockSpec((1, tq, ns), lambda g, i: (g, i, 0)),
            pl.BlockSpec((n_exp, ns, tk), lambda g, i: (0, 0, 0)),
            pl.BlockSpec((tq, GATE_LANES), lambda g, i: (i, g)),
        ],
        out_specs=pl.BlockSpec((tq, GROUP_Q_WIDTH), lambda g, i: (i, g)),
        out_shape=jax.ShapeDtypeStruct((T, Q_WIDTH), BF16),
        scratch_shapes=[
            pltpu.VMEM((HEADS_PER_GROUP * tq, HEAD_DIM), BF16),
            pltpu.VMEM((HEADS_PER_GROUP * tq, HEAD_DIM), F32),
        ],
        compiler_params=_params("parallel", "arbitrary"),
        name="nsa_attend",
    )(slopes, tile_list, tile_count, z, z, z, z, z, o_cmp, sel, expand, gates)


_CAND_PIECES = [(0, 16), (1, 8), (2, 8), (3, 8), (4, 8), (5, 8), (6, 8), (7, 8)]


def _peer_select_head(q, k1, k2, exact):
    K = PEER_TOPK
    s1 = lax.dot_general(k1, q[:, :PEER_HALF], _NT, preferred_element_type=F32)
    s2 = lax.dot_general(k2, q[:, PEER_HALF:], _NT, preferred_element_type=F32)
    rank1, v1 = _extract_topk(s1, K, want_vals=True, exact=exact)
    rank2, v2 = _extract_topk(s2, K, want_vals=True, exact=exact)
    v1s = jnp.concatenate(v1, axis=0)
    v2s = jnp.concatenate(v2, axis=0)
    pieces = [v1[r1] + v2s[:L] for r1, L in _CAND_PIECES]
    pieces.append(v1s[8:] + v2[0])
    cand = jnp.concatenate(pieces, axis=0)
    crank = _extract_topk(cand, K, exact=exact)
    chosen = crank < float(K)
    top = v1[0] + v2[0]
    z = jnp.sum(jnp.where(chosen, jnp.exp(cand - top), 0.0), axis=0, keepdims=True)
    chosen_f = jnp.where(chosen, 1.0, 0.0)
    counts = []
    off = 0
    for _, L in _CAND_PIECES:
        counts.append(jnp.sum(chosen_f[off:off + L], axis=0, keepdims=True))
        off += L
    tail = chosen_f[off:off + 8]
    for r in range(8):
        counts.append(tail[r:r + 1])
    cnt = jnp.zeros_like(s1)
    for r1 in range(K):
        cnt = jnp.where(rank1 == float(r1), counts[r1], cnt)

    def n_ranked(rank):
        return jnp.sum(jnp.where(rank < float(K), 1.0, 0.0), axis=0, keepdims=True)

    tied = (jnp.abs(n_ranked(rank1) - float(K)) + jnp.abs(n_ranked(rank2) - float(K))
            + jnp.abs(n_ranked(crank) - float(K)))
    return (cnt, jnp.exp(s1 - v1[0]) / z, rank2, jnp.exp(s2 - v2[0])), tied


def _peer_select_kernel(q_ref, k1_ref, k2_ref, cnt_ref, f1_ref, rank2_ref, e2_ref, *, heads):
    k1 = k1_ref[...]
    k2 = k2_ref[...]
    out_refs = (cnt_ref, f1_ref, rank2_ref, e2_ref)

    def run(exact):
        tied = None
        for hh in range(heads):
            q = q_ref[:, hh * PEER_QDIM:(hh + 1) * PEER_QDIM].astype(BF16)
            outs, t = _peer_select_head(q, k1, k2, exact)
            for ref, val in zip(out_refs, outs):
                ref[hh, 0] = val
            tied = t if tied is None else tied + t
        return jnp.max(tied)

    any_tie = run(False)

    @pl.when(any_tie > 0.0)
    def _():
        run(True)


def _peer_select(qp, keys1, keys2):
    T = qp.shape[0]
    tp = _pick(T, (128,))
    heads = 2
    shp = jax.ShapeDtypeStruct((PEER_HEADS, T // tp, PEER_NKEYS, tp), F32)
    ospec = pl.BlockSpec((heads, 1, PEER_NKEYS, tp), lambda i, h: (h, i, 0, 0))
    return pl.pallas_call(
        functools.partial(_peer_select_kernel, heads=heads),
        grid=(T // tp, PEER_HEADS // heads),
        in_specs=[
            pl.BlockSpec((tp, heads * PEER_QDIM), lambda i, h: (i, h)),
            pl.BlockSpec((PEER_NKEYS, PEER_HALF), lambda i, h: (0, 0)),
            pl.BlockSpec((PEER_NKEYS, PEER_HALF), lambda i, h: (0, 0)),
        ],
        out_specs=[ospec, ospec, ospec, ospec],
        out_shape=[shp, shp, shp, shp],
        compiler_params=_params("parallel", "parallel"),
        name="peer_select",
    )(qp, keys1, keys2)


def _peer_dense_kernel(h_ref, u_ref, v_ref, cnt_ref, f1_ref, rank2_ref, e2_ref, o_ref,
                       w_scr, at_scr, *, ipc, tm, out_chunk):
    c = pl.program_id(1)
    nc = pl.num_programs(1)
    nk = PEER_NKEYS
    lt = LANES
    D = o_ref.shape[1]

    def build_gates(chunk, k, tb, tie):
        i1 = chunk * ipc + k
        lanes = slice(tb * lt, (tb + 1) * lt)
        w = jnp.zeros((nk, lt), F32)
        for h in range(PEER_HEADS):
            crow = cnt_ref[h, tb, pl.ds(i1, 1), :]
            if h == 0:
                crow = crow + (tie if tie.shape[1] == lt else tie[:, lanes])
            frow = f1_ref[h, tb, pl.ds(i1, 1), :]
            w = w + jnp.where(rank2_ref[h, tb] < crow, e2_ref[h, tb] * frow, 0.0)
        w_scr[k * nk:(k + 1) * nk, lanes] = w

    tiles = [(k, tb) for k in range(ipc) for tb in range(tm // lt)]
    early, late = tiles[:len(tiles) // 2], tiles[len(tiles) // 2:]
    zero_row = jnp.zeros((1, tm), F32)

    @pl.when(c == 0)
    def _():
        o_ref[...] = jnp.zeros_like(o_ref)
        for k, tb in early:
            build_gates(c, k, tb, zero_row)

    half = (ipc * nk) // 2
    h = h_ref[...]
    at_scr[:half, :] = lax.dot_general(u_ref[:half, :], h, _NT, preferred_element_type=F32)
    tie = at_scr[0:1, :] * 0.0
    at_scr[half:, :] = lax.dot_general(u_ref[half:, :], h, _NT, preferred_element_type=F32)
    for k, tb in late:
        build_gates(c, k, tb, tie)
    coef = (w_scr[...] * jax.nn.gelu(at_scr[...])).T.astype(BF16)
    nxt = jnp.minimum(c + 1, nc - 1)
    n_out = D // out_chunk
    for j in range(n_out):
        d0 = j * out_chunk
        o_ref[:, d0:d0 + out_chunk] += jnp.dot(coef, v_ref[:, d0:d0 + out_chunk], preferred_element_type=F32)
        tie2 = o_ref[0:1, d0:d0 + lt] * 0.0
        for k, tb in early[j::n_out]:
            build_gates(nxt, k, tb, tie2)


def _peer_dense(h2, u, v, cnt, f1, rank2, e2):
    T, D = h2.shape
    NE = u.shape[0]
    tm = _pick(T, (512, 256, 128))
    ipc = PEER_CHUNK_KEYS
    ec = ipc * PEER_NKEYS
    once = pl.Buffered(1)
    sel_spec = pl.BlockSpec((PEER_HEADS, tm // LANES, PEER_NKEYS, LANES), lambda i, c: (0, i, 0, 0),
                            pipeline_mode=once)
    return pl.pallas_call(
        functools.partial(_peer_dense_kernel, ipc=ipc, tm=tm, out_chunk=_pick(D, (512, 256, 128))),
        grid=(T // tm, NE // ec),
        in_specs=[
            pl.BlockSpec((tm, D), lambda i, c: (i, 0), pipeline_mode=once),
            pl.BlockSpec((ec, D), lambda i, c: (c, 0)),
            pl.BlockSpec((ec, D), lambda i, c: (c, 0)),
            sel_spec, sel_spec, sel_spec, sel_spec,
        ],
        out_specs=pl.BlockSpec((tm, D), lambda i, c: (i, 0), pipeline_mode=once),
        out_shape=jax.ShapeDtypeStruct((T, D), F32),
        scratch_shapes=[pltpu.VMEM((ec, tm), F32), pltpu.VMEM((ec, tm), F32)],
        compiler_params=_params("parallel", "arbitrary"),
        name="peer_dense",
    )(h2, u, v, cnt, f1, rank2, e2)


def _ple_final_kernel(x1_ref, peer_ref, gple_ref, wg_ref, p_ref, wp_ref, gfin_ref, o_ref, acc_scr, ss_scr,
                      *, tk, nk, tn):
    j = pl.program_id(1)
    D = acc_scr.shape[1]
    x2 = x1_ref[...] + peer_ref[...]
    lhs = (x2 * gple_ref[...]).astype(BF16)
    ssq = jnp.sum(x2 * x2, axis=-1, keepdims=True)

    @pl.when(j == 0)
    def _():
        acc_scr[...] = jnp.zeros_like(acc_scr)
        ss_scr[...] = jnp.zeros_like(ss_scr)

    ss_scr[...] += ssq
    for n0 in range(0, D, tn):
        acc_scr[:, n0:n0 + tn] += jnp.dot(lhs, wg_ref[:, n0:n0 + tn], preferred_element_type=F32)
    for jj in range(nk):
        @pl.when(j == jj)
        def _():
            o_ref[:, jj * tk:(jj + 1) * tk] = x2

    @pl.when(j == nk - 1)
    def _():
        inv = lax.rsqrt(ss_scr[...] / float(D) + RMS_EPS)
        pb = p_ref[...].astype(BF16)
        ss3 = jnp.zeros_like(inv)
        for n0 in range(0, D, tn):
            cols = slice(n0, n0 + tn)
            gate = jax.nn.sigmoid(acc_scr[:, cols] * inv)
            proj = jnp.dot(pb, wp_ref[:, cols], preferred_element_type=F32)
            x3 = o_ref[:, cols] + gate * proj
            o_ref[:, cols] = x3
            ss3 = ss3 + jnp.sum(x3 * x3, axis=-1, keepdims=True)
        inv3 = lax.rsqrt(ss3 / float(D) + RMS_EPS)
        for n0 in range(0, D, tn):
            cols = slice(n0, n0 + tn)
            o_ref[:, cols] = o_ref[:, cols] * inv3 * gfin_ref[:, cols]


def _ple_final(x1, peer, g_ple, w_gate, p, w_proj, g_final):
    T, D = x1.shape
    tm = _pick(T, (512, 256, 128))
    tk = _pick(D, (1024, 512, 256, 128))
    nk = D // tk
    once = pl.Buffered(1)
    return pl.pallas_call(
        functools.partial(_ple_final_kernel, tk=tk, nk=nk, tn=tk),
        grid=(T // tm, nk),
        in_specs=[
            pl.BlockSpec((tm, tk), lambda i, j: (i, j)),
            pl.BlockSpec((tm, tk), lambda i, j: (i, j)),
            pl.BlockSpec((1, tk), lambda i, j: (0, j)),
            pl.BlockSpec((tk, D), lambda i, j: (j, 0)),
            pl.BlockSpec((tm, PLE_DIM), lambda i, j: (i, 0)),
            pl.BlockSpec((PLE_DIM, D), lambda i, j: (0, 0), pipeline_mode=once),
            pl.BlockSpec((1, D), lambda i, j: (0, 0)),
        ],
        out_specs=pl.BlockSpec((tm, D), lambda i, j: (i, 0), pipeline_mode=once),
        out_shape=jax.ShapeDtypeStruct((T, D), F32),
        scratch_shapes=[pltpu.VMEM((tm, D), F32), pltpu.VMEM((tm, 1), F32)],
        compiler_params=_params("parallel", "arbitrary"),
        name="ple_final",
    )(x1, peer, g_ple.reshape(1, D), w_gate, p, w_proj, g_final.reshape(1, D))


def _alibi_slopes():
    return jnp.asarray(2.0 ** (-8.0 * np.arange(1, N_HEADS + 1) / N_HEADS), dtype=F32)


def _overlap_t(T):
    ncp = T // CMP_STRIDE
    n_cmp = (T - CMP_BLOCK) // CMP_STRIDE + 1
    n_sel = T // SLC_BLOCK
    c_start = np.arange(ncp) * CMP_STRIDE
    s_start = np.arange(n_sel) * SLC_BLOCK
    ov = (c_start[None, :] < s_start[:, None] + SLC_BLOCK) & (c_start[None, :] + CMP_BLOCK > s_start[:, None])
    ov = ov & (np.arange(ncp)[None, :] < n_cmp)
    return jnp.asarray(ov.astype(np.float32), dtype=BF16)


def _expand_blocks(T, tk):
    n_sel = T // SLC_BLOCK
    key_blk = (np.arange(T) // SLC_BLOCK).reshape(T // tk, 1, tk)
    e = key_blk == np.arange(n_sel).reshape(1, n_sel, 1)
    return jnp.asarray(e.astype(np.float32), dtype=BF16)


def _tile_lists(used, n_tiles):
    G, nq, _, ns = used.shape
    flags = used.reshape(G, nq, n_tiles, ns // n_tiles).max(axis=-1) > 0.0
    order = jnp.argsort(jnp.where(flags, 0, 1), axis=-1, stable=True)
    return order.astype(jnp.int32).reshape(-1), flags.sum(axis=-1).astype(jnp.int32).reshape(-1)


def _layer(x, p, norm_mix_g, w_in, pool_w, pool_scale, cmp_pos_k, cmp_pos_v, cmp_w_k, cmp_w_v,
           w_up_pool, w_up_nsa, w_out, norm_ffn_g, peer_w_q, peer_keys1, peer_keys2, peer_u, peer_v,
           norm_ple_g, ple_w_gate, ple_w_proj, norm_final_g):
    T, D = x.shape
    G = N_KV_GROUPS
    q_off = POOL_WIDTH
    kv_off = q_off + Q_WIDTH
    gates_off = kv_off + 6 * KV_WIDTH
    gp_src = gates_off + 3 * N_HEADS
    gp_off = gates_off
    gn_off = gp_off + D

    w_nk = jnp.swapaxes(w_in, 0, 1)
    w_main = jnp.concatenate([w_nk[:gates_off], w_nk[gp_src:]], axis=0).astype(BF16)
    w_g = w_nk[gates_off:gp_src].reshape(G, 3 * HEADS_PER_GROUP, D)
    w_g = jnp.pad(w_g, ((0, 0), (0, GATE_LANES - 3 * HEADS_PER_GROUP), (0, 0))).reshape(G * GATE_LANES, D)
    w_g = w_g.astype(BF16)

    h = _rmsnorm(x, norm_mix_g)
    z = _mm(h, w_main, BF16, b_is_nk=True, name="in_proj")
    gates = _mm(h, w_g, F32, b_is_nk=True, name="gate_proj")

    ncp = T // CMP_STRIDE
    n_cmp = (T - CMP_BLOCK) // CMP_STRIDE + 1

    def blocks16(col):
        a = z[:, col:col + KV_WIDTH].reshape(ncp, CMP_STRIDE, G, HEAD_DIM)
        return a.transpose(2, 0, 1, 3).reshape(G, ncp, CMP_STRIDE * HEAD_DIM)

    kc = _compress(blocks16(kv_off), cmp_w_k.astype(BF16), cmp_pos_k.reshape(1, -1).astype(BF16), n_cmp, False)
    vct = _compress(blocks16(kv_off + KV_WIDTH), cmp_w_v.astype(BF16), cmp_pos_v.reshape(1, -1).astype(BF16),
                    n_cmp, True)
    slopes = _alibi_slopes()
    o_cmp, sel, used = _cmp_select(slopes, z, q_off, kc, vct, _overlap_t(T), T)
    tk = _pick(T, (512, 256))
    tile_list, tile_count = _tile_lists(used, T // tk)
    nsa_out = _nsa_attend(slopes, tile_list, tile_count, z, q_off, kv_off, o_cmp, sel, _expand_blocks(T, tk), gates, T)

    merged = _pool_merge(z, gp_off, gn_off, pool_w.astype(BF16), pool_scale, w_up_pool.astype(BF16),
                         nsa_out, w_up_nsa.astype(BF16), D)
    x1 = _mm(merged, w_out.astype(BF16), F32, res=x, name="out_proj")

    h2 = _rmsnorm(x1, norm_ffn_g)
    qp = _mm(h2, peer_w_q.astype(BF16), F32, name="peer_q")
    cnt, f1, rank2, e2 = _peer_select(qp, peer_keys1.astype(BF16), peer_keys2.astype(BF16))
    peer = _peer_dense(h2, peer_u.astype(BF16), peer_v.astype(BF16), cnt, f1, rank2, e2)

    return _ple_final(x1, peer, norm_ple_g, ple_w_gate.astype(BF16), p, ple_w_proj.astype(BF16), norm_final_g)


def kernel(x, p, norm_mix_g, w_in, pool_w, pool_scale, cmp_pos_k, cmp_pos_v, cmp_w_k, cmp_w_v, w_up_pool, w_up_nsa, w_out, norm_ffn_g, peer_w_q, peer_keys1, peer_keys2, peer_u, peer_v, norm_ple_g, ple_w_gate, ple_w_proj, norm_final_g):
    B, T, D = x.shape
    depth = w_in.shape[0]
    assert depth == 1, "the final rmsnorm is fused into the layer's last kernel"
    outs = []
    for b in range(B):
        outs.append(_layer(
            x[b], p[0, b], norm_mix_g[0], w_in[0], pool_w[0], pool_scale[0], cmp_pos_k[0], cmp_pos_v[0],
            cmp_w_k[0], cmp_w_v[0], w_up_pool[0], w_up_nsa[0], w_out[0], norm_ffn_g[0], peer_w_q[0],
            peer_keys1[0], peer_keys2[0], peer_u[0], peer_v[0], norm_ple_g[0], ple_w_gate[0], ple_w_proj[0],
            norm_final_g))
    return outs[0][None] if B == 1 else jnp.stack(outs, axis=0)
```

```python
import functools

import numpy as np
import jax
import jax.numpy as jnp
from jax import lax
from jax.experimental import pallas as pl
from jax.experimental.pallas import tpu as pltpu

F32 = jnp.float32
BF16 = jnp.bfloat16

POOL_GROUPS = 4
POOL_GROUP_WIDTH = 512
POOL_WIDTH = POOL_GROUPS * POOL_GROUP_WIDTH
POOL_WINDOWS = (2, 4, 8, 16)
POOL_HALO = 16
N_HEADS = 16
N_KV_GROUPS = 4
HEADS_PER_GROUP = N_HEADS // N_KV_GROUPS
HEAD_DIM = 128
Q_WIDTH = N_HEADS * HEAD_DIM
KV_WIDTH = N_KV_GROUPS * HEAD_DIM
GROUP_Q_WIDTH = HEADS_PER_GROUP * HEAD_DIM
CMP_BLOCK = 32
CMP_STRIDE = 16
SLC_BLOCK = 64
SLC_TOPK = 16
WINDOW = 512
FORCE_SCORE = 1e4
PEER_HEADS = 8
PEER_NKEYS = 128
PEER_QDIM = 256
PEER_HALF = PEER_QDIM // 2
PEER_TOPK = 16
PEER_CHUNK_KEYS = 4
PLE_DIM = 256
RMS_EPS = 1e-6
NEG = -1e30
GATE_LANES = 128

LANES = 128
V7X_VMEM_BYTES = 64 * 1024 * 1024
VMEM_LIMIT = V7X_VMEM_BYTES * 7 // 8

_NT = (((1,), (1,)), ((), ()))


def _params(*sem):
    return pltpu.CompilerParams(dimension_semantics=sem, vmem_limit_bytes=VMEM_LIMIT)


def _pick(n, cands):
    for c in cands:
        if n % c == 0:
            return c
    raise ValueError(f"no tile in {cands} divides {n}")


def _rmsnorm_kernel(x_ref, g_ref, o_ref):
    x = x_ref[...]
    ms = jnp.mean(x * x, axis=-1, keepdims=True)
    o_ref[...] = (x * lax.rsqrt(ms + RMS_EPS) * g_ref[...]).astype(o_ref.dtype)


def _rmsnorm(x, g, out_dtype=BF16):
    T, D = x.shape
    tm = _pick(T, (256, 128, 64, 8))
    return pl.pallas_call(
        _rmsnorm_kernel,
        grid=(T // tm,),
        in_specs=[pl.BlockSpec((tm, D), lambda i: (i, 0)), pl.BlockSpec((1, D), lambda i: (0, 0))],
        out_specs=pl.BlockSpec((tm, D), lambda i: (i, 0)),
        out_shape=jax.ShapeDtypeStruct((T, D), out_dtype),
        compiler_params=_params("parallel"),
        name="rmsnorm",
    )(x, g.reshape(1, D))


def _mm_kernel(a_ref, b_ref, *rest, has_res, b_is_nk):
    o_ref = rest[-1]
    if b_is_nk:
        acc = lax.dot_general(a_ref[...], b_ref[...], _NT, preferred_element_type=F32)
    else:
        acc = jnp.dot(a_ref[...], b_ref[...], preferred_element_type=F32)
    if has_res:
        acc = acc + rest[0][...]
    o_ref[...] = acc.astype(o_ref.dtype)


def _mm(a, b, out_dtype, res=None, b_is_nk=False, name="mm"):
    M, K = a.shape
    N = b.shape[0] if b_is_nk else b.shape[1]
    tm = _pick(M, (1024, 512, 256, 128))
    tn = _pick(N, (1024, 512, 256, 128))
    b_spec = pl.BlockSpec((tn, K), lambda j, i: (j, 0)) if b_is_nk else pl.BlockSpec((K, tn), lambda j, i: (0, j))
    in_specs = [pl.BlockSpec((tm, K), lambda j, i: (i, 0)), b_spec]
    args = [a, b]
    if res is not None:
        in_specs.append(pl.BlockSpec((tm, tn), lambda j, i: (i, j)))
        args.append(res)
    return pl.pallas_call(
        functools.partial(_mm_kernel, has_res=res is not None, b_is_nk=b_is_nk),
        grid=(N // tn, M // tm),
        in_specs=in_specs,
        out_specs=pl.BlockSpec((tm, tn), lambda j, i: (i, j)),
        out_shape=jax.ShapeDtypeStruct((M, N), out_dtype),
        compiler_params=_params("parallel", "parallel"),
        name=name,
    )(*args)


def _pool_merge_kernel(u_ref, halo_ref, pw_ref, ps_ref, wp_ref, nsa_ref, wn_ref, gp_ref, gn_ref,
                       o_ref, pool_scr, *, tm):
    i = pl.program_id(0)
    j = pl.program_id(1)

    @pl.when(j == 0)
    def _():
        u = u_ref[...].astype(F32)
        halo = halo_ref[...].astype(F32)
        halo = jnp.where(i == 0, 0.0, halo)
        ext = jnp.concatenate([halo, u], axis=0)
        t = (i * tm + lax.broadcasted_iota(jnp.int32, (tm, 1), 0)).astype(F32)
        for g, w in enumerate(POOL_WINDOWS):
            c0, c1 = g * POOL_GROUP_WIDTH, (g + 1) * POOL_GROUP_WIDTH
            s = ext[:, c0:c1]
            k = 1
            while k < w:
                s = s[:-k] + s[k:]
                k *= 2
            wsum = s[POOL_HALO + 1 - w:]
            cnt = jnp.minimum(t + 1.0, float(w))
            pooled = wsum / cnt - u[:, c0:c1]
            mixed = jnp.dot(pooled.astype(BF16), pw_ref[g], preferred_element_type=F32)
            pool_scr[:, c0:c1] = (mixed * ps_ref[:, c0:c1]).astype(BF16)

    up_pool = jnp.dot(pool_scr[...], wp_ref[...], preferred_element_type=F32)
    up_nsa = jnp.dot(nsa_ref[...], wn_ref[...], preferred_element_type=F32)
    merged = jax.nn.sigmoid(gp_ref[...].astype(F32)) * up_pool + jax.nn.sigmoid(gn_ref[...].astype(F32)) * up_nsa
    o_ref[...] = merged.astype(o_ref.dtype)


def _pool_merge(z, zg, pool_w, pool_scale, w_up_pool, nsa_out, w_up_nsa, D):
    T = z.shape[0]
    tm = _pick(T, (512, 256, 128))
    tn = _pick(D, (1024, 512, 256, 128))
    assert tm % POOL_HALO == 0
    hb = tm // POOL_HALO
    return pl.pallas_call(
        functools.partial(_pool_merge_kernel, tm=tm),
        grid=(T // tm, D // tn),
        in_specs=[
            pl.BlockSpec((tm, POOL_WIDTH), lambda i, j: (i, 0)),
            pl.BlockSpec((POOL_HALO, POOL_WIDTH), lambda i, j: (jnp.maximum(i * hb - 1, 0), 0)),
            pl.BlockSpec((POOL_GROUPS, POOL_GROUP_WIDTH, POOL_GROUP_WIDTH), lambda i, j: (0, 0, 0)),
            pl.BlockSpec((1, POOL_WIDTH), lambda i, j: (0, 0)),
            pl.BlockSpec((POOL_WIDTH, tn), lambda i, j: (0, j)),
            pl.BlockSpec((tm, Q_WIDTH), lambda i, j: (i, 0)),
            pl.BlockSpec((Q_WIDTH, tn), lambda i, j: (0, j)),
            pl.BlockSpec((tm, tn), lambda i, j: (i, j)),
            pl.BlockSpec((tm, tn), lambda i, j: (i, D // tn + j)),
        ],
        out_specs=pl.BlockSpec((tm, tn), lambda i, j: (i, j)),
        out_shape=jax.ShapeDtypeStruct((T, D), BF16),
        scratch_shapes=[pltpu.VMEM((tm, POOL_WIDTH), BF16)],
        compiler_params=_params("parallel", "arbitrary"),
        name="pool_merge",
    )(z, z, pool_w, pool_scale.reshape(1, POOL_WIDTH), w_up_pool, nsa_out, w_up_nsa, zg, zg)


def _compress_kernel(r_ref, w_ref, pos_ref, o_ref, *, n_cmp, transpose_out):
    half = CMP_STRIDE * HEAD_DIM
    r = r_ref[0]
    w = w_ref[...]
    top = jnp.dot(r, w[:half], preferred_element_type=F32)
    bot = jnp.dot(r, w[half:], preferred_element_type=F32)
    bias = jnp.dot(pos_ref[...], w, preferred_element_type=F32)
    ncp = top.shape[0]
    bot_next = jnp.concatenate([bot[1:], jnp.zeros((1, HEAD_DIM), F32)], axis=0)
    out = top + bot_next + bias
    row = lax.broadcasted_iota(jnp.int32, (ncp, 1), 0)
    out = jnp.where(row < n_cmp, out, 0.0)
    if transpose_out:
        o_ref[0] = out.T.astype(o_ref.dtype)
    else:
        o_ref[0] = out.astype(o_ref.dtype)


def _compress(r, w, pos, n_cmp, transpose_out):
    G, ncp, width = r.shape
    oshape = (G, HEAD_DIM, ncp) if transpose_out else (G, ncp, HEAD_DIM)
    return pl.pallas_call(
        functools.partial(_compress_kernel, n_cmp=n_cmp, transpose_out=transpose_out),
        grid=(G,),
        in_specs=[
            pl.BlockSpec((1, ncp, width), lambda g: (g, 0, 0)),
            pl.BlockSpec((CMP_BLOCK * HEAD_DIM, HEAD_DIM), lambda g: (0, 0)),
            pl.BlockSpec((1, CMP_BLOCK * HEAD_DIM), lambda g: (0, 0)),
        ],
        out_specs=pl.BlockSpec((1,) + oshape[1:], lambda g: (g, 0, 0)),
        out_shape=jax.ShapeDtypeStruct(oshape, BF16),
        compiler_params=_params("parallel"),
        name="nsa_compress",
    )(r, w, pos)


def _extract_topk(a, k, want_vals=False, exact=True):
    n = a.shape[0]
    idx = lax.broadcasted_iota(jnp.int32, a.shape, 0).astype(F32)
    rank = jnp.full(a.shape, float(k), F32)
    vals = []
    for r in range(k):
        m = jnp.max(a, axis=0, keepdims=True)
        hit = a == m
        if exact:
            first = jnp.min(jnp.where(hit, idx, float(n)), axis=0, keepdims=True)
            hit = idx == first
        rank = jnp.where(hit, float(r), rank)
        a = jnp.where(hit, -jnp.inf, a)
        vals.append(m)
    return (rank, vals) if want_vals else rank


def _cmp_select_kernel(slope_ref, q_ref, kc_ref, vct_ref, ovt_ref, o_ref, sel_ref, used_ref, *, tq):
    g = pl.program_id(0)
    i = pl.program_id(1)
    ncp = kc_ref.shape[1]
    ns = ovt_ref.shape[0]
    t = i * tq + lax.broadcasted_iota(jnp.int32, (1, tq), 1)
    n = lax.broadcasted_iota(jnp.int32, (ncp, 1), 0)
    dist = (t - (n * CMP_STRIDE + CMP_BLOCK - 1)).astype(F32)
    valid = dist >= 0.0
    kc = kc_ref[0]
    vct = vct_ref[0]
    q = q_ref[...]
    psum = jnp.zeros((ncp, tq), F32)
    for h in range(HEADS_PER_GROUP):
        qh = (q[:, h * HEAD_DIM:(h + 1) * HEAD_DIM].astype(F32) * (HEAD_DIM ** -0.5)).astype(BF16)
        s = lax.dot_general(kc, qh, _NT, preferred_element_type=F32)
        s = s - slope_ref[g * HEADS_PER_GROUP + h] * dist
        s = jnp.where(valid, s, NEG)
        m = jnp.max(s, axis=0, keepdims=True)
        e = jnp.exp(s - m)
        l = jnp.sum(e, axis=0, keepdims=True)
        p = jnp.where(valid, e / l, 0.0)
        ot = jnp.dot(vct, p.astype(BF16), preferred_element_type=F32)
        o_ref[:, h * HEAD_DIM:(h + 1) * HEAD_DIM] = ot.T.astype(o_ref.dtype)
        psum = psum + p
    p_hi = psum.astype(BF16)
    p_lo = (psum - p_hi.astype(F32)).astype(BF16)
    ovt = ovt_ref[...]
    imp = jnp.dot(ovt, p_hi, preferred_element_type=F32) + jnp.dot(ovt, p_lo, preferred_element_type=F32)
    kk = lax.broadcasted_iota(jnp.int32, (ns, 1), 0)
    cur = jnp.right_shift(t, SLC_BLOCK.bit_length() - 1)
    forced = (kk == 0) | (kk == cur) | (kk == cur - 1)
    score = jnp.where(forced, FORCE_SCORE, jnp.where(kk <= cur, imp, -1.0))
    k_sel = min(SLC_TOPK, ns)
    rank = _extract_topk(score, k_sel)
    sel = jnp.where((rank < float(k_sel)) & (kk <= cur), 1.0, 0.0)
    sel_t = sel.T
    sel_ref[0] = sel_t.astype(sel_ref.dtype)
    used_ref[0, 0] = jnp.max(sel_t, axis=0, keepdims=True)


def _cmp_select(slopes, z, q_off, kc, vct, ovt, T):
    tq = _pick(T, (256, 128))
    ncp = kc.shape[1]
    ns = ovt.shape[0]
    assert q_off % GROUP_Q_WIDTH == 0
    qb = q_off // GROUP_Q_WIDTH
    return pl.pallas_call(
        functools.partial(_cmp_select_kernel, tq=tq),
        grid=(N_KV_GROUPS, T // tq),
        in_specs=[
            pl.BlockSpec(memory_space=pltpu.SMEM),
            pl.BlockSpec((tq, GROUP_Q_WIDTH), lambda g, i: (i, qb + g)),
            pl.BlockSpec((1, ncp, HEAD_DIM), lambda g, i: (g, 0, 0)),
            pl.BlockSpec((1, HEAD_DIM, ncp), lambda g, i: (g, 0, 0)),
            pl.BlockSpec((ns, ncp), lambda g, i: (0, 0)),
        ],
        out_specs=[
            pl.BlockSpec((tq, GROUP_Q_WIDTH), lambda g, i: (i, g)),
            pl.BlockSpec((1, tq, ns), lambda g, i: (g, i, 0)),
            pl.BlockSpec((1, 1, 1, ns), lambda g, i: (g, i, 0, 0)),
        ],
        out_shape=[
            jax.ShapeDtypeStruct((T, Q_WIDTH), BF16),
            jax.ShapeDtypeStruct((N_KV_GROUPS, T, ns), BF16),
            jax.ShapeDtypeStruct((N_KV_GROUPS, T // tq, 1, ns), F32),
        ],
        compiler_params=_params("parallel", "parallel"),
        name="nsa_cmp_select",
    )(slopes, z, kc, vct, ovt)


def _nsa_attend_kernel(slope_ref, tile_ref, count_ref, q_ref, ks_ref, vs_ref, kw_ref, vw_ref, ocmp_ref, sel_ref,
                       exp_ref, gate_ref, o_ref, q_scr, acc_scr, *, tq, tk, span):
    g = pl.program_id(0)
    i = pl.program_id(1)
    t0 = i * tq
    hg = HEADS_PER_GROUP
    q = q_ref[...]
    for h in range(hg):
        qh = q[:, h * HEAD_DIM:(h + 1) * HEAD_DIM].astype(F32) * (HEAD_DIM ** -0.5)
        q_scr[h * tq:(h + 1) * tq, :] = qh.astype(BF16)
    q_all = q_scr[...]
    slopes = [slope_ref[g * hg + h] for h in range(hg)]

    sel = sel_ref[0]
    rel = (lax.broadcasted_iota(jnp.int32, (tq, tk), 0) - lax.broadcasted_iota(jnp.int32, (tq, tk), 1)).astype(F32)
    acc_scr[...] = jnp.zeros_like(acc_scr)
    step = g * pl.num_programs(1) + i
    list_base = step * exp_ref.shape[0]

    def kv_step(n, carry):
        ms, ls = carry
        j = tile_ref[list_base + n]
        s0 = pl.multiple_of(j * tk, tk)
        kt = ks_ref[pl.ds(s0, tk), :]
        vt = vs_ref[pl.ds(s0, tk), :]
        d = rel + (t0 - s0).astype(F32)
        chosen = jnp.dot(sel, exp_ref[j], preferred_element_type=F32)
        allow = (chosen > 0.5) & (d >= 0.0)
        s_all = lax.dot_general(q_all, kt, _NT, preferred_element_type=F32)
        kcol = (lax.broadcasted_iota(jnp.int32, (1, tk), 1) + (s0 - t0)).astype(F32)
        new_ms, new_ls, ps = [], [], []
        for h in range(hg):
            rows = slice(h * tq, (h + 1) * tq)
            s = s_all[rows] + slopes[h] * kcol
            s = jnp.where(allow, s, NEG)
            m_new = jnp.maximum(ms[h], jnp.max(s, axis=1, keepdims=True))
            alpha = jnp.exp(ms[h] - m_new)
            p = jnp.exp(s - m_new)
            new_ls.append(alpha * ls[h] + jnp.sum(p, axis=1, keepdims=True))
            new_ms.append(m_new)
            acc_scr[rows, :] = alpha * acc_scr[rows, :]
            ps.append(p.astype(BF16))
        acc_scr[...] += jnp.dot(jnp.concatenate(ps, axis=0), vt, preferred_element_type=F32)
        return tuple(new_ms), tuple(new_ls)

    init = (tuple(jnp.full((tq, 1), NEG, F32) for _ in range(hg)),
            tuple(jnp.zeros((tq, 1), F32) for _ in range(hg)))
    ms, ls = lax.fori_loop(0, count_ref[step], kv_step, init)

    w0 = pl.multiple_of(jnp.maximum(t0 - WINDOW, 0), 8)
    kw = kw_ref[pl.ds(w0, span), :]
    vw = vw_ref[pl.ds(w0, span), :]
    dw = ((t0 - w0) + lax.broadcasted_iota(jnp.int32, (tq, span), 0)
          - lax.broadcasted_iota(jnp.int32, (tq, span), 1)).astype(F32)
    allow_w = (dw >= 0.0) & (dw < float(WINDOW))
    sw_all = lax.dot_general(q_all, kw, _NT, preferred_element_type=F32)
    kcol_w = (lax.broadcasted_iota(jnp.int32, (1, span), 1) + (w0 - t0)).astype(F32)
    pws, lws = [], []
    for h in range(hg):
        s = sw_all[h * tq:(h + 1) * tq] + slopes[h] * kcol_w
        s = jnp.where(allow_w, s, NEG)
        m = jnp.max(s, axis=1, keepdims=True)
        p = jnp.exp(s - m)
        lws.append(jnp.sum(p, axis=1, keepdims=True))
        pws.append(p.astype(BF16))
    ow_all = jnp.dot(jnp.concatenate(pws, axis=0), vw, preferred_element_type=F32)

    gt = jax.nn.sigmoid(gate_ref[...])
    ocmp = ocmp_ref[...].astype(F32)
    for h in range(hg):
        o_c = ocmp[:, h * HEAD_DIM:(h + 1) * HEAD_DIM]
        o_s = acc_scr[h * tq:(h + 1) * tq, :] / ls[h]
        o_w = ow_all[h * tq:(h + 1) * tq] / lws[h]
        o = gt[:, 3 * h:3 * h + 1] * o_c + gt[:, 3 * h + 1:3 * h + 2] * o_s + gt[:, 3 * h + 2:3 * h + 3] * o_w
        o_ref[:, h * HEAD_DIM:(h + 1) * HEAD_DIM] = o.astype(o_ref.dtype)


def _nsa_attend(slopes, tile_list, tile_count, z, q_off, kv_off, o_cmp, sel, expand, gates, T):
    tq = _pick(T, (256, 128))
    n_exp, ns, tk = expand.shape
    span = WINDOW + tq
    assert T >= span and q_off % GROUP_Q_WIDTH == 0 and kv_off % HEAD_DIM == 0
    qb = q_off // GROUP_Q_WIDTH
    kvb = kv_off // HEAD_DIM
    G = N_KV_GROUPS

    def kv_spec(branch):
        return pl.BlockSpec((T, HEAD_DIM), lambda g, i: (0, kvb + branch * G + g))

    return pl.pallas_call(
        functools.partial(_nsa_attend_kernel, tq=tq, tk=tk, span=span),
        grid=(G, T // tq),
        in_specs=[
            pl.BlockSpec(memory_space=pltpu.SMEM),
            pl.BlockSpec(memory_space=pltpu.SMEM),
            pl.BlockSpec(memory_space=pltpu.SMEM),
            pl.BlockSpec((tq, GROUP_Q_WIDTH), lambda g, i: (i, qb + g)),
            kv_spec(2), kv_spec(3), kv_spec(4), kv_spec(5),
            pl.BlockSpec((tq, GROUP_Q_WIDTH), lambda g, i: (i, g)),
            pl.BlockSpec((1, tq, ns), lambda g, i: (g, i, 0)),
            pl.BlockSpec((n_exp, ns, tk), lambda g, i: (0, 0, 0)),
            pl.BlockSpec((tq, GATE_LANES), lambda g, i: (i, g)),
        ],
        out_specs=pl.BlockSpec((tq, GROUP_Q_WIDTH), lambda g, i: (i, g)),
        out_shape=jax.ShapeDtypeStruct((T, Q_WIDTH), BF16),
        scratch_shapes=[
            pltpu.VMEM((HEADS_PER_GROUP * tq, HEAD_DIM), BF16),
            pltpu.VMEM((HEADS_PER_GROUP * tq, HEAD_DIM), F32),
        ],
        compiler_params=_params("parallel", "arbitrary"),
        name="nsa_attend",
    )(slopes, tile_list, tile_count, z, z, z, z, z, o_cmp, sel, expand, gates)


_CAND_PIECES = [(0, 16), (1, 8), (2, 8), (3, 8), (4, 8), (5, 8), (6, 8), (7, 8)]


def _peer_select_head(q, k1, k2, exact):
    K = PEER_TOPK
    s1 = lax.dot_general(k1, q[:, :PEER_HALF], _NT, preferred_element_type=F32)
    s2 = lax.dot_general(k2, q[:, PEER_HALF:], _NT, preferred_element_type=F32)
    rank1, v1 = _extract_topk(s1, K, want_vals=True, exact=exact)
    rank2, v2 = _extract_topk(s2, K, want_vals=True, exact=exact)
    v1s = jnp.concatenate(v1, axis=0)
    v2s = jnp.concatenate(v2, axis=0)
    pieces = [v1[r1] + v2s[:L] for r1, L in _CAND_PIECES]
    pieces.append(v1s[8:] + v2[0])
    cand = jnp.concatenate(pieces, axis=0)
    crank = _extract_topk(cand, K, exact=exact)
    chosen = crank < float(K)
    top = v1[0] + v2[0]
    z = jnp.sum(jnp.where(chosen, jnp.exp(cand - top), 0.0), axis=0, keepdims=True)
    chosen_f = jnp.where(chosen, 1.0, 0.0)
    counts = []
    off = 0
    for _, L in _CAND_PIECES:
        counts.append(jnp.sum(chosen_f[off:off + L], axis=0, keepdims=True))
        off += L
    tail = chosen_f[off:off + 8]
    for r in range(8):
        counts.append(tail[r:r + 1])
    cnt = jnp.zeros_like(s1)
    for r1 in range(K):
        cnt = jnp.where(rank1 == float(r1), counts[r1], cnt)

    def n_ranked(rank):
        return jnp.sum(jnp.where(rank < float(K), 1.0, 0.0), axis=0, keepdims=True)

    tied = (jnp.abs(n_ranked(rank1) - float(K)) + jnp.abs(n_ranked(rank2) - float(K))
            + jnp.abs(n_ranked(crank) - float(K)))
    return (cnt, jnp.exp(s1 - v1[0]) / z, rank2, jnp.exp(s2 - v2[0])), tied


def _peer_select_kernel(q_ref, k1_ref, k2_ref, cnt_ref, f1_ref, rank2_ref, e2_ref, *, heads):
    k1 = k1_ref[...]
    k2 = k2_ref[...]
    out_refs = (cnt_ref, f1_ref, rank2_ref, e2_ref)

    def run(exact):
        tied = None
        for hh in range(heads):
            q = q_ref[:, hh * PEER_QDIM:(hh + 1) * PEER_QDIM].astype(BF16)
            outs, t = _peer_select_head(q, k1, k2, exact)
            for ref, val in zip(out_refs, outs):
                ref[hh, 0] = val
            tied = t if tied is None else tied + t
        return jnp.max(tied)

    any_tie = run(False)

    @pl.when(any_tie > 0.0)
    def _():
        run(True)


def _peer_select(qp, keys1, keys2):
    T = qp.shape[0]
    tp = _pick(T, (128,))
    heads = 2
    shp = jax.ShapeDtypeStruct((PEER_HEADS, T // tp, PEER_NKEYS, tp), F32)
    ospec = pl.BlockSpec((heads, 1, PEER_NKEYS, tp), lambda i, h: (h, i, 0, 0))
    return pl.pallas_call(
        functools.partial(_peer_select_kernel, heads=heads),
        grid=(T // tp, PEER_HEADS // heads),
        in_specs=[
            pl.BlockSpec((tp, heads * PEER_QDIM), lambda i, h: (i, h)),
            pl.BlockSpec((PEER_NKEYS, PEER_HALF), lambda i, h: (0, 0)),
            pl.BlockSpec((PEER_NKEYS, PEER_HALF), lambda i, h: (0, 0)),
        ],
        out_specs=[ospec, ospec, ospec, ospec],
        out_shape=[shp, shp, shp, shp],
        compiler_params=_params("parallel", "parallel"),
        name="peer_select",
    )(qp, keys1, keys2)


def _peer_dense_kernel(h_ref, u_ref, v_ref, cnt_ref, f1_ref, rank2_ref, e2_ref, o_ref,
                       w_scr, at_scr, ht_scr, *, ipc, tm, out_chunk):
    c = pl.program_id(1)
    nc = pl.num_programs(1)
    nk = PEER_NKEYS
    lt = LANES
    D = o_ref.shape[1]

    def build_gates(chunk, k, tb, tie):
        i1 = chunk * ipc + k
        lanes = slice(tb * lt, (tb + 1) * lt)
        w = jnp.zeros((nk, lt), F32)
        for h in range(PEER_HEADS):
            crow = cnt_ref[h, tb, pl.ds(i1, 1), :]
            if h == 0:
                crow = crow + (tie if tie.shape[1] == lt else tie[:, lanes])
            frow = f1_ref[h, tb, pl.ds(i1, 1), :]
            w = w + jnp.where(rank2_ref[h, tb] < crow, e2_ref[h, tb] * frow, 0.0)
        w_scr[k * nk:(k + 1) * nk, lanes] = w

    tiles = [(k, tb) for k in range(ipc) for tb in range(tm // lt)]
    early, late = tiles[:len(tiles) // 2], tiles[len(tiles) // 2:]
    zero_row = jnp.zeros((1, tm), F32)

    @pl.when(c == 0)
    def _():
        o_ref[...] = jnp.zeros_like(o_ref)
        for d0 in range(0, D, out_chunk):
            ht_scr[d0:d0 + out_chunk, :] = h_ref[:, d0:d0 + out_chunk].astype(F32).T.astype(BF16)
        for k, tb in early:
            build_gates(c, k, tb, zero_row)

    half = (ipc * nk) // 2
    ht = ht_scr[...]
    at_scr[:half, :] = jnp.dot(u_ref[:half, :], ht, preferred_element_type=F32)
    tie = at_scr[0:1, :] * 0.0
    at_scr[half:, :] = jnp.dot(u_ref[half:, :], ht, preferred_element_type=F32)
    for k, tb in late:
        build_gates(c, k, tb, tie)
    coef = (w_scr[...] * jax.nn.gelu(at_scr[...])).T.astype(BF16)
    nxt = jnp.minimum(c + 1, nc - 1)
    n_out = D // out_chunk
    for j in range(n_out):
        d0 = j * out_chunk
        o_ref[:, d0:d0 + out_chunk] += jnp.dot(coef, v_ref[:, d0:d0 + out_chunk], preferred_element_type=F32)
        tie2 = o_ref[0:1, d0:d0 + lt] * 0.0
        for k, tb in early[j::n_out]:
            build_gates(nxt, k, tb, tie2)


def _peer_dense(h2, u, v, cnt, f1, rank2, e2):
    T, D = h2.shape
    NE = u.shape[0]
    tm = _pick(T, (512, 256, 128))
    ipc = PEER_CHUNK_KEYS
    ec = ipc * PEER_NKEYS
    once = pl.Buffered(1)
    sel_spec = pl.BlockSpec((PEER_HEADS, tm // LANES, PEER_NKEYS, LANES), lambda i, c: (0, i, 0, 0),
                            pipeline_mode=once)
    return pl.pallas_call(
        functools.partial(_peer_dense_kernel, ipc=ipc, tm=tm, out_chunk=_pick(D, (512, 256, 128))),
        grid=(T // tm, NE // ec),
        in_specs=[
            pl.BlockSpec((tm, D), lambda i, c: (i, 0), pipeline_mode=once),
            pl.BlockSpec((ec, D), lambda i, c: (c, 0)),
            pl.BlockSpec((ec, D), lambda i, c: (c, 0)),
            sel_spec, sel_spec, sel_spec, sel_spec,
        ],
        out_specs=pl.BlockSpec((tm, D), lambda i, c: (i, 0), pipeline_mode=once),
        out_shape=jax.ShapeDtypeStruct((T, D), F32),
        scratch_shapes=[pltpu.VMEM((ec, tm), F32), pltpu.VMEM((ec, tm), F32), pltpu.VMEM((D, tm), BF16)],
        compiler_params=_params("parallel", "arbitrary"),
        name="peer_dense",
    )(h2, u, v, cnt, f1, rank2, e2)


def _ple_final_kernel(x1_ref, peer_ref, gple_ref, wg_ref, p_ref, wp_ref, gfin_ref, o_ref, acc_scr, ss_scr,
                      *, tk, nk, tn):
    j = pl.program_id(1)
    D = acc_scr.shape[1]
    x2 = x1_ref[...] + peer_ref[...]
    lhs = (x2 * gple_ref[...]).astype(BF16)
    ssq = jnp.sum(x2 * x2, axis=-1, keepdims=True)

    @pl.when(j == 0)
    def _():
        acc_scr[...] = jnp.zeros_like(acc_scr)
        ss_scr[...] = jnp.zeros_like(ss_scr)

    ss_scr[...] += ssq
    for n0 in range(0, D, tn):
        acc_scr[:, n0:n0 + tn] += jnp.dot(lhs, wg_ref[:, n0:n0 + tn], preferred_element_type=F32)
    for jj in range(nk):
        @pl.when(j == jj)
        def _():
            o_ref[:, jj * tk:(jj + 1) * tk] = x2

    @pl.when(j == nk - 1)
    def _():
        inv = lax.rsqrt(ss_scr[...] / float(D) + RMS_EPS)
        pb = p_ref[...].astype(BF16)
        ss3 = jnp.zeros_like(inv)
        for n0 in range(0, D, tn):
            cols = slice(n0, n0 + tn)
            gate = jax.nn.sigmoid(acc_scr[:, cols] * inv)
            proj = jnp.dot(pb, wp_ref[:, cols], preferred_element_type=F32)
            x3 = o_ref[:, cols] + gate * proj
            o_ref[:, cols] = x3
            ss3 = ss3 + jnp.sum(x3 * x3, axis=-1, keepdims=True)
        inv3 = lax.rsqrt(ss3 / float(D) + RMS_EPS)
        for n0 in range(0, D, tn):
            cols = slice(n0, n0 + tn)
            o_ref[:, cols] = o_ref[:, cols] * inv3 * gfin_ref[:, cols]


def _ple_final(x1, peer, g_ple, w_gate, p, w_proj, g_final):
    T, D = x1.shape
    tm = _pick(T, (512, 256, 128))
    tk = _pick(D, (1024, 512, 256, 128))
    nk = D // tk
    once = pl.Buffered(1)
    return pl.pallas_call(
        functools.partial(_ple_final_kernel, tk=tk, nk=nk, tn=tk),
        grid=(T // tm, nk),
        in_specs=[
            pl.BlockSpec((tm, tk), lambda i, j: (i, j)),
            pl.BlockSpec((tm, tk), lambda i, j: (i, j)),
            pl.BlockSpec((1, tk), lambda i, j: (0, j)),
            pl.BlockSpec((tk, D), lambda i, j: (j, 0)),
            pl.BlockSpec((tm, PLE_DIM), lambda i, j: (i, 0)),
            pl.BlockSpec((PLE_DIM, D), lambda i, j: (0, 0), pipeline_mode=once),
            pl.BlockSpec((1, D), lambda i, j: (0, 0)),
        ],
        out_specs=pl.BlockSpec((tm, D), lambda i, j: (i, 0), pipeline_mode=once),
        out_shape=jax.ShapeDtypeStruct((T, D), F32),
        scratch_shapes=[pltpu.VMEM((tm, D), F32), pltpu.VMEM((tm, 1), F32)],
        compiler_params=_params("parallel", "arbitrary"),
        name="ple_final",
    )(x1, peer, g_ple.reshape(1, D), w_gate, p, w_proj, g_final.reshape(1, D))


def _alibi_slopes():
    return jnp.asarray(2.0 ** (-8.0 * np.arange(1, N_HEADS + 1) / N_HEADS), dtype=F32)


def _overlap_t(T):
    ncp = T // CMP_STRIDE
    n_cmp = (T - CMP_BLOCK) // CMP_STRIDE + 1
    n_sel = T // SLC_BLOCK
    c_start = np.arange(ncp) * CMP_STRIDE
    s_start = np.arange(n_sel) * SLC_BLOCK
    ov = (c_start[None, :] < s_start[:, None] + SLC_BLOCK) & (c_start[None, :] + CMP_BLOCK > s_start[:, None])
    ov = ov & (np.arange(ncp)[None, :] < n_cmp)
    return jnp.asarray(ov.astype(np.float32), dtype=BF16)


def _expand_blocks(T, tk):
    n_sel = T // SLC_BLOCK
    key_blk = (np.arange(T) // SLC_BLOCK).reshape(T // tk, 1, tk)
    e = key_blk == np.arange(n_sel).reshape(1, n_sel, 1)
    return jnp.asarray(e.astype(np.float32), dtype=BF16)


def _tile_lists(used, n_tiles):
    G, nq, _, ns = used.shape
    flags = used.reshape(G, nq, n_tiles, ns // n_tiles).max(axis=-1) > 0.0
    order = jnp.argsort(jnp.where(flags, 0, 1), axis=-1, stable=True)
    return order.astype(jnp.int32).reshape(-1), flags.sum(axis=-1).astype(jnp.int32).reshape(-1)


def _layer(x, p, norm_mix_g, w_in, pool_w, pool_scale, cmp_pos_k, cmp_pos_v, cmp_w_k, cmp_w_v,
           w_up_pool, w_up_nsa, w_out, norm_ffn_g, peer_w_q, peer_keys1, peer_keys2, peer_u, peer_v,
           norm_ple_g, ple_w_gate, ple_w_proj, norm_final_g):
    T, D = x.shape
    G = N_KV_GROUPS
    q_off = POOL_WIDTH
    kv_off = q_off + Q_WIDTH
    gates_off = kv_off + 6 * KV_WIDTH
    gp_src = gates_off + 3 * N_HEADS

    w_nk = jnp.swapaxes(w_in, 0, 1)
    w_head = w_nk[:gates_off].astype(BF16)
    w_merge_gates = w_nk[gp_src:].astype(BF16)
    w_g = w_nk[gates_off:gp_src].reshape(G, 3 * HEADS_PER_GROUP, D)
    w_g = jnp.pad(w_g, ((0, 0), (0, GATE_LANES - 3 * HEADS_PER_GROUP), (0, 0))).reshape(G * GATE_LANES, D)
    w_g = w_g.astype(BF16)

    h = _rmsnorm(x, norm_mix_g)
    z = _mm(h, w_head, BF16, b_is_nk=True, name="in_proj")
    zg = _mm(h, w_merge_gates, BF16, b_is_nk=True, name="merge_gate_proj")
    gates = _mm(h, w_g, F32, b_is_nk=True, name="gate_proj")

    ncp = T // CMP_STRIDE
    n_cmp = (T - CMP_BLOCK) // CMP_STRIDE + 1

    def blocks16(col):
        a = z[:, col:col + KV_WIDTH].reshape(ncp, CMP_STRIDE, G, HEAD_DIM)
        return a.transpose(2, 0, 1, 3).reshape(G, ncp, CMP_STRIDE * HEAD_DIM)

    kc = _compress(blocks16(kv_off), cmp_w_k.astype(BF16), cmp_pos_k.reshape(1, -1).astype(BF16), n_cmp, False)
    vct = _compress(blocks16(kv_off + KV_WIDTH), cmp_w_v.astype(BF16), cmp_pos_v.reshape(1, -1).astype(BF16),
                    n_cmp, True)
    slopes = _alibi_slopes()
    o_cmp, sel, used = _cmp_select(slopes, z, q_off, kc, vct, _overlap_t(T), T)
    tk = _pick(T, (512, 256))
    tile_list, tile_count = _tile_lists(used, T // tk)
    nsa_out = _nsa_attend(slopes, tile_list, tile_count, z, q_off, kv_off, o_cmp, sel, _expand_blocks(T, tk), gates, T)

    merged = _pool_merge(z, zg, pool_w.astype(BF16), pool_scale, w_up_pool.astype(BF16),
                         nsa_out, w_up_nsa.astype(BF16), D)
    x1 = _mm(merged, w_out.astype(BF16), F32, res=x, name="out_proj")

    h2 = _rmsnorm(x1, norm_ffn_g)
    qp = _mm(h2, peer_w_q.astype(BF16), F32, name="peer_q")
    cnt, f1, rank2, e2 = _peer_select(qp, peer_keys1.astype(BF16), peer_keys2.astype(BF16))
    peer = _peer_dense(h2, peer_u.astype(BF16), peer_v.astype(BF16), cnt, f1, rank2, e2)

    return _ple_final(x1, peer, norm_ple_g, ple_w_gate.astype(BF16), p, ple_w_proj.astype(BF16), norm_final_g)


def kernel(x, p, norm_mix_g, w_in, pool_w, pool_scale, cmp_pos_k, cmp_pos_v, cmp_w_k, cmp_w_v, w_up_pool, w_up_nsa, w_out, norm_ffn_g, peer_w_q, peer_keys1, peer_keys2, peer_u, peer_v, norm_ple_g, ple_w_gate, ple_w_proj, norm_final_g):
    B, T, D = x.shape
    depth = w_in.shape[0]
    assert depth == 1, "the final rmsnorm is fused into the layer's last kernel"
    outs = []
    for b in range(B):
        outs.append(_layer(
            x[b], p[0, b], norm_mix_g[0], w_in[0], pool_w[0], pool_scale[0], cmp_pos_k[0], cmp_pos_v[0],
            cmp_w_k[0], cmp_w_v[0], w_up_pool[0], w_up_nsa[0], w_out[0], norm_ffn_g[0], peer_w_q[0],
            peer_keys1[0], peer_keys2[0], peer_u[0], peer_v[0], norm_ple_g[0], ple_w_gate[0], ple_w_proj[0],
            norm_final_g))
    return outs[0][None] if B == 1 else jnp.stack(outs, axis=0)
```

```python
import functools

import numpy as np
import jax
import jax.numpy as jnp
from jax import lax
from jax.experimental import pallas as pl
from jax.experimental.pallas import tpu as pltpu

F32 = jnp.float32
BF16 = jnp.bfloat16

POOL_GROUPS = 4
POOL_GROUP_WIDTH = 512
POOL_WIDTH = POOL_GROUPS * POOL_GROUP_WIDTH
POOL_WINDOWS = (2, 4, 8, 16)
POOL_HALO = 16
N_HEADS = 16
N_KV_GROUPS = 4
HEADS_PER_GROUP = N_HEADS // N_KV_GROUPS
HEAD_DIM = 128
Q_WIDTH = N_HEADS * HEAD_DIM
KV_WIDTH = N_KV_GROUPS * HEAD_DIM
GROUP_Q_WIDTH = HEADS_PER_GROUP * HEAD_DIM
CMP_BLOCK = 32
CMP_STRIDE = 16
SLC_BLOCK = 64
SLC_TOPK = 16
WINDOW = 512
NSA_WINDOW_TILE = 256
FORCE_SCORE = 1e4
PEER_HEADS = 8
PEER_NKEYS = 128
PEER_QDIM = 256
PEER_HALF = PEER_QDIM // 2
PEER_TOPK = 16
PEER_CHUNK_KEYS = 4
PLE_DIM = 256
RMS_EPS = 1e-6
NEG = -1e30
LOG2E = 1.4426950408889634
GATE_LANES = 128

LANES = 128
V7X_VMEM_BYTES = 64 * 1024 * 1024
VMEM_LIMIT = V7X_VMEM_BYTES * 7 // 8

_NT = (((1,), (1,)), ((), ()))


def _params(*sem):
    return pltpu.CompilerParams(dimension_semantics=sem, vmem_limit_bytes=VMEM_LIMIT)


def _pick(n, cands):
    for c in cands:
        if n % c == 0:
            return c
    raise ValueError(f"no tile in {cands} divides {n}")


def _rmsnorm_kernel(x_ref, g_ref, o_ref):
    x = x_ref[...]
    ms = jnp.mean(x * x, axis=-1, keepdims=True)
    o_ref[...] = (x * lax.rsqrt(ms + RMS_EPS) * g_ref[...]).astype(o_ref.dtype)


def _rmsnorm(x, g, out_dtype=BF16):
    T, D = x.shape
    tm = _pick(T, (256, 128, 64, 8))
    return pl.pallas_call(
        _rmsnorm_kernel,
        grid=(T // tm,),
        in_specs=[pl.BlockSpec((tm, D), lambda i: (i, 0)), pl.BlockSpec((1, D), lambda i: (0, 0))],
        out_specs=pl.BlockSpec((tm, D), lambda i: (i, 0)),
        out_shape=jax.ShapeDtypeStruct((T, D), out_dtype),
        compiler_params=_params("parallel"),
        name="rmsnorm",
    )(x, g.reshape(1, D))


def _mm_kernel(a_ref, b_ref, *rest, has_res, b_is_nk):
    o_ref = rest[-1]
    if b_is_nk:
        acc = lax.dot_general(a_ref[...], b_ref[...], _NT, preferred_element_type=F32)
    else:
        acc = jnp.dot(a_ref[...], b_ref[...], preferred_element_type=F32)
    if has_res:
        acc = acc + rest[0][...]
    o_ref[...] = acc.astype(o_ref.dtype)


def _mm(a, b, out_dtype, res=None, b_is_nk=False, name="mm"):
    M, K = a.shape
    N = b.shape[0] if b_is_nk else b.shape[1]
    tm = _pick(M, (1024, 512, 256, 128))
    tn = _pick(N, (1024, 512, 256, 128))
    b_spec = pl.BlockSpec((tn, K), lambda j, i: (j, 0)) if b_is_nk else pl.BlockSpec((K, tn), lambda j, i: (0, j))
    in_specs = [pl.BlockSpec((tm, K), lambda j, i: (i, 0)), b_spec]
    args = [a, b]
    if res is not None:
        in_specs.append(pl.BlockSpec((tm, tn), lambda j, i: (i, j)))
        args.append(res)
    return pl.pallas_call(
        functools.partial(_mm_kernel, has_res=res is not None, b_is_nk=b_is_nk),
        grid=(N // tn, M // tm),
        in_specs=in_specs,
        out_specs=pl.BlockSpec((tm, tn), lambda j, i: (i, j)),
        out_shape=jax.ShapeDtypeStruct((M, N), out_dtype),
        compiler_params=_params("parallel", "parallel"),
        name=name,
    )(*args)


def _pool_merge_kernel(u_ref, halo_ref, pw_ref, ps_ref, wp_ref, nsa_ref, wn_ref, gp_ref, gn_ref,
                       o_ref, pool_scr, *, tm):
    i = pl.program_id(0)
    j = pl.program_id(1)

    @pl.when(j == 0)
    def _():
        u = u_ref[...].astype(F32)
        halo = halo_ref[...].astype(F32)
        halo = jnp.where(i == 0, 0.0, halo)
        ext = jnp.concatenate([halo, u], axis=0)
        t = (i * tm + lax.broadcasted_iota(jnp.int32, (tm, 1), 0)).astype(F32)
        for g, w in enumerate(POOL_WINDOWS):
            c0, c1 = g * POOL_GROUP_WIDTH, (g + 1) * POOL_GROUP_WIDTH
            s = ext[:, c0:c1]
            k = 1
            while k < w:
                s = s[:-k] + s[k:]
                k *= 2
            wsum = s[POOL_HALO + 1 - w:]
            cnt = jnp.minimum(t + 1.0, float(w))
            pooled = wsum / cnt - u[:, c0:c1]
            mixed = jnp.dot(pooled.astype(BF16), pw_ref[g], preferred_element_type=F32)
            pool_scr[:, c0:c1] = (mixed * ps_ref[:, c0:c1]).astype(BF16)

    up_pool = jnp.dot(pool_scr[...], wp_ref[...], preferred_element_type=F32)
    up_nsa = jnp.dot(nsa_ref[...], wn_ref[...], preferred_element_type=F32)
    merged = jax.nn.sigmoid(gp_ref[...].astype(F32)) * up_pool + jax.nn.sigmoid(gn_ref[...].astype(F32)) * up_nsa
    o_ref[...] = merged.astype(o_ref.dtype)


def _pool_merge(z, zg, pool_w, pool_scale, w_up_pool, nsa_out, w_up_nsa, D):
    T = z.shape[0]
    tm = _pick(T, (512, 256, 128))
    tn = _pick(D, (1024, 512, 256, 128))
    assert tm % POOL_HALO == 0
    hb = tm // POOL_HALO
    return pl.pallas_call(
        functools.partial(_pool_merge_kernel, tm=tm),
        grid=(T // tm, D // tn),
        in_specs=[
            pl.BlockSpec((tm, POOL_WIDTH), lambda i, j: (i, 0)),
            pl.BlockSpec((POOL_HALO, POOL_WIDTH), lambda i, j: (jnp.maximum(i * hb - 1, 0), 0)),
            pl.BlockSpec((POOL_GROUPS, POOL_GROUP_WIDTH, POOL_GROUP_WIDTH), lambda i, j: (0, 0, 0)),
            pl.BlockSpec((1, POOL_WIDTH), lambda i, j: (0, 0)),
            pl.BlockSpec((POOL_WIDTH, tn), lambda i, j: (0, j)),
            pl.BlockSpec((tm, Q_WIDTH), lambda i, j: (i, 0)),
            pl.BlockSpec((Q_WIDTH, tn), lambda i, j: (0, j)),
            pl.BlockSpec((tm, tn), lambda i, j: (i, j)),
            pl.BlockSpec((tm, tn), lambda i, j: (i, D // tn + j)),
        ],
        out_specs=pl.BlockSpec((tm, tn), lambda i, j: (i, j)),
        out_shape=jax.ShapeDtypeStruct((T, D), BF16),
        scratch_shapes=[pltpu.VMEM((tm, POOL_WIDTH), BF16)],
        compiler_params=_params("parallel", "arbitrary"),
        name="pool_merge",
    )(z, z, pool_w, pool_scale.reshape(1, POOL_WIDTH), w_up_pool, nsa_out, w_up_nsa, zg, zg)


def _compress_kernel(r_ref, w_ref, pos_ref, o_ref, *, n_cmp, transpose_out):
    half = CMP_STRIDE * HEAD_DIM
    r = r_ref[0]
    w = w_ref[...]
    top = jnp.dot(r, w[:half], preferred_element_type=F32)
    bot = jnp.dot(r, w[half:], preferred_element_type=F32)
    bias = jnp.dot(pos_ref[...], w, preferred_element_type=F32)
    ncp = top.shape[0]
    bot_next = jnp.concatenate([bot[1:], jnp.zeros((1, HEAD_DIM), F32)], axis=0)
    out = top + bot_next + bias
    row = lax.broadcasted_iota(jnp.int32, (ncp, 1), 0)
    out = jnp.where(row < n_cmp, out, 0.0)
    if transpose_out:
        o_ref[0] = out.T.astype(o_ref.dtype)
    else:
        o_ref[0] = out.astype(o_ref.dtype)


def _compress(r, w, pos, n_cmp, transpose_out):
    G, ncp, width = r.shape
    oshape = (G, HEAD_DIM, ncp) if transpose_out else (G, ncp, HEAD_DIM)
    return pl.pallas_call(
        functools.partial(_compress_kernel, n_cmp=n_cmp, transpose_out=transpose_out),
        grid=(G,),
        in_specs=[
            pl.BlockSpec((1, ncp, width), lambda g: (g, 0, 0)),
            pl.BlockSpec((CMP_BLOCK * HEAD_DIM, HEAD_DIM), lambda g: (0, 0)),
            pl.BlockSpec((1, CMP_BLOCK * HEAD_DIM), lambda g: (0, 0)),
        ],
        out_specs=pl.BlockSpec((1,) + oshape[1:], lambda g: (g, 0, 0)),
        out_shape=jax.ShapeDtypeStruct(oshape, BF16),
        compiler_params=_params("parallel"),
        name="nsa_compress",
    )(r, w, pos)


def _extract_topk(a, k, want_vals=False, exact=True):
    n = a.shape[0]
    idx = lax.broadcasted_iota(jnp.int32, a.shape, 0).astype(F32)
    rank = jnp.full(a.shape, float(k), F32)
    vals = []
    for r in range(k):
        m = jnp.max(a, axis=0, keepdims=True)
        hit = a == m
        if exact:
            first = jnp.min(jnp.where(hit, idx, float(n)), axis=0, keepdims=True)
            hit = idx == first
        rank = jnp.where(hit, float(r), rank)
        a = jnp.where(hit, -jnp.inf, a)
        vals.append(m)
    return (rank, vals) if want_vals else rank


def _cmp_select_kernel(slope_ref, q_ref, kc_ref, vct_ref, ovt_ref, o_ref, sel_ref, used_ref, *, tq):
    g = pl.program_id(0)
    i = pl.program_id(1)
    ncp = kc_ref.shape[1]
    ns = ovt_ref.shape[0]
    t = i * tq + lax.broadcasted_iota(jnp.int32, (1, tq), 1)
    n = lax.broadcasted_iota(jnp.int32, (ncp, 1), 0)
    dist = (t - (n * CMP_STRIDE + CMP_BLOCK - 1)).astype(F32)
    valid = dist >= 0.0
    kc = kc_ref[0]
    vct = vct_ref[0]
    q = q_ref[...]
    psum = jnp.zeros((ncp, tq), F32)
    for h in range(HEADS_PER_GROUP):
        qh = (q[:, h * HEAD_DIM:(h + 1) * HEAD_DIM].astype(F32) * (HEAD_DIM ** -0.5)).astype(BF16)
        s = lax.dot_general(kc, qh, _NT, preferred_element_type=F32)
        s = s - slope_ref[g * HEADS_PER_GROUP + h] * dist
        s = jnp.where(valid, s, NEG)
        m = jnp.max(s, axis=0, keepdims=True)
        e = jnp.exp(s - m)
        l = jnp.sum(e, axis=0, keepdims=True)
        p = jnp.where(valid, e / l, 0.0)
        ot = jnp.dot(vct, p.astype(BF16), preferred_element_type=F32)
        o_ref[:, h * HEAD_DIM:(h + 1) * HEAD_DIM] = ot.T.astype(o_ref.dtype)
        psum = psum + p
    p_hi = psum.astype(BF16)
    p_lo = (psum - p_hi.astype(F32)).astype(BF16)
    ovt = ovt_ref[...]
    imp = jnp.dot(ovt, p_hi, preferred_element_type=F32) + jnp.dot(ovt, p_lo, preferred_element_type=F32)
    kk = lax.broadcasted_iota(jnp.int32, (ns, 1), 0)
    cur = jnp.right_shift(t, SLC_BLOCK.bit_length() - 1)
    forced = (kk == 0) | (kk == cur) | (kk == cur - 1)
    score = jnp.where(forced, FORCE_SCORE, jnp.where(kk <= cur, imp, -1.0))
    k_sel = min(SLC_TOPK, ns)
    rank = _extract_topk(score, k_sel)
    sel = jnp.where((rank < float(k_sel)) & (kk <= cur), 1.0, 0.0)
    sel_ref[0] = sel.astype(sel_ref.dtype)
    used_ref[0, 0] = jnp.max(sel.T, axis=0, keepdims=True)


def _cmp_select(slopes, z, q_off, kc, vct, ovt, T):
    tq = _pick(T, (256, 128))
    ncp = kc.shape[1]
    ns = ovt.shape[0]
    assert q_off % GROUP_Q_WIDTH == 0
    qb = q_off // GROUP_Q_WIDTH
    return pl.pallas_call(
        functools.partial(_cmp_select_kernel, tq=tq),
        grid=(N_KV_GROUPS, T // tq),
        in_specs=[
            pl.BlockSpec(memory_space=pltpu.SMEM),
            pl.BlockSpec((tq, GROUP_Q_WIDTH), lambda g, i: (i, qb + g)),
            pl.BlockSpec((1, ncp, HEAD_DIM), lambda g, i: (g, 0, 0)),
            pl.BlockSpec((1, HEAD_DIM, ncp), lambda g, i: (g, 0, 0)),
            pl.BlockSpec((ns, ncp), lambda g, i: (0, 0)),
        ],
        out_specs=[
            pl.BlockSpec((tq, GROUP_Q_WIDTH), lambda g, i: (i, g)),
            pl.BlockSpec((1, ns, tq), lambda g, i: (g, 0, i)),
            pl.BlockSpec((1, 1, 1, ns), lambda g, i: (g, i, 0, 0)),
        ],
        out_shape=[
            jax.ShapeDtypeStruct((T, Q_WIDTH), BF16),
            jax.ShapeDtypeStruct((N_KV_GROUPS, ns, T), BF16),
            jax.ShapeDtypeStruct((N_KV_GROUPS, T // tq, 1, ns), F32),
        ],
        compiler_params=_params("parallel", "parallel"),
        name="nsa_cmp_select",
    )(slopes, z, kc, vct, ovt)


def _nsa_attend_kernel(slope_ref, parts_ref, tile_ref, count_ref, q_ref, ks_ref, vst_ref, kw_ref, vwt_ref, ocmp_ref,
                       sel_ref, exp_ref, kpos_ref, gate_ref, o_ref, q_scr, acc_scr, *, tq, tk, wt, n_wt):
    g = pl.program_id(0)
    i = pl.program_id(1)
    t0 = i * tq
    hg = HEADS_PER_GROUP
    dk = HEAD_DIM
    q = q_ref[...]
    lane = lax.broadcasted_iota(jnp.int32, (tq, dk), 1)
    for h in range(hg):
        qh = q[:, h * dk:(h + 1) * dk].astype(F32) * (dk ** -0.5 * LOG2E)
        q_scr[h * tq:(h + 1) * tq, :dk] = qh.astype(BF16)
        parts = [parts_ref[(g * hg + h) * 3 + r] for r in range(3)]
        extra = jnp.zeros((tq, dk), F32)
        for r in range(6):
            extra = jnp.where(lane == r, parts[r % 3], extra)
        q_scr[h * tq:(h + 1) * tq, dk:] = extra.astype(BF16)
    q_all = q_scr[...]
    slopes = [slope_ref[g * hg + h] * LOG2E for h in range(hg)]
    tpos = t0 + lax.broadcasted_iota(jnp.int32, (1, tq), 1)

    selt = sel_ref[0]
    krel = lax.broadcasted_iota(jnp.int32, (tk, tq), 0)
    acc_scr[...] = jnp.zeros_like(acc_scr)
    step = g * pl.num_programs(1) + i
    list_base = step * exp_ref.shape[0]

    def kv_step(n, carry):
        ms, ls = carry
        j = tile_ref[list_base + n]
        s0 = pl.multiple_of(j * tk, tk)
        kt = ks_ref[pl.ds(s0, tk), :]
        vtt = vst_ref[j]
        chosen = jnp.dot(exp_ref[j], selt, preferred_element_type=F32)
        allow = (chosen > 0.5) & ((krel + s0) <= tpos)
        k_aug = jnp.concatenate([kt, kpos_ref[:tk, :]], axis=1)
        s_all = lax.dot_general(k_aug, q_all, _NT, preferred_element_type=F32)
        shift = (s0 - t0).astype(F32)
        new_ms, new_ls = [], []
        for h in range(hg):
            lanes = slice(h * tq, (h + 1) * tq)
            cj = slopes[h] * shift
            s = jnp.where(allow, s_all[:, lanes], NEG)
            m_new = jnp.maximum(ms[h], jnp.max(s, axis=0, keepdims=True) + cj)
            alpha = jnp.exp2(ms[h] - m_new)
            p = jnp.exp2(s - (m_new - cj))
            new_ls.append(alpha * ls[h] + jnp.sum(p, axis=0, keepdims=True))
            new_ms.append(m_new)
            rows = slice(h * dk, (h + 1) * dk)
            acc_scr[rows, :] = alpha * acc_scr[rows, :] + jnp.dot(vtt, p.astype(BF16), preferred_element_type=F32)
        return tuple(new_ms), tuple(new_ls)

    init = (tuple(jnp.full((1, tq), NEG, F32) for _ in range(hg)),
            tuple(jnp.zeros((1, tq), F32) for _ in range(hg)))
    ms, ls = lax.fori_loop(0, count_ref[step], kv_step, init)

    w0 = pl.multiple_of(jnp.maximum(t0 - WINDOW, 0), wt)
    span = n_wt * wt
    kw = kw_ref[pl.ds(w0, span), :]
    wi = w0 // wt
    vwt = jnp.concatenate([vwt_ref[wi + r] for r in range(n_wt)], axis=1)
    kpos_w = w0 + lax.broadcasted_iota(jnp.int32, (span, tq), 0)
    dw = tpos - kpos_w
    allow_w = (dw >= 0) & (dw < WINDOW)
    kw_aug = jnp.concatenate([kw, kpos_ref[:span, :]], axis=1)
    sw_all = lax.dot_general(kw_aug, q_all, _NT, preferred_element_type=F32)
    ows, lws = [], []
    for h in range(hg):
        s = jnp.where(allow_w, sw_all[:, h * tq:(h + 1) * tq], NEG)
        m = jnp.max(s, axis=0, keepdims=True)
        p = jnp.exp2(s - m)
        lws.append(jnp.sum(p, axis=0, keepdims=True))
        ows.append(jnp.dot(vwt, p.astype(BF16), preferred_element_type=F32))

    gt = jax.nn.sigmoid(gate_ref[...])
    gtt = gt.T
    ocmp = ocmp_ref[...].astype(F32)
    for h in range(hg):
        o_st = acc_scr[h * dk:(h + 1) * dk, :] / ls[h]
        o_wt = ows[h] / lws[h]
        o_t = gtt[3 * h + 1:3 * h + 2, :] * o_st + gtt[3 * h + 2:3 * h + 3, :] * o_wt
        o = gt[:, 3 * h:3 * h + 1] * ocmp[:, h * dk:(h + 1) * dk] + o_t.T
        o_ref[:, h * dk:(h + 1) * dk] = o.astype(o_ref.dtype)


def _nsa_attend(slopes, slope_parts, tile_list, tile_count, z, q_off, kv_off, vst, vwt, o_cmp, sel, expand, gates, T):
    tq = _pick(T, (256, 128))
    n_exp, tk, ns = expand.shape
    wt = vwt.shape[-1]
    n_wt = (WINDOW + tq) // wt
    kpos = _key_offset_columns(max(tk, n_wt * wt))
    assert T >= n_wt * wt and (WINDOW + tq) % wt == 0 and WINDOW % wt == 0 and tq % wt == 0
    assert q_off % GROUP_Q_WIDTH == 0 and kv_off % HEAD_DIM == 0
    qb = q_off // GROUP_Q_WIDTH
    kvb = kv_off // HEAD_DIM
    G = N_KV_GROUPS

    def k_spec(branch):
        return pl.BlockSpec((T, HEAD_DIM), lambda g, i: (0, kvb + branch * G + g))

    def vt_spec(a):
        return pl.BlockSpec((None,) + a.shape[1:], lambda g, i: (g, 0, 0, 0))

    return pl.pallas_call(
        functools.partial(_nsa_attend_kernel, tq=tq, tk=tk, wt=wt, n_wt=n_wt),
        grid=(G, T // tq),
        in_specs=[
            pl.BlockSpec(memory_space=pltpu.SMEM),
            pl.BlockSpec(memory_space=pltpu.SMEM),
            pl.BlockSpec(memory_space=pltpu.SMEM),
            pl.BlockSpec(memory_space=pltpu.SMEM),
            pl.BlockSpec((tq, GROUP_Q_WIDTH), lambda g, i: (i, qb + g)),
            k_spec(2), vt_spec(vst), k_spec(4), vt_spec(vwt),
            pl.BlockSpec((tq, GROUP_Q_WIDTH), lambda g, i: (i, g)),
            pl.BlockSpec((1, ns, tq), lambda g, i: (g, 0, i)),
            pl.BlockSpec((n_exp, tk, ns), lambda g, i: (0, 0, 0)),
            pl.BlockSpec(kpos.shape, lambda g, i: (0, 0)),
            pl.BlockSpec((tq, GATE_LANES), lambda g, i: (i, g)),
        ],
        out_specs=pl.BlockSpec((tq, GROUP_Q_WIDTH), lambda g, i: (i, g)),
        out_shape=jax.ShapeDtypeStruct((T, Q_WIDTH), BF16),
        scratch_shapes=[
            pltpu.VMEM((HEADS_PER_GROUP * tq, 2 * HEAD_DIM), BF16),
            pltpu.VMEM((HEADS_PER_GROUP * HEAD_DIM, tq), F32),
        ],
        compiler_params=_params("parallel", "arbitrary"),
        name="nsa_attend",
    )(slopes, slope_parts, tile_list, tile_count, z, z, vst, z, vwt, o_cmp, sel, expand, kpos, gates)


_CAND_PIECES = [(0, 16), (1, 8), (2, 8), (3, 8), (4, 8), (5, 8), (6, 8), (7, 8)]


def _peer_select_head(q, k1, k2, exact):
    K = PEER_TOPK
    s1 = lax.dot_general(k1, q[:, :PEER_HALF], _NT, preferred_element_type=F32)
    s2 = lax.dot_general(k2, q[:, PEER_HALF:], _NT, preferred_element_type=F32)
    rank1, v1 = _extract_topk(s1, K, want_vals=True, exact=exact)
    rank2, v2 = _extract_topk(s2, K, want_vals=True, exact=exact)
    v1s = jnp.concatenate(v1, axis=0)
    v2s = jnp.concatenate(v2, axis=0)
    pieces = [v1[r1] + v2s[:L] for r1, L in _CAND_PIECES]
    pieces.append(v1s[8:] + v2[0])
    cand = jnp.concatenate(pieces, axis=0)
    crank = _extract_topk(cand, K, exact=exact)
    chosen = crank < float(K)
    top = v1[0] + v2[0]
    z = jnp.sum(jnp.where(chosen, jnp.exp(cand - top), 0.0), axis=0, keepdims=True)
    chosen_f = jnp.where(chosen, 1.0, 0.0)
    counts = []
    off = 0
    for _, L in _CAND_PIECES:
        counts.append(jnp.sum(chosen_f[off:off + L], axis=0, keepdims=True))
        off += L
    tail = chosen_f[off:off + 8]
    for r in range(8):
        counts.append(tail[r:r + 1])
    cnt = jnp.zeros_like(s1)
    for r1 in range(K):
        cnt = jnp.where(rank1 == float(r1), counts[r1], cnt)

    def n_ranked(rank):
        return jnp.sum(jnp.where(rank < float(K), 1.0, 0.0), axis=0, keepdims=True)

    tied = (jnp.abs(n_ranked(rank1) - float(K)) + jnp.abs(n_ranked(rank2) - float(K))
            + jnp.abs(n_ranked(crank) - float(K)))
    return (cnt, jnp.exp(s1 - v1[0]) / z, rank2, jnp.exp(s2 - v2[0])), tied


def _peer_select_kernel(q_ref, k1_ref, k2_ref, cnt_ref, f1_ref, rank2_ref, e2_ref, *, heads):
    k1 = k1_ref[...]
    k2 = k2_ref[...]
    out_refs = (cnt_ref, f1_ref, rank2_ref, e2_ref)

    def run(exact):
        tied = None
        for hh in range(heads):
            q = q_ref[:, hh * PEER_QDIM:(hh + 1) * PEER_QDIM].astype(BF16)
            outs, t = _peer_select_head(q, k1, k2, exact)
            for ref, val in zip(out_refs, outs):
                ref[hh, 0] = val
            tied = t if tied is None else tied + t
        return jnp.max(tied)

    any_tie = run(False)

    @pl.when(any_tie > 0.0)
    def _():
        run(True)


def _peer_select(qp, keys1, keys2):
    T = qp.shape[0]
    tp = _pick(T, (128,))
    heads = 2
    shp = jax.ShapeDtypeStruct((PEER_HEADS, T // tp, PEER_NKEYS, tp), F32)
    ospec = pl.BlockSpec((heads, 1, PEER_NKEYS, tp), lambda i, h: (h, i, 0, 0))
    return pl.pallas_call(
        functools.partial(_peer_select_kernel, heads=heads),
        grid=(T // tp, PEER_HEADS // heads),
        in_specs=[
            pl.BlockSpec((tp, heads * PEER_QDIM), lambda i, h: (i, h)),
            pl.BlockSpec((PEER_NKEYS, PEER_HALF), lambda i, h: (0, 0)),
            pl.BlockSpec((PEER_NKEYS, PEER_HALF), lambda i, h: (0, 0)),
        ],
        out_specs=[ospec, ospec, ospec, ospec],
        out_shape=[shp, shp, shp, shp],
        compiler_params=_params("parallel", "parallel"),
        name="peer_select",
    )(qp, keys1, keys2)


def _peer_dense_kernel(h_ref, u_ref, v_ref, cnt_ref, f1_ref, rank2_ref, e2_ref, o_ref,
                       w_scr, at_scr, ht_scr, *, ipc, tm, out_chunk):
    c = pl.program_id(1)
    nc = pl.num_programs(1)
    nk = PEER_NKEYS
    lt = LANES
    D = o_ref.shape[1]

    def build_gates(chunk, k, tb, tie):
        i1 = chunk * ipc + k
        lanes = slice(tb * lt, (tb + 1) * lt)
        w = jnp.zeros((nk, lt), F32)
        for h in range(PEER_HEADS):
            crow = cnt_ref[h, tb, pl.ds(i1, 1), :]
            if h == 0:
                crow = crow + (tie if tie.shape[1] == lt else tie[:, lanes])
            frow = f1_ref[h, tb, pl.ds(i1, 1), :]
            w = w + jnp.where(rank2_ref[h, tb] < crow, e2_ref[h, tb] * frow, 0.0)
        w_scr[k * nk:(k + 1) * nk, lanes] = w

    tiles = [(k, tb) for k in range(ipc) for tb in range(tm // lt)]
    early, late = tiles[:len(tiles) // 2], tiles[len(tiles) // 2:]
    zero_row = jnp.zeros((1, tm), F32)

    @pl.when(c == 0)
    def _():
        o_ref[...] = jnp.zeros_like(o_ref)
        for d0 in range(0, D, out_chunk):
            ht_scr[d0:d0 + out_chunk, :] = h_ref[:, d0:d0 + out_chunk].astype(F32).T.astype(BF16)
        for k, tb in early:
            build_gates(c, k, tb, zero_row)

    half = (ipc * nk) // 2
    ht = ht_scr[...]
    at_scr[:half, :] = jnp.dot(u_ref[:half, :], ht, preferred_element_type=F32)
    tie = at_scr[0:1, :] * 0.0
    at_scr[half:, :] = jnp.dot(u_ref[half:, :], ht, preferred_element_type=F32)
    for k, tb in late:
        build_gates(c, k, tb, tie)
    coef = (w_scr[...] * jax.nn.gelu(at_scr[...])).T.astype(BF16)
    nxt = jnp.minimum(c + 1, nc - 1)
    n_out = D // out_chunk
    for j in range(n_out):
        d0 = j * out_chunk
        o_ref[:, d0:d0 + out_chunk] += jnp.dot(coef, v_ref[:, d0:d0 + out_chunk], preferred_element_type=F32)
        tie2 = o_ref[0:1, d0:d0 + lt] * 0.0
        for k, tb in early[j::n_out]:
            build_gates(nxt, k, tb, tie2)


def _peer_dense(h2, u, v, cnt, f1, rank2, e2):
    T, D = h2.shape
    NE = u.shape[0]
    tm = _pick(T, (512, 256, 128))
    ipc = PEER_CHUNK_KEYS
    ec = ipc * PEER_NKEYS
    once = pl.Buffered(1)
    sel_spec = pl.BlockSpec((PEER_HEADS, tm // LANES, PEER_NKEYS, LANES), lambda i, c: (0, i, 0, 0),
                            pipeline_mode=once)
    return pl.pallas_call(
        functools.partial(_peer_dense_kernel, ipc=ipc, tm=tm, out_chunk=_pick(D, (512, 256, 128))),
        grid=(T // tm, NE // ec),
        in_specs=[
            pl.BlockSpec((tm, D), lambda i, c: (i, 0), pipeline_mode=once),
            pl.BlockSpec((ec, D), lambda i, c: (c, 0)),
            pl.BlockSpec((ec, D), lambda i, c: (c, 0)),
            sel_spec, sel_spec, sel_spec, sel_spec,
        ],
        out_specs=pl.BlockSpec((tm, D), lambda i, c: (i, 0), pipeline_mode=once),
        out_shape=jax.ShapeDtypeStruct((T, D), F32),
        scratch_shapes=[pltpu.VMEM((ec, tm), F32), pltpu.VMEM((ec, tm), F32), pltpu.VMEM((D, tm), BF16)],
        compiler_params=_params("parallel", "arbitrary"),
        name="peer_dense",
    )(h2, u, v, cnt, f1, rank2, e2)


def _ple_final_kernel(x1_ref, peer_ref, gple_ref, wg_ref, p_ref, wp_ref, gfin_ref, o_ref, acc_scr, ss_scr,
                      *, tk, nk, tn):
    j = pl.program_id(1)
    D = acc_scr.shape[1]
    x2 = x1_ref[...] + peer_ref[...]
    lhs = (x2 * gple_ref[...]).astype(BF16)
    ssq = jnp.sum(x2 * x2, axis=-1, keepdims=True)

    @pl.when(j == 0)
    def _():
        acc_scr[...] = jnp.zeros_like(acc_scr)
        ss_scr[...] = jnp.zeros_like(ss_scr)

    ss_scr[...] += ssq
    for n0 in range(0, D, tn):
        acc_scr[:, n0:n0 + tn] += jnp.dot(lhs, wg_ref[:, n0:n0 + tn], preferred_element_type=F32)
    for jj in range(nk):
        @pl.when(j == jj)
        def _():
            o_ref[:, jj * tk:(jj + 1) * tk] = x2

    @pl.when(j == nk - 1)
    def _():
        inv = lax.rsqrt(ss_scr[...] / float(D) + RMS_EPS)
        pb = p_ref[...].astype(BF16)
        ss3 = jnp.zeros_like(inv)
        for n0 in range(0, D, tn):
            cols = slice(n0, n0 + tn)
            gate = jax.nn.sigmoid(acc_scr[:, cols] * inv)
            proj = jnp.dot(pb, wp_ref[:, cols], preferred_element_type=F32)
            x3 = o_ref[:, cols] + gate * proj
            o_ref[:, cols] = x3
            ss3 = ss3 + jnp.sum(x3 * x3, axis=-1, keepdims=True)
        inv3 = lax.rsqrt(ss3 / float(D) + RMS_EPS)
        for n0 in range(0, D, tn):
            cols = slice(n0, n0 + tn)
            o_ref[:, cols] = o_ref[:, cols] * inv3 * gfin_ref[:, cols]


def _ple_final(x1, peer, g_ple, w_gate, p, w_proj, g_final):
    T, D = x1.shape
    tm = _pick(T, (512, 256, 128))
    tk = _pick(D, (1024, 512, 256, 128))
    nk = D // tk
    once = pl.Buffered(1)
    return pl.pallas_call(
        functools.partial(_ple_final_kernel, tk=tk, nk=nk, tn=tk),
        grid=(T // tm, nk),
        in_specs=[
            pl.BlockSpec((tm, tk), lambda i, j: (i, j)),
            pl.BlockSpec((tm, tk), lambda i, j: (i, j)),
            pl.BlockSpec((1, tk), lambda i, j: (0, j)),
            pl.BlockSpec((tk, D), lambda i, j: (j, 0)),
            pl.BlockSpec((tm, PLE_DIM), lambda i, j: (i, 0)),
            pl.BlockSpec((PLE_DIM, D), lambda i, j: (0, 0), pipeline_mode=once),
            pl.BlockSpec((1, D), lambda i, j: (0, 0)),
        ],
        out_specs=pl.BlockSpec((tm, D), lambda i, j: (i, 0), pipeline_mode=once),
        out_shape=jax.ShapeDtypeStruct((T, D), F32),
        scratch_shapes=[pltpu.VMEM((tm, D), F32), pltpu.VMEM((tm, 1), F32)],
        compiler_params=_params("parallel", "arbitrary"),
        name="ple_final",
    )(x1, peer, g_ple.reshape(1, D), w_gate, p, w_proj, g_final.reshape(1, D))


def _alibi_slopes():
    return jnp.asarray(2.0 ** (-8.0 * np.arange(1, N_HEADS + 1) / N_HEADS), dtype=F32)


def _alibi_slope_parts():
    s = (2.0 ** (-8.0 * np.arange(1, N_HEADS + 1) / N_HEADS) * LOG2E).astype(np.float32)
    parts = []
    for _ in range(3):
        hi = s.astype(BF16).astype(np.float32)
        parts.append(hi)
        s = (s - hi).astype(np.float32)
    return jnp.asarray(np.stack(parts, axis=1).reshape(-1), dtype=F32)


def _key_offset_columns(n):
    r = np.arange(n)
    cols = np.zeros((n, HEAD_DIM), np.float32)
    cols[:, 0:3] = (16 * (r // 16))[:, None]
    cols[:, 3:6] = (r % 16)[:, None]
    return jnp.asarray(cols, dtype=BF16)


def _overlap_t(T):
    ncp = T // CMP_STRIDE
    n_cmp = (T - CMP_BLOCK) // CMP_STRIDE + 1
    n_sel = T // SLC_BLOCK
    c_start = np.arange(ncp) * CMP_STRIDE
    s_start = np.arange(n_sel) * SLC_BLOCK
    ov = (c_start[None, :] < s_start[:, None] + SLC_BLOCK) & (c_start[None, :] + CMP_BLOCK > s_start[:, None])
    ov = ov & (np.arange(ncp)[None, :] < n_cmp)
    return jnp.asarray(ov.astype(np.float32), dtype=BF16)


def _expand_blocks(T, tk):
    n_sel = T // SLC_BLOCK
    key_blk = (np.arange(T) // SLC_BLOCK).reshape(T // tk, tk, 1)
    e = key_blk == np.arange(n_sel).reshape(1, 1, n_sel)
    return jnp.asarray(e.astype(np.float32), dtype=BF16)


def _tile_lists(used, n_tiles):
    G, nq, _, ns = used.shape
    flags = used.reshape(G, nq, n_tiles, ns // n_tiles).max(axis=-1) > 0.0
    order = jnp.argsort(jnp.where(flags, 0, 1), axis=-1, stable=True)
    return order.astype(jnp.int32).reshape(-1), flags.sum(axis=-1).astype(jnp.int32).reshape(-1)


def _layer(x, p, norm_mix_g, w_in, pool_w, pool_scale, cmp_pos_k, cmp_pos_v, cmp_w_k, cmp_w_v,
           w_up_pool, w_up_nsa, w_out, norm_ffn_g, peer_w_q, peer_keys1, peer_keys2, peer_u, peer_v,
           norm_ple_g, ple_w_gate, ple_w_proj, norm_final_g):
    T, D = x.shape
    G = N_KV_GROUPS
    q_off = POOL_WIDTH
    kv_off = q_off + Q_WIDTH
    gates_off = kv_off + 6 * KV_WIDTH
    gp_src = gates_off + 3 * N_HEADS

    w_nk = jnp.swapaxes(w_in, 0, 1)
    w_head = w_nk[:gates_off].astype(BF16)
    w_merge_gates = w_nk[gp_src:].astype(BF16)
    w_g = w_nk[gates_off:gp_src].reshape(G, 3 * HEADS_PER_GROUP, D)
    w_g = jnp.pad(w_g, ((0, 0), (0, GATE_LANES - 3 * HEADS_PER_GROUP), (0, 0))).reshape(G * GATE_LANES, D)
    w_g = w_g.astype(BF16)

    h = _rmsnorm(x, norm_mix_g)
    z = _mm(h, w_head, BF16, b_is_nk=True, name="in_proj")
    zg = _mm(h, w_merge_gates, BF16, b_is_nk=True, name="merge_gate_proj")
    gates = _mm(h, w_g, F32, b_is_nk=True, name="gate_proj")

    ncp = T // CMP_STRIDE
    n_cmp = (T - CMP_BLOCK) // CMP_STRIDE + 1

    def blocks16(col):
        a = z[:, col:col + KV_WIDTH].reshape(ncp, CMP_STRIDE, G, HEAD_DIM)
        return a.transpose(2, 0, 1, 3).reshape(G, ncp, CMP_STRIDE * HEAD_DIM)

    kc = _compress(blocks16(kv_off), cmp_w_k.astype(BF16), cmp_pos_k.reshape(1, -1).astype(BF16), n_cmp, False)
    vct = _compress(blocks16(kv_off + KV_WIDTH), cmp_w_v.astype(BF16), cmp_pos_v.reshape(1, -1).astype(BF16),
                    n_cmp, True)
    slopes = _alibi_slopes()
    o_cmp, sel, used = _cmp_select(slopes, z, q_off, kc, vct, _overlap_t(T), T)
    tk = _pick(T, (512, 256))
    tile_list, tile_count = _tile_lists(used, T // tk)

    def value_tiles(branch, tile):
        col = kv_off + branch * KV_WIDTH
        a = z[:, col:col + KV_WIDTH].reshape(T // tile, tile, G, HEAD_DIM)
        return a.transpose(2, 0, 3, 1)

    nsa_out = _nsa_attend(slopes, _alibi_slope_parts(), tile_list, tile_count, z, q_off, kv_off, value_tiles(3, tk),
                          value_tiles(5, NSA_WINDOW_TILE), o_cmp, sel, _expand_blocks(T, tk), gates, T)

    merged = _pool_merge(z, zg, pool_w.astype(BF16), pool_scale, w_up_pool.astype(BF16),
                         nsa_out, w_up_nsa.astype(BF16), D)
    x1 = _mm(merged, w_out.astype(BF16), F32, res=x, name="out_proj")

    h2 = _rmsnorm(x1, norm_ffn_g)
    qp = _mm(h2, peer_w_q.astype(BF16), F32, name="peer_q")
    cnt, f1, rank2, e2 = _peer_select(qp, peer_keys1.astype(BF16), peer_keys2.astype(BF16))
    peer = _peer_dense(h2, peer_u.astype(BF16), peer_v.astype(BF16), cnt, f1, rank2, e2)

    return _ple_final(x1, peer, norm_ple_g, ple_w_gate.astype(BF16), p, ple_w_proj.astype(BF16), norm_final_g)


def kernel(x, p, norm_mix_g, w_in, pool_w, pool_scale, cmp_pos_k, cmp_pos_v, cmp_w_k, cmp_w_v, w_up_pool, w_up_nsa, w_out, norm_ffn_g, peer_w_q, peer_keys1, peer_keys2, peer_u, peer_v, norm_ple_g, ple_w_gate, ple_w_proj, norm_final_g):
    B, T, D = x.shape
    depth = w_in.shape[0]
    assert depth == 1, "the final rmsnorm is fused into the layer's last kernel"
    outs = []
    for b in range(B):
        outs.append(_layer(
            x[b], p[0, b], norm_mix_g[0], w_in[0], pool_w[0], pool_scale[0], cmp_pos_k[0], cmp_pos_v[0],
            cmp_w_k[0], cmp_w_v[0], w_up_pool[0], w_up_nsa[0], w_out[0], norm_ffn_g[0], peer_w_q[0],
            peer_keys1[0], peer_keys2[0], peer_u[0], peer_v[0], norm_ple_g[0], ple_w_gate[0], ple_w_proj[0],
            norm_final_g))
    return outs[0][None] if B == 1 else jnp.stack(outs, axis=0)
```

```python
import functools

import numpy as np
import jax
import jax.numpy as jnp
from jax import lax
from jax.experimental import pallas as pl
from jax.experimental.pallas import tpu as pltpu

F32 = jnp.float32
BF16 = jnp.bfloat16

POOL_GROUPS = 4
POOL_GROUP_WIDTH = 512
POOL_WIDTH = POOL_GROUPS * POOL_GROUP_WIDTH
POOL_WINDOWS = (2, 4, 8, 16)
POOL_HALO = 16
N_HEADS = 16
N_KV_GROUPS = 4
HEADS_PER_GROUP = N_HEADS // N_KV_GROUPS
HEAD_DIM = 128
Q_WIDTH = N_HEADS * HEAD_DIM
KV_WIDTH = N_KV_GROUPS * HEAD_DIM
GROUP_Q_WIDTH = HEADS_PER_GROUP * HEAD_DIM
CMP_BLOCK = 32
CMP_STRIDE = 16
SLC_BLOCK = 64
SLC_TOPK = 16
WINDOW = 512
NSA_WINDOW_TILE = 256
FORCE_SCORE = 1e4
PEER_HEADS = 8
PEER_NKEYS = 128
PEER_QDIM = 256
PEER_HALF = PEER_QDIM // 2
PEER_TOPK = 16
PEER_CHUNK_KEYS = 4
PLE_DIM = 256
RMS_EPS = 1e-6
NEG = -1e30
LOG2E = 1.4426950408889634
GATE_LANES = 128

LANES = 128
V7X_VMEM_BYTES = 64 * 1024 * 1024
VMEM_LIMIT = V7X_VMEM_BYTES * 7 // 8

_NT = (((1,), (1,)), ((), ()))


def _params(*sem):
    return pltpu.CompilerParams(dimension_semantics=sem, vmem_limit_bytes=VMEM_LIMIT)


def _pick(n, cands):
    for c in cands:
        if n % c == 0:
            return c
    raise ValueError(f"no tile in {cands} divides {n}")


def _rmsnorm_kernel(x_ref, g_ref, o_ref):
    x = x_ref[...]
    ms = jnp.mean(x * x, axis=-1, keepdims=True)
    o_ref[...] = (x * lax.rsqrt(ms + RMS_EPS) * g_ref[...]).astype(o_ref.dtype)


def _rmsnorm(x, g, out_dtype=BF16):
    T, D = x.shape
    tm = _pick(T, (256, 128, 64, 8))
    return pl.pallas_call(
        _rmsnorm_kernel,
        grid=(T // tm,),
        in_specs=[pl.BlockSpec((tm, D), lambda i: (i, 0)), pl.BlockSpec((1, D), lambda i: (0, 0))],
        out_specs=pl.BlockSpec((tm, D), lambda i: (i, 0)),
        out_shape=jax.ShapeDtypeStruct((T, D), out_dtype),
        compiler_params=_params("parallel"),
        name="rmsnorm",
    )(x, g.reshape(1, D))


def _mm_kernel(a_ref, b_ref, *rest, has_res, b_is_nk):
    o_ref = rest[-1]
    if b_is_nk:
        acc = lax.dot_general(a_ref[...], b_ref[...], _NT, preferred_element_type=F32)
    else:
        acc = jnp.dot(a_ref[...], b_ref[...], preferred_element_type=F32)
    if has_res:
        acc = acc + rest[0][...]
    o_ref[...] = acc.astype(o_ref.dtype)


def _mm(a, b, out_dtype, res=None, b_is_nk=False, name="mm"):
    M, K = a.shape
    N = b.shape[0] if b_is_nk else b.shape[1]
    tm = _pick(M, (1024, 512, 256, 128))
    tn = _pick(N, (1024, 512, 256, 128))
    b_spec = pl.BlockSpec((tn, K), lambda j, i: (j, 0)) if b_is_nk else pl.BlockSpec((K, tn), lambda j, i: (0, j))
    in_specs = [pl.BlockSpec((tm, K), lambda j, i: (i, 0)), b_spec]
    args = [a, b]
    if res is not None:
        in_specs.append(pl.BlockSpec((tm, tn), lambda j, i: (i, j)))
        args.append(res)
    return pl.pallas_call(
        functools.partial(_mm_kernel, has_res=res is not None, b_is_nk=b_is_nk),
        grid=(N // tn, M // tm),
        in_specs=in_specs,
        out_specs=pl.BlockSpec((tm, tn), lambda j, i: (i, j)),
        out_shape=jax.ShapeDtypeStruct((M, N), out_dtype),
        compiler_params=_params("parallel", "parallel"),
        name=name,
    )(*args)


def _pool_merge_kernel(u_ref, halo_ref, pw_ref, ps_ref, wp_ref, nsa_ref, wn_ref, gp_ref, gn_ref,
                       o_ref, pool_scr, *, tm):
    i = pl.program_id(0)
    j = pl.program_id(1)

    @pl.when(j == 0)
    def _():
        u = u_ref[...].astype(F32)
        halo = halo_ref[...].astype(F32)
        halo = jnp.where(i == 0, 0.0, halo)
        ext = jnp.concatenate([halo, u], axis=0)
        t = (i * tm + lax.broadcasted_iota(jnp.int32, (tm, 1), 0)).astype(F32)
        for g, w in enumerate(POOL_WINDOWS):
            c0, c1 = g * POOL_GROUP_WIDTH, (g + 1) * POOL_GROUP_WIDTH
            s = ext[:, c0:c1]
            k = 1
            while k < w:
                s = s[:-k] + s[k:]
                k *= 2
            wsum = s[POOL_HALO + 1 - w:]
            cnt = jnp.minimum(t + 1.0, float(w))
            pooled = wsum / cnt - u[:, c0:c1]
            mixed = jnp.dot(pooled.astype(BF16), pw_ref[g], preferred_element_type=F32)
            pool_scr[:, c0:c1] = (mixed * ps_ref[:, c0:c1]).astype(BF16)

    up_pool = jnp.dot(pool_scr[...], wp_ref[...], preferred_element_type=F32)
    up_nsa = jnp.dot(nsa_ref[...], wn_ref[...], preferred_element_type=F32)
    merged = jax.nn.sigmoid(gp_ref[...].astype(F32)) * up_pool + jax.nn.sigmoid(gn_ref[...].astype(F32)) * up_nsa
    o_ref[...] = merged.astype(o_ref.dtype)


def _pool_merge(z, zg, pool_w, pool_scale, w_up_pool, nsa_out, w_up_nsa, D):
    T = z.shape[0]
    tm = _pick(T, (512, 256, 128))
    tn = _pick(D, (1024, 512, 256, 128))
    assert tm % POOL_HALO == 0
    hb = tm // POOL_HALO
    return pl.pallas_call(
        functools.partial(_pool_merge_kernel, tm=tm),
        grid=(T // tm, D // tn),
        in_specs=[
            pl.BlockSpec((tm, POOL_WIDTH), lambda i, j: (i, 0)),
            pl.BlockSpec((POOL_HALO, POOL_WIDTH), lambda i, j: (jnp.maximum(i * hb - 1, 0), 0)),
            pl.BlockSpec((POOL_GROUPS, POOL_GROUP_WIDTH, POOL_GROUP_WIDTH), lambda i, j: (0, 0, 0)),
            pl.BlockSpec((1, POOL_WIDTH), lambda i, j: (0, 0)),
            pl.BlockSpec((POOL_WIDTH, tn), lambda i, j: (0, j)),
            pl.BlockSpec((tm, Q_WIDTH), lambda i, j: (i, 0)),
            pl.BlockSpec((Q_WIDTH, tn), lambda i, j: (0, j)),
            pl.BlockSpec((tm, tn), lambda i, j: (i, j)),
            pl.BlockSpec((tm, tn), lambda i, j: (i, D // tn + j)),
        ],
        out_specs=pl.BlockSpec((tm, tn), lambda i, j: (i, j)),
        out_shape=jax.ShapeDtypeStruct((T, D), BF16),
        scratch_shapes=[pltpu.VMEM((tm, POOL_WIDTH), BF16)],
        compiler_params=_params("parallel", "arbitrary"),
        name="pool_merge",
    )(z, z, pool_w, pool_scale.reshape(1, POOL_WIDTH), w_up_pool, nsa_out, w_up_nsa, zg, zg)


def _compress_kernel(r_ref, w_ref, pos_ref, o_ref, *, n_cmp, transpose_out):
    half = CMP_STRIDE * HEAD_DIM
    r = r_ref[0]
    w = w_ref[...]
    top = jnp.dot(r, w[:half], preferred_element_type=F32)
    bot = jnp.dot(r, w[half:], preferred_element_type=F32)
    bias = jnp.dot(pos_ref[...], w, preferred_element_type=F32)
    ncp = top.shape[0]
    bot_next = jnp.concatenate([bot[1:], jnp.zeros((1, HEAD_DIM), F32)], axis=0)
    out = top + bot_next + bias
    row = lax.broadcasted_iota(jnp.int32, (ncp, 1), 0)
    out = jnp.where(row < n_cmp, out, 0.0)
    if transpose_out:
        o_ref[0] = out.T.astype(o_ref.dtype)
    else:
        o_ref[0] = out.astype(o_ref.dtype)


def _compress(r, w, pos, n_cmp, transpose_out):
    G, ncp, width = r.shape
    oshape = (G, HEAD_DIM, ncp) if transpose_out else (G, ncp, HEAD_DIM)
    return pl.pallas_call(
        functools.partial(_compress_kernel, n_cmp=n_cmp, transpose_out=transpose_out),
        grid=(G,),
        in_specs=[
            pl.BlockSpec((1, ncp, width), lambda g: (g, 0, 0)),
            pl.BlockSpec((CMP_BLOCK * HEAD_DIM, HEAD_DIM), lambda g: (0, 0)),
            pl.BlockSpec((1, CMP_BLOCK * HEAD_DIM), lambda g: (0, 0)),
        ],
        out_specs=pl.BlockSpec((1,) + oshape[1:], lambda g: (g, 0, 0)),
        out_shape=jax.ShapeDtypeStruct(oshape, BF16),
        compiler_params=_params("parallel"),
        name="nsa_compress",
    )(r, w, pos)


def _extract_topk(a, k, want_vals=False, exact=True):
    n = a.shape[0]
    idx = lax.broadcasted_iota(jnp.int32, a.shape, 0).astype(F32)
    rank = jnp.full(a.shape, float(k), F32)
    vals = []
    for r in range(k):
        m = jnp.max(a, axis=0, keepdims=True)
        hit = a == m
        if exact:
            first = jnp.min(jnp.where(hit, idx, float(n)), axis=0, keepdims=True)
            hit = idx == first
        rank = jnp.where(hit, float(r), rank)
        a = jnp.where(hit, -jnp.inf, a)
        vals.append(m)
    return (rank, vals) if want_vals else rank


def _cmp_select_kernel(parts_ref, q_ref, kc_ref, vct_ref, ovt_ref, cpos_ref, o_ref, sel_ref, used_ref, *, tq):
    g = pl.program_id(0)
    i = pl.program_id(1)
    ncp = kc_ref.shape[1]
    ns = ovt_ref.shape[0]
    dk = HEAD_DIM
    t = i * tq + lax.broadcasted_iota(jnp.int32, (1, tq), 1)
    n = lax.broadcasted_iota(jnp.int32, (ncp, 1), 0)
    valid = t >= (n * CMP_STRIDE + CMP_BLOCK - 1)
    sees_any = valid[0:1, :]
    kc = jnp.concatenate([kc_ref[0], cpos_ref[...]], axis=1)
    vct = vct_ref[0]
    q = q_ref[...]
    lane = lax.broadcasted_iota(jnp.int32, (tq, dk), 1)
    psum = jnp.zeros((ncp, tq), F32)
    for h in range(HEADS_PER_GROUP):
        qh = q[:, h * dk:(h + 1) * dk].astype(F32) * (dk ** -0.5 * LOG2E)
        parts = [parts_ref[(g * HEADS_PER_GROUP + h) * 3 + r] for r in range(3)]
        extra = jnp.zeros((tq, dk), F32)
        for r in range(6):
            extra = jnp.where(lane == r, parts[r % 3], extra)
        q_aug = jnp.concatenate([qh.astype(BF16), extra.astype(BF16)], axis=1)
        s = lax.dot_general(kc, q_aug, _NT, preferred_element_type=F32)
        s = jnp.where(valid, s, NEG)
        m = jnp.max(s, axis=0, keepdims=True)
        e = jnp.exp2(s - m)
        l = jnp.sum(e, axis=0, keepdims=True)
        p = e * jnp.where(sees_any, 1.0 / l, 0.0)
        ot = jnp.dot(vct, p.astype(BF16), preferred_element_type=F32)
        o_ref[:, h * dk:(h + 1) * dk] = ot.T.astype(o_ref.dtype)
        psum = psum + p
    p_hi = psum.astype(BF16)
    p_lo = (psum - p_hi.astype(F32)).astype(BF16)
    ovt = ovt_ref[...]
    imp = jnp.dot(ovt, p_hi, preferred_element_type=F32) + jnp.dot(ovt, p_lo, preferred_element_type=F32)
    kk = lax.broadcasted_iota(jnp.int32, (ns, 1), 0)
    cur = jnp.right_shift(t, SLC_BLOCK.bit_length() - 1)
    forced = (kk == 0) | (kk == cur) | (kk == cur - 1)
    score = jnp.where(forced, FORCE_SCORE, jnp.where(kk <= cur, imp, -1.0))
    k_sel = min(SLC_TOPK, ns)
    rank = _extract_topk(score, k_sel)
    sel = jnp.where((rank < float(k_sel)) & (kk <= cur), 1.0, 0.0)
    sel_ref[0] = sel.astype(sel_ref.dtype)
    used_ref[0, 0] = jnp.max(sel.T, axis=0, keepdims=True)


def _cmp_select(slope_parts, z, q_off, kc, vct, ovt, T):
    tq = _pick(T, (256, 128))
    ncp = kc.shape[1]
    ns = ovt.shape[0]
    assert q_off % GROUP_Q_WIDTH == 0
    qb = q_off // GROUP_Q_WIDTH
    cpos = _key_offset_columns(ncp, unit=CMP_STRIDE, split=32)
    return pl.pallas_call(
        functools.partial(_cmp_select_kernel, tq=tq),
        grid=(N_KV_GROUPS, T // tq),
        in_specs=[
            pl.BlockSpec(memory_space=pltpu.SMEM),
            pl.BlockSpec((tq, GROUP_Q_WIDTH), lambda g, i: (i, qb + g)),
            pl.BlockSpec((1, ncp, HEAD_DIM), lambda g, i: (g, 0, 0)),
            pl.BlockSpec((1, HEAD_DIM, ncp), lambda g, i: (g, 0, 0)),
            pl.BlockSpec((ns, ncp), lambda g, i: (0, 0)),
            pl.BlockSpec((ncp, HEAD_DIM), lambda g, i: (0, 0)),
        ],
        out_specs=[
            pl.BlockSpec((tq, GROUP_Q_WIDTH), lambda g, i: (i, g)),
            pl.BlockSpec((1, ns, tq), lambda g, i: (g, 0, i)),
            pl.BlockSpec((1, 1, 1, ns), lambda g, i: (g, i, 0, 0)),
        ],
        out_shape=[
            jax.ShapeDtypeStruct((T, Q_WIDTH), BF16),
            jax.ShapeDtypeStruct((N_KV_GROUPS, ns, T), BF16),
            jax.ShapeDtypeStruct((N_KV_GROUPS, T // tq, 1, ns), F32),
        ],
        compiler_params=_params("parallel", "parallel"),
        name="nsa_cmp_select",
    )(slope_parts, z, kc, vct, ovt, cpos)


def _nsa_attend_kernel(slope_ref, parts_ref, tile_ref, count_ref, q_ref, ks_ref, vst_ref, kw_ref, vwt_ref, ocmp_ref,
                       sel_ref, exp_ref, kpos_ref, gate_ref, o_ref, q_scr, acc_scr, *, tq, tk, wt, n_wt):
    g = pl.program_id(0)
    i = pl.program_id(1)
    t0 = i * tq
    hg = HEADS_PER_GROUP
    dk = HEAD_DIM
    q = q_ref[...]
    lane = lax.broadcasted_iota(jnp.int32, (tq, dk), 1)
    for h in range(hg):
        qh = q[:, h * dk:(h + 1) * dk].astype(F32) * (dk ** -0.5 * LOG2E)
        q_scr[h * tq:(h + 1) * tq, :dk] = qh.astype(BF16)
        parts = [parts_ref[(g * hg + h) * 3 + r] for r in range(3)]
        extra = jnp.zeros((tq, dk), F32)
        for r in range(6):
            extra = jnp.where(lane == r, parts[r % 3], extra)
        q_scr[h * tq:(h + 1) * tq, dk:] = extra.astype(BF16)
    q_all = q_scr[...]
    slopes = [slope_ref[g * hg + h] * LOG2E for h in range(hg)]
    tpos = t0 + lax.broadcasted_iota(jnp.int32, (1, tq), 1)

    selt = sel_ref[0]
    krel = lax.broadcasted_iota(jnp.int32, (tk, tq), 0)
    acc_scr[...] = jnp.zeros_like(acc_scr)
    step = g * pl.num_programs(1) + i
    list_base = step * exp_ref.shape[0]

    def kv_step(n, carry):
        ms, ls = carry
        j = tile_ref[list_base + n]
        s0 = pl.multiple_of(j * tk, tk)
        kt = ks_ref[pl.ds(s0, tk), :]
        vtt = vst_ref[j]
        chosen = jnp.dot(exp_ref[j], selt, preferred_element_type=F32)
        allow = (chosen > 0.5) & ((krel + s0) <= tpos)
        k_aug = jnp.concatenate([kt, kpos_ref[:tk, :]], axis=1)
        s_all = lax.dot_general(k_aug, q_all, _NT, preferred_element_type=F32)
        shift = (s0 - t0).astype(F32)
        new_ms, new_ls = [], []
        for h in range(hg):
            lanes = slice(h * tq, (h + 1) * tq)
            cj = slopes[h] * shift
            s = jnp.where(allow, s_all[:, lanes], NEG)
            m_new = jnp.maximum(ms[h], jnp.max(s, axis=0, keepdims=True) + cj)
            alpha = jnp.exp2(ms[h] - m_new)
            p = jnp.exp2(s - (m_new - cj))
            new_ls.append(alpha * ls[h] + jnp.sum(p, axis=0, keepdims=True))
            new_ms.append(m_new)
            rows = slice(h * dk, (h + 1) * dk)
            acc_scr[rows, :] = alpha * acc_scr[rows, :] + jnp.dot(vtt, p.astype(BF16), preferred_element_type=F32)
        return tuple(new_ms), tuple(new_ls)

    init = (tuple(jnp.full((1, tq), NEG, F32) for _ in range(hg)),
            tuple(jnp.zeros((1, tq), F32) for _ in range(hg)))
    ms, ls = lax.fori_loop(0, count_ref[step], kv_step, init)

    w0 = pl.multiple_of(jnp.maximum(t0 - WINDOW, 0), wt)
    span = n_wt * wt
    kw = kw_ref[pl.ds(w0, span), :]
    wi = w0 // wt
    vwt = jnp.concatenate([vwt_ref[wi + r] for r in range(n_wt)], axis=1)
    kpos_w = w0 + lax.broadcasted_iota(jnp.int32, (span, tq), 0)
    dw = tpos - kpos_w
    allow_w = (dw >= 0) & (dw < WINDOW)
    kw_aug = jnp.concatenate([kw, kpos_ref[:span, :]], axis=1)
    sw_all = lax.dot_general(kw_aug, q_all, _NT, preferred_element_type=F32)
    ows, lws = [], []
    for h in range(hg):
        s = jnp.where(allow_w, sw_all[:, h * tq:(h + 1) * tq], NEG)
        m = jnp.max(s, axis=0, keepdims=True)
        p = jnp.exp2(s - m)
        lws.append(jnp.sum(p, axis=0, keepdims=True))
        ows.append(jnp.dot(vwt, p.astype(BF16), preferred_element_type=F32))

    gt = jax.nn.sigmoid(gate_ref[...])
    gtt = gt.T
    ocmp = ocmp_ref[...].astype(F32)
    for h in range(hg):
        o_st = acc_scr[h * dk:(h + 1) * dk, :] / ls[h]
        o_wt = ows[h] / lws[h]
        o_t = gtt[3 * h + 1:3 * h + 2, :] * o_st + gtt[3 * h + 2:3 * h + 3, :] * o_wt
        o = gt[:, 3 * h:3 * h + 1] * ocmp[:, h * dk:(h + 1) * dk] + o_t.T
        o_ref[:, h * dk:(h + 1) * dk] = o.astype(o_ref.dtype)


def _nsa_attend(slopes, slope_parts, tile_list, tile_count, z, q_off, kv_off, vst, vwt, o_cmp, sel, expand, gates, T):
    tq = _pick(T, (256, 128))
    n_exp, tk, ns = expand.shape
    wt = vwt.shape[-1]
    n_wt = (WINDOW + tq) // wt
    kpos = _key_offset_columns(max(tk, n_wt * wt))
    assert T >= n_wt * wt and (WINDOW + tq) % wt == 0 and WINDOW % wt == 0 and tq % wt == 0
    assert q_off % GROUP_Q_WIDTH == 0 and kv_off % HEAD_DIM == 0
    qb = q_off // GROUP_Q_WIDTH
    kvb = kv_off // HEAD_DIM
    G = N_KV_GROUPS

    def k_spec(branch):
        return pl.BlockSpec((T, HEAD_DIM), lambda g, i: (0, kvb + branch * G + g))

    def vt_spec(a):
        return pl.BlockSpec((None,) + a.shape[1:], lambda g, i: (g, 0, 0, 0))

    return pl.pallas_call(
        functools.partial(_nsa_attend_kernel, tq=tq, tk=tk, wt=wt, n_wt=n_wt),
        grid=(G, T // tq),
        in_specs=[
            pl.BlockSpec(memory_space=pltpu.SMEM),
            pl.BlockSpec(memory_space=pltpu.SMEM),
            pl.BlockSpec(memory_space=pltpu.SMEM),
            pl.BlockSpec(memory_space=pltpu.SMEM),
            pl.BlockSpec((tq, GROUP_Q_WIDTH), lambda g, i: (i, qb + g)),
            k_spec(2), vt_spec(vst), k_spec(4), vt_spec(vwt),
            pl.BlockSpec((tq, GROUP_Q_WIDTH), lambda g, i: (i, g)),
            pl.BlockSpec((1, ns, tq), lambda g, i: (g, 0, i)),
            pl.BlockSpec((n_exp, tk, ns), lambda g, i: (0, 0, 0)),
            pl.BlockSpec(kpos.shape, lambda g, i: (0, 0)),
            pl.BlockSpec((tq, GATE_LANES), lambda g, i: (i, g)),
        ],
        out_specs=pl.BlockSpec((tq, GROUP_Q_WIDTH), lambda g, i: (i, g)),
        out_shape=jax.ShapeDtypeStruct((T, Q_WIDTH), BF16),
        scratch_shapes=[
            pltpu.VMEM((HEADS_PER_GROUP * tq, 2 * HEAD_DIM), BF16),
            pltpu.VMEM((HEADS_PER_GROUP * HEAD_DIM, tq), F32),
        ],
        compiler_params=_params("parallel", "arbitrary"),
        name="nsa_attend",
    )(slopes, slope_parts, tile_list, tile_count, z, z, vst, z, vwt, o_cmp, sel, expand, kpos, gates)


_CAND_PIECES = [(0, 16), (1, 8), (2, 8), (3, 8), (4, 8), (5, 8), (6, 8), (7, 8)]


def _peer_select_head(q, k1, k2, exact):
    K = PEER_TOPK
    s1 = lax.dot_general(k1, q[:, :PEER_HALF], _NT, preferred_element_type=F32)
    s2 = lax.dot_general(k2, q[:, PEER_HALF:], _NT, preferred_element_type=F32)
    rank1, v1 = _extract_topk(s1, K, want_vals=True, exact=exact)
    rank2, v2 = _extract_topk(s2, K, want_vals=True, exact=exact)
    v1s = jnp.concatenate(v1, axis=0)
    v2s = jnp.concatenate(v2, axis=0)
    pieces = [v1[r1] + v2s[:L] for r1, L in _CAND_PIECES]
    pieces.append(v1s[8:] + v2[0])
    cand = jnp.concatenate(pieces, axis=0)
    crank = _extract_topk(cand, K, exact=exact)
    chosen = crank < float(K)
    top = v1[0] + v2[0]
    z = jnp.sum(jnp.where(chosen, jnp.exp(cand - top), 0.0), axis=0, keepdims=True)
    chosen_f = jnp.where(chosen, 1.0, 0.0)
    counts = []
    off = 0
    for _, L in _CAND_PIECES:
        counts.append(jnp.sum(chosen_f[off:off + L], axis=0, keepdims=True))
        off += L
    tail = chosen_f[off:off + 8]
    for r in range(8):
        counts.append(tail[r:r + 1])
    cnt = jnp.zeros_like(s1)
    for r1 in range(K):
        cnt = jnp.where(rank1 == float(r1), counts[r1], cnt)

    def n_ranked(rank):
        return jnp.sum(jnp.where(rank < float(K), 1.0, 0.0), axis=0, keepdims=True)

    tied = (jnp.abs(n_ranked(rank1) - float(K)) + jnp.abs(n_ranked(rank2) - float(K))
            + jnp.abs(n_ranked(crank) - float(K)))
    return (cnt, jnp.exp(s1 - v1[0]) / z, rank2, jnp.exp(s2 - v2[0])), tied


def _peer_select_kernel(q_ref, k1_ref, k2_ref, cnt_ref, f1_ref, rank2_ref, e2_ref, *, heads):
    k1 = k1_ref[...]
    k2 = k2_ref[...]
    out_refs = (cnt_ref, f1_ref, rank2_ref, e2_ref)

    def run(exact):
        tied = None
        for hh in range(heads):
            q = q_ref[:, hh * PEER_QDIM:(hh + 1) * PEER_QDIM].astype(BF16)
            outs, t = _peer_select_head(q, k1, k2, exact)
            for ref, val in zip(out_refs, outs):
                ref[hh, 0] = val
            tied = t if tied is None else tied + t
        return jnp.max(tied)

    any_tie = run(False)

    @pl.when(any_tie > 0.0)
    def _():
        run(True)


def _peer_select(qp, keys1, keys2):
    T = qp.shape[0]
    tp = _pick(T, (128,))
    heads = 2
    shp = jax.ShapeDtypeStruct((PEER_HEADS, T // tp, PEER_NKEYS, tp), F32)
    ospec = pl.BlockSpec((heads, 1, PEER_NKEYS, tp), lambda i, h: (h, i, 0, 0))
    return pl.pallas_call(
        functools.partial(_peer_select_kernel, heads=heads),
        grid=(T // tp, PEER_HEADS // heads),
        in_specs=[
            pl.BlockSpec((tp, heads * PEER_QDIM), lambda i, h: (i, h)),
            pl.BlockSpec((PEER_NKEYS, PEER_HALF), lambda i, h: (0, 0)),
            pl.BlockSpec((PEER_NKEYS, PEER_HALF), lambda i, h: (0, 0)),
        ],
        out_specs=[ospec, ospec, ospec, ospec],
        out_shape=[shp, shp, shp, shp],
        compiler_params=_params("parallel", "parallel"),
        name="peer_select",
    )(qp, keys1, keys2)


def _peer_dense_kernel(h_ref, u_ref, v_ref, cnt_ref, f1_ref, rank2_ref, e2_ref, o_ref,
                       w_scr, at_scr, ht_scr, *, ipc, tm, out_chunk):
    c = pl.program_id(1)
    nc = pl.num_programs(1)
    nk = PEER_NKEYS
    lt = LANES
    D = o_ref.shape[1]

    def build_gates(chunk, k, tb, tie):
        i1 = chunk * ipc + k
        lanes = slice(tb * lt, (tb + 1) * lt)
        w = jnp.zeros((nk, lt), F32)
        for h in range(PEER_HEADS):
            crow = cnt_ref[h, tb, pl.ds(i1, 1), :]
            if h == 0:
                crow = crow + (tie if tie.shape[1] == lt else tie[:, lanes])
            frow = f1_ref[h, tb, pl.ds(i1, 1), :]
            w = w + jnp.where(rank2_ref[h, tb] < crow, e2_ref[h, tb] * frow, 0.0)
        w_scr[k * nk:(k + 1) * nk, lanes] = w

    tiles = [(k, tb) for k in range(ipc) for tb in range(tm // lt)]
    early, late = tiles[:len(tiles) // 2], tiles[len(tiles) // 2:]
    zero_row = jnp.zeros((1, tm), F32)

    @pl.when(c == 0)
    def _():
        o_ref[...] = jnp.zeros_like(o_ref)
        for d0 in range(0, D, out_chunk):
            ht_scr[d0:d0 + out_chunk, :] = h_ref[:, d0:d0 + out_chunk].astype(F32).T.astype(BF16)
        for k, tb in early:
            build_gates(c, k, tb, zero_row)

    half = (ipc * nk) // 2
    ht = ht_scr[...]
    at_scr[:half, :] = jnp.dot(u_ref[:half, :], ht, preferred_element_type=F32)
    tie = at_scr[0:1, :] * 0.0
    at_scr[half:, :] = jnp.dot(u_ref[half:, :], ht, preferred_element_type=F32)
    for k, tb in late:
        build_gates(c, k, tb, tie)
    coef = (w_scr[...] * jax.nn.gelu(at_scr[...])).T.astype(BF16)
    nxt = jnp.minimum(c + 1, nc - 1)
    n_out = D // out_chunk
    for j in range(n_out):
        d0 = j * out_chunk
        o_ref[:, d0:d0 + out_chunk] += jnp.dot(coef, v_ref[:, d0:d0 + out_chunk], preferred_element_type=F32)
        tie2 = o_ref[0:1, d0:d0 + lt] * 0.0
        for k, tb in early[j::n_out]:
            build_gates(nxt, k, tb, tie2)


def _peer_dense(h2, u, v, cnt, f1, rank2, e2):
    T, D = h2.shape
    NE = u.shape[0]
    tm = _pick(T, (512, 256, 128))
    ipc = PEER_CHUNK_KEYS
    ec = ipc * PEER_NKEYS
    once = pl.Buffered(1)
    sel_spec = pl.BlockSpec((PEER_HEADS, tm // LANES, PEER_NKEYS, LANES), lambda i, c: (0, i, 0, 0),
                            pipeline_mode=once)
    return pl.pallas_call(
        functools.partial(_peer_dense_kernel, ipc=ipc, tm=tm, out_chunk=_pick(D, (512, 256, 128))),
        grid=(T // tm, NE // ec),
        in_specs=[
            pl.BlockSpec((tm, D), lambda i, c: (i, 0), pipeline_mode=once),
            pl.BlockSpec((ec, D), lambda i, c: (c, 0)),
            pl.BlockSpec((ec, D), lambda i, c: (c, 0)),
            sel_spec, sel_spec, sel_spec, sel_spec,
        ],
        out_specs=pl.BlockSpec((tm, D), lambda i, c: (i, 0), pipeline_mode=once),
        out_shape=jax.ShapeDtypeStruct((T, D), F32),
        scratch_shapes=[pltpu.VMEM((ec, tm), F32), pltpu.VMEM((ec, tm), F32), pltpu.VMEM((D, tm), BF16)],
        compiler_params=_params("parallel", "arbitrary"),
        name="peer_dense",
    )(h2, u, v, cnt, f1, rank2, e2)


def _ple_final_kernel(x1_ref, peer_ref, gple_ref, wg_ref, p_ref, wp_ref, gfin_ref, o_ref, acc_scr, ss_scr,
                      *, tk, nk, tn):
    j = pl.program_id(1)
    D = acc_scr.shape[1]
    x2 = x1_ref[...] + peer_ref[...]
    lhs = (x2 * gple_ref[...]).astype(BF16)
    ssq = jnp.sum(x2 * x2, axis=-1, keepdims=True)

    @pl.when(j == 0)
    def _():
        acc_scr[...] = jnp.zeros_like(acc_scr)
        ss_scr[...] = jnp.zeros_like(ss_scr)

    ss_scr[...] += ssq
    for n0 in range(0, D, tn):
        acc_scr[:, n0:n0 + tn] += jnp.dot(lhs, wg_ref[:, n0:n0 + tn], preferred_element_type=F32)
    for jj in range(nk):
        @pl.when(j == jj)
        def _():
            o_ref[:, jj * tk:(jj + 1) * tk] = x2

    @pl.when(j == nk - 1)
    def _():
        inv = lax.rsqrt(ss_scr[...] / float(D) + RMS_EPS)
        pb = p_ref[...].astype(BF16)
        ss3 = jnp.zeros_like(inv)
        for n0 in range(0, D, tn):
            cols = slice(n0, n0 + tn)
            gate = jax.nn.sigmoid(acc_scr[:, cols] * inv)
            proj = jnp.dot(pb, wp_ref[:, cols], preferred_element_type=F32)
            x3 = o_ref[:, cols] + gate * proj
            o_ref[:, cols] = x3
            ss3 = ss3 + jnp.sum(x3 * x3, axis=-1, keepdims=True)
        inv3 = lax.rsqrt(ss3 / float(D) + RMS_EPS)
        for n0 in range(0, D, tn):
            cols = slice(n0, n0 + tn)
            o_ref[:, cols] = o_ref[:, cols] * inv3 * gfin_ref[:, cols]


def _ple_final(x1, peer, g_ple, w_gate, p, w_proj, g_final):
    T, D = x1.shape
    tm = _pick(T, (512, 256, 128))
    tk = _pick(D, (1024, 512, 256, 128))
    nk = D // tk
    once = pl.Buffered(1)
    return pl.pallas_call(
        functools.partial(_ple_final_kernel, tk=tk, nk=nk, tn=tk),
        grid=(T // tm, nk),
        in_specs=[
            pl.BlockSpec((tm, tk), lambda i, j: (i, j)),
            pl.BlockSpec((tm, tk), lambda i, j: (i, j)),
            pl.BlockSpec((1, tk), lambda i, j: (0, j)),
            pl.BlockSpec((tk, D), lambda i, j: (j, 0)),
            pl.BlockSpec((tm, PLE_DIM), lambda i, j: (i, 0)),
            pl.BlockSpec((PLE_DIM, D), lambda i, j: (0, 0), pipeline_mode=once),
            pl.BlockSpec((1, D), lambda i, j: (0, 0)),
        ],
        out_specs=pl.BlockSpec((tm, D), lambda i, j: (i, 0), pipeline_mode=once),
        out_shape=jax.ShapeDtypeStruct((T, D), F32),
        scratch_shapes=[pltpu.VMEM((tm, D), F32), pltpu.VMEM((tm, 1), F32)],
        compiler_params=_params("parallel", "arbitrary"),
        name="ple_final",
    )(x1, peer, g_ple.reshape(1, D), w_gate, p, w_proj, g_final.reshape(1, D))


def _alibi_slopes():
    return jnp.asarray(2.0 ** (-8.0 * np.arange(1, N_HEADS + 1) / N_HEADS), dtype=F32)


def _alibi_slope_parts():
    s = (2.0 ** (-8.0 * np.arange(1, N_HEADS + 1) / N_HEADS) * LOG2E).astype(np.float32)
    parts = []
    for _ in range(3):
        hi = s.astype(BF16).astype(np.float32)
        parts.append(hi)
        s = (s - hi).astype(np.float32)
    return jnp.asarray(np.stack(parts, axis=1).reshape(-1), dtype=F32)


def _key_offset_columns(n, unit=1, split=16):
    r = np.arange(n)
    hi, lo = unit * split * (r // split), unit * (r % split)
    for part in (hi, lo):
        assert np.array_equal(part.astype(np.float32).astype(BF16).astype(np.float32), part)
    cols = np.zeros((n, HEAD_DIM), np.float32)
    cols[:, 0:3] = hi[:, None]
    cols[:, 3:6] = lo[:, None]
    return jnp.asarray(cols, dtype=BF16)


def _overlap_t(T):
    ncp = T // CMP_STRIDE
    n_cmp = (T - CMP_BLOCK) // CMP_STRIDE + 1
    n_sel = T // SLC_BLOCK
    c_start = np.arange(ncp) * CMP_STRIDE
    s_start = np.arange(n_sel) * SLC_BLOCK
    ov = (c_start[None, :] < s_start[:, None] + SLC_BLOCK) & (c_start[None, :] + CMP_BLOCK > s_start[:, None])
    ov = ov & (np.arange(ncp)[None, :] < n_cmp)
    return jnp.asarray(ov.astype(np.float32), dtype=BF16)


def _expand_blocks(T, tk):
    n_sel = T // SLC_BLOCK
    key_blk = (np.arange(T) // SLC_BLOCK).reshape(T // tk, tk, 1)
    e = key_blk == np.arange(n_sel).reshape(1, 1, n_sel)
    return jnp.asarray(e.astype(np.float32), dtype=BF16)


def _tile_lists(used, n_tiles):
    G, nq, _, ns = used.shape
    flags = used.reshape(G, nq, n_tiles, ns // n_tiles).max(axis=-1) > 0.0
    order = jnp.argsort(jnp.where(flags, 0, 1), axis=-1, stable=True)
    return order.astype(jnp.int32).reshape(-1), flags.sum(axis=-1).astype(jnp.int32).reshape(-1)


def _layer(x, p, norm_mix_g, w_in, pool_w, pool_scale, cmp_pos_k, cmp_pos_v, cmp_w_k, cmp_w_v,
           w_up_pool, w_up_nsa, w_out, norm_ffn_g, peer_w_q, peer_keys1, peer_keys2, peer_u, peer_v,
           norm_ple_g, ple_w_gate, ple_w_proj, norm_final_g):
    T, D = x.shape
    G = N_KV_GROUPS
    q_off = POOL_WIDTH
    kv_off = q_off + Q_WIDTH
    gates_off = kv_off + 6 * KV_WIDTH
    gp_src = gates_off + 3 * N_HEADS

    w_nk = jnp.swapaxes(w_in, 0, 1)
    w_head = w_nk[:gates_off].astype(BF16)
    w_merge_gates = w_nk[gp_src:].astype(BF16)
    w_g = w_nk[gates_off:gp_src].reshape(G, 3 * HEADS_PER_GROUP, D)
    w_g = jnp.pad(w_g, ((0, 0), (0, GATE_LANES - 3 * HEADS_PER_GROUP), (0, 0))).reshape(G * GATE_LANES, D)
    w_g = w_g.astype(BF16)

    h = _rmsnorm(x, norm_mix_g)
    z = _mm(h, w_head, BF16, b_is_nk=True, name="in_proj")
    zg = _mm(h, w_merge_gates, BF16, b_is_nk=True, name="merge_gate_proj")
    gates = _mm(h, w_g, F32, b_is_nk=True, name="gate_proj")

    ncp = T // CMP_STRIDE
    n_cmp = (T - CMP_BLOCK) // CMP_STRIDE + 1

    def blocks16(col):
        a = z[:, col:col + KV_WIDTH].reshape(ncp, CMP_STRIDE, G, HEAD_DIM)
        return a.transpose(2, 0, 1, 3).reshape(G, ncp, CMP_STRIDE * HEAD_DIM)

    kc = _compress(blocks16(kv_off), cmp_w_k.astype(BF16), cmp_pos_k.reshape(1, -1).astype(BF16), n_cmp, False)
    vct = _compress(blocks16(kv_off + KV_WIDTH), cmp_w_v.astype(BF16), cmp_pos_v.reshape(1, -1).astype(BF16),
                    n_cmp, True)
    slopes = _alibi_slopes()
    slope_parts = _alibi_slope_parts()
    o_cmp, sel, used = _cmp_select(slope_parts, z, q_off, kc, vct, _overlap_t(T), T)
    tk = _pick(T, (512, 256))
    tile_list, tile_count = _tile_lists(used, T // tk)

    def value_tiles(branch, tile):
        col = kv_off + branch * KV_WIDTH
        a = z[:, col:col + KV_WIDTH].reshape(T // tile, tile, G, HEAD_DIM)
        return a.transpose(2, 0, 3, 1)

    nsa_out = _nsa_attend(slopes, slope_parts, tile_list, tile_count, z, q_off, kv_off, value_tiles(3, tk),
                          value_tiles(5, NSA_WINDOW_TILE), o_cmp, sel, _expand_blocks(T, tk), gates, T)

    merged = _pool_merge(z, zg, pool_w.astype(BF16), pool_scale, w_up_pool.astype(BF16),
                         nsa_out, w_up_nsa.astype(BF16), D)
    x1 = _mm(merged, w_out.astype(BF16), F32, res=x, name="out_proj")

    h2 = _rmsnorm(x1, norm_ffn_g)
    qp = _mm(h2, peer_w_q.astype(BF16), F32, name="peer_q")
    cnt, f1, rank2, e2 = _peer_select(qp, peer_keys1.astype(BF16), peer_keys2.astype(BF16))
    peer = _peer_dense(h2, peer_u.astype(BF16), peer_v.astype(BF16), cnt, f1, rank2, e2)

    return _ple_final(x1, peer, norm_ple_g, ple_w_gate.astype(BF16), p, ple_w_proj.astype(BF16), norm_final_g)


def kernel(x, p, norm_mix_g, w_in, pool_w, pool_scale, cmp_pos_k, cmp_pos_v, cmp_w_k, cmp_w_v, w_up_pool, w_up_nsa, w_out, norm_ffn_g, peer_w_q, peer_keys1, peer_keys2, peer_u, peer_v, norm_ple_g, ple_w_gate, ple_w_proj, norm_final_g):
    B, T, D = x.shape
    depth = w_in.shape[0]
    assert depth == 1, "the final rmsnorm is fused into the layer's last kernel"
    outs = []
    for b in range(B):
        outs.append(_layer(
            x[b], p[0, b], norm_mix_g[0], w_in[0], pool_w[0], pool_scale[0], cmp_pos_k[0], cmp_pos_v[0],
            cmp_w_k[0], cmp_w_v[0], w_up_pool[0], w_up_nsa[0], w_out[0], norm_ffn_g[0], peer_w_q[0],
            peer_keys1[0], peer_keys2[0], peer_u[0], peer_v[0], norm_ple_g[0], ple_w_gate[0], ple_w_proj[0],
            norm_final_g))
    return outs[0][None] if B == 1 else jnp.stack(outs, axis=0)
```

```python
import functools

import numpy as np
import jax
import jax.numpy as jnp
from jax import lax
from jax.experimental import pallas as pl
from jax.experimental.pallas import tpu as pltpu

F32 = jnp.float32
BF16 = jnp.bfloat16

POOL_GROUPS = 4
POOL_GROUP_WIDTH = 512
POOL_WIDTH = POOL_GROUPS * POOL_GROUP_WIDTH
POOL_WINDOWS = (2, 4, 8, 16)
POOL_HALO = 16
N_HEADS = 16
N_KV_GROUPS = 4
HEADS_PER_GROUP = N_HEADS // N_KV_GROUPS
HEAD_DIM = 128
Q_WIDTH = N_HEADS * HEAD_DIM
KV_WIDTH = N_KV_GROUPS * HEAD_DIM
GROUP_Q_WIDTH = HEADS_PER_GROUP * HEAD_DIM
CMP_BLOCK = 32
CMP_STRIDE = 16
SLC_BLOCK = 64
SLC_TOPK = 16
WINDOW = 512
NSA_WINDOW_TILE = 256
FORCE_SCORE = 1e4
PEER_HEADS = 8
PEER_NKEYS = 128
PEER_QDIM = 256
PEER_HALF = PEER_QDIM // 2
PEER_TOPK = 16
PEER_CHUNK_KEYS = 4
PLE_DIM = 256
RMS_EPS = 1e-6
NEG = -1e30
LOG2E = 1.4426950408889634
GATE_LANES = 128

LANES = 128
V7X_VMEM_BYTES = 64 * 1024 * 1024
VMEM_LIMIT = V7X_VMEM_BYTES * 7 // 8

_NT = (((1,), (1,)), ((), ()))


def _params(*sem):
    return pltpu.CompilerParams(dimension_semantics=sem, vmem_limit_bytes=VMEM_LIMIT)


def _pick(n, cands):
    for c in cands:
        if n % c == 0:
            return c
    raise ValueError(f"no tile in {cands} divides {n}")


def _rmsnorm_kernel(x_ref, g_ref, o_ref):
    x = x_ref[...]
    ms = jnp.mean(x * x, axis=-1, keepdims=True)
    o_ref[...] = (x * lax.rsqrt(ms + RMS_EPS) * g_ref[...]).astype(o_ref.dtype)


def _rmsnorm(x, g, out_dtype=BF16):
    T, D = x.shape
    tm = _pick(T, (256, 128, 64, 8))
    return pl.pallas_call(
        _rmsnorm_kernel,
        grid=(T // tm,),
        in_specs=[pl.BlockSpec((tm, D), lambda i: (i, 0)), pl.BlockSpec((1, D), lambda i: (0, 0))],
        out_specs=pl.BlockSpec((tm, D), lambda i: (i, 0)),
        out_shape=jax.ShapeDtypeStruct((T, D), out_dtype),
        compiler_params=_params("parallel"),
        name="rmsnorm",
    )(x, g.reshape(1, D))


def _mm_kernel(a_ref, b_ref, *rest, has_res, b_is_nk):
    o_ref = rest[-1]
    if b_is_nk:
        acc = lax.dot_general(a_ref[...], b_ref[...], _NT, preferred_element_type=F32)
    else:
        acc = jnp.dot(a_ref[...], b_ref[...], preferred_element_type=F32)
    if has_res:
        acc = acc + rest[0][...]
    o_ref[...] = acc.astype(o_ref.dtype)


def _mm(a, b, out_dtype, res=None, b_is_nk=False, n_out=None, name="mm"):
    M, K = a.shape
    N = n_out if n_out is not None else (b.shape[0] if b_is_nk else b.shape[1])
    tm = _pick(M, (1024, 512, 256, 128))
    tn = _pick(N, (1024, 512, 256, 128))
    b_spec = pl.BlockSpec((tn, K), lambda j, i: (j, 0)) if b_is_nk else pl.BlockSpec((K, tn), lambda j, i: (0, j))
    in_specs = [pl.BlockSpec((tm, K), lambda j, i: (i, 0)), b_spec]
    args = [a, b]
    if res is not None:
        in_specs.append(pl.BlockSpec((tm, tn), lambda j, i: (i, j)))
        args.append(res)
    return pl.pallas_call(
        functools.partial(_mm_kernel, has_res=res is not None, b_is_nk=b_is_nk),
        grid=(N // tn, M // tm),
        in_specs=in_specs,
        out_specs=pl.BlockSpec((tm, tn), lambda j, i: (i, j)),
        out_shape=jax.ShapeDtypeStruct((M, N), out_dtype),
        compiler_params=_params("parallel", "parallel"),
        name=name,
    )(*args)


def _pool_merge_kernel(u_ref, halo_ref, pw_ref, ps_ref, wp_ref, nsa_ref, wn_ref, gp_ref, gn_ref,
                       o_ref, pool_scr, *, tm):
    i = pl.program_id(0)
    j = pl.program_id(1)

    @pl.when(j == 0)
    def _():
        u = u_ref[...].astype(F32)
        halo = halo_ref[...].astype(F32)
        halo = jnp.where(i == 0, 0.0, halo)
        ext = jnp.concatenate([halo, u], axis=0)
        t = (i * tm + lax.broadcasted_iota(jnp.int32, (tm, 1), 0)).astype(F32)
        for g, w in enumerate(POOL_WINDOWS):
            c0, c1 = g * POOL_GROUP_WIDTH, (g + 1) * POOL_GROUP_WIDTH
            s = ext[:, c0:c1]
            k = 1
            while k < w:
                s = s[:-k] + s[k:]
                k *= 2
            wsum = s[POOL_HALO + 1 - w:]
            cnt = jnp.minimum(t + 1.0, float(w))
            pooled = wsum / cnt - u[:, c0:c1]
            mixed = jnp.dot(pooled.astype(BF16), pw_ref[g], preferred_element_type=F32)
            pool_scr[:, c0:c1] = (mixed * ps_ref[:, c0:c1]).astype(BF16)

    up_pool = jnp.dot(pool_scr[...], wp_ref[...], preferred_element_type=F32)
    up_nsa = jnp.dot(nsa_ref[...], wn_ref[...], preferred_element_type=F32)
    merged = jax.nn.sigmoid(gp_ref[...].astype(F32)) * up_pool + jax.nn.sigmoid(gn_ref[...].astype(F32)) * up_nsa
    o_ref[...] = merged.astype(o_ref.dtype)


def _pool_merge(z, zg, pool_w, pool_scale, w_up_pool, nsa_out, w_up_nsa, D):
    T = z.shape[0]
    tm = _pick(T, (512, 256, 128))
    tn = _pick(D, (1024, 512, 256, 128))
    assert tm % POOL_HALO == 0
    hb = tm // POOL_HALO
    return pl.pallas_call(
        functools.partial(_pool_merge_kernel, tm=tm),
        grid=(T // tm, D // tn),
        in_specs=[
            pl.BlockSpec((tm, POOL_WIDTH), lambda i, j: (i, 0)),
            pl.BlockSpec((POOL_HALO, POOL_WIDTH), lambda i, j: (jnp.maximum(i * hb - 1, 0), 0)),
            pl.BlockSpec((POOL_GROUPS, POOL_GROUP_WIDTH, POOL_GROUP_WIDTH), lambda i, j: (0, 0, 0)),
            pl.BlockSpec((1, POOL_WIDTH), lambda i, j: (0, 0)),
            pl.BlockSpec((POOL_WIDTH, tn), lambda i, j: (0, j)),
            pl.BlockSpec((tm, Q_WIDTH), lambda i, j: (i, 0)),
            pl.BlockSpec((Q_WIDTH, tn), lambda i, j: (0, j)),
            pl.BlockSpec((tm, tn), lambda i, j: (i, j)),
            pl.BlockSpec((tm, tn), lambda i, j: (i, D // tn + j)),
        ],
        out_specs=pl.BlockSpec((tm, tn), lambda i, j: (i, j)),
        out_shape=jax.ShapeDtypeStruct((T, D), BF16),
        scratch_shapes=[pltpu.VMEM((tm, POOL_WIDTH), BF16)],
        compiler_params=_params("parallel", "arbitrary"),
        name="pool_merge",
    )(z, z, pool_w, pool_scale.reshape(1, POOL_WIDTH), w_up_pool, nsa_out, w_up_nsa, zg, zg)


def _compress_kernel(r_ref, w_ref, pos_ref, o_ref, *, n_cmp, transpose_out):
    half = CMP_STRIDE * HEAD_DIM
    r = r_ref[0]
    w = w_ref[...]
    top = jnp.dot(r, w[:half], preferred_element_type=F32)
    bot = jnp.dot(r, w[half:], preferred_element_type=F32)
    bias = jnp.dot(pos_ref[...], w, preferred_element_type=F32)
    ncp = top.shape[0]
    bot_next = jnp.concatenate([bot[1:], jnp.zeros((1, HEAD_DIM), F32)], axis=0)
    out = top + bot_next + bias
    row = lax.broadcasted_iota(jnp.int32, (ncp, 1), 0)
    out = jnp.where(row < n_cmp, out, 0.0)
    if transpose_out:
        o_ref[0] = out.T.astype(o_ref.dtype)
    else:
        o_ref[0] = out.astype(o_ref.dtype)


def _compress(r, w, pos, n_cmp, transpose_out):
    G, ncp, width = r.shape
    oshape = (G, HEAD_DIM, ncp) if transpose_out else (G, ncp, HEAD_DIM)
    return pl.pallas_call(
        functools.partial(_compress_kernel, n_cmp=n_cmp, transpose_out=transpose_out),
        grid=(G,),
        in_specs=[
            pl.BlockSpec((1, ncp, width), lambda g: (g, 0, 0)),
            pl.BlockSpec((CMP_BLOCK * HEAD_DIM, HEAD_DIM), lambda g: (0, 0)),
            pl.BlockSpec((1, CMP_BLOCK * HEAD_DIM), lambda g: (0, 0)),
        ],
        out_specs=pl.BlockSpec((1,) + oshape[1:], lambda g: (g, 0, 0)),
        out_shape=jax.ShapeDtypeStruct(oshape, BF16),
        compiler_params=_params("parallel"),
        name="nsa_compress",
    )(r, w, pos)


def _extract_topk(a, k, want_vals=False, exact=True):
    n = a.shape[0]
    idx = lax.broadcasted_iota(jnp.int32, a.shape, 0).astype(F32)
    rank = jnp.full(a.shape, float(k), F32)
    vals = []
    for r in range(k):
        m = jnp.max(a, axis=0, keepdims=True)
        hit = a == m
        if exact:
            first = jnp.min(jnp.where(hit, idx, float(n)), axis=0, keepdims=True)
            hit = idx == first
        rank = jnp.where(hit, float(r), rank)
        a = jnp.where(hit, -jnp.inf, a)
        vals.append(m)
    return (rank, vals) if want_vals else rank


def _cmp_select_kernel(parts_ref, q_ref, kc_ref, vct_ref, ovt_ref, cpos_ref, o_ref, sel_ref, used_ref, *, tq):
    g = pl.program_id(0)
    i = pl.program_id(1)
    ncp = kc_ref.shape[1]
    ns = ovt_ref.shape[0]
    dk = HEAD_DIM
    t = i * tq + lax.broadcasted_iota(jnp.int32, (1, tq), 1)
    n = lax.broadcasted_iota(jnp.int32, (ncp, 1), 0)
    valid = t >= (n * CMP_STRIDE + CMP_BLOCK - 1)
    sees_any = valid[0:1, :]
    kc = jnp.concatenate([kc_ref[0], cpos_ref[...]], axis=1)
    vct = vct_ref[0]
    q = q_ref[...]
    lane = lax.broadcasted_iota(jnp.int32, (tq, dk), 1)
    psum = jnp.zeros((ncp, tq), F32)
    for h in range(HEADS_PER_GROUP):
        qh = q[:, h * dk:(h + 1) * dk].astype(F32) * (dk ** -0.5 * LOG2E)
        parts = [parts_ref[(g * HEADS_PER_GROUP + h) * 3 + r] for r in range(3)]
        extra = jnp.zeros((tq, dk), F32)
        for r in range(6):
            extra = jnp.where(lane == r, parts[r % 3], extra)
        q_aug = jnp.concatenate([qh.astype(BF16), extra.astype(BF16)], axis=1)
        s = lax.dot_general(kc, q_aug, _NT, preferred_element_type=F32)
        s = jnp.where(valid, s, NEG)
        m = jnp.max(s, axis=0, keepdims=True)
        e = jnp.exp2(s - m)
        l = jnp.sum(e, axis=0, keepdims=True)
        p = e * jnp.where(sees_any, 1.0 / l, 0.0)
        ot = jnp.dot(vct, p.astype(BF16), preferred_element_type=F32)
        o_ref[:, h * dk:(h + 1) * dk] = ot.T.astype(o_ref.dtype)
        psum = psum + p
    p_hi = psum.astype(BF16)
    p_lo = (psum - p_hi.astype(F32)).astype(BF16)
    ovt = ovt_ref[...]
    imp = jnp.dot(ovt, p_hi, preferred_element_type=F32) + jnp.dot(ovt, p_lo, preferred_element_type=F32)
    kk = lax.broadcasted_iota(jnp.int32, (ns, 1), 0)
    cur = jnp.right_shift(t, SLC_BLOCK.bit_length() - 1)
    forced = (kk == 0) | (kk == cur) | (kk == cur - 1)
    score = jnp.where(forced, FORCE_SCORE, jnp.where(kk <= cur, imp, -1.0))
    k_sel = min(SLC_TOPK, ns)
    rank = _extract_topk(score, k_sel)
    sel = jnp.where((rank < float(k_sel)) & (kk <= cur), 1.0, 0.0)
    sel_ref[0] = sel.astype(sel_ref.dtype)
    used_ref[0, 0] = jnp.max(sel.T, axis=0, keepdims=True)


def _cmp_select(slope_parts, z, q_off, kc, vct, ovt, T):
    tq = _pick(T, (256, 128))
    ncp = kc.shape[1]
    ns = ovt.shape[0]
    assert q_off % GROUP_Q_WIDTH == 0
    qb = q_off // GROUP_Q_WIDTH
    cpos = _key_offset_columns(ncp, unit=CMP_STRIDE, split=32)
    return pl.pallas_call(
        functools.partial(_cmp_select_kernel, tq=tq),
        grid=(N_KV_GROUPS, T // tq),
        in_specs=[
            pl.BlockSpec(memory_space=pltpu.SMEM),
            pl.BlockSpec((tq, GROUP_Q_WIDTH), lambda g, i: (i, qb + g)),
            pl.BlockSpec((1, ncp, HEAD_DIM), lambda g, i: (g, 0, 0)),
            pl.BlockSpec((1, HEAD_DIM, ncp), lambda g, i: (g, 0, 0)),
            pl.BlockSpec((ns, ncp), lambda g, i: (0, 0)),
            pl.BlockSpec((ncp, HEAD_DIM), lambda g, i: (0, 0)),
        ],
        out_specs=[
            pl.BlockSpec((tq, GROUP_Q_WIDTH), lambda g, i: (i, g)),
            pl.BlockSpec((1, ns, tq), lambda g, i: (g, 0, i)),
            pl.BlockSpec((1, 1, 1, ns), lambda g, i: (g, i, 0, 0)),
        ],
        out_shape=[
            jax.ShapeDtypeStruct((T, Q_WIDTH), BF16),
            jax.ShapeDtypeStruct((N_KV_GROUPS, ns, T), BF16),
            jax.ShapeDtypeStruct((N_KV_GROUPS, T // tq, 1, ns), F32),
        ],
        compiler_params=_params("parallel", "parallel"),
        name="nsa_cmp_select",
    )(slope_parts, z, kc, vct, ovt, cpos)


def _nsa_attend_kernel(slope_ref, parts_ref, tile_ref, count_ref, q_ref, ks_ref, vst_ref, kw_ref, vwt_ref, ocmp_ref,
                       sel_ref, exp_ref, kpos_ref, gate_ref, o_ref, q_scr, acc_scr, *, tq, tk, wt, n_wt):
    g = pl.program_id(0)
    i = pl.program_id(1)
    t0 = i * tq
    hg = HEADS_PER_GROUP
    dk = HEAD_DIM
    q = q_ref[...]
    lane = lax.broadcasted_iota(jnp.int32, (tq, dk), 1)
    for h in range(hg):
        qh = q[:, h * dk:(h + 1) * dk].astype(F32) * (dk ** -0.5 * LOG2E)
        q_scr[h * tq:(h + 1) * tq, :dk] = qh.astype(BF16)
        parts = [parts_ref[(g * hg + h) * 3 + r] for r in range(3)]
        extra = jnp.zeros((tq, dk), F32)
        for r in range(6):
            extra = jnp.where(lane == r, parts[r % 3], extra)
        q_scr[h * tq:(h + 1) * tq, dk:] = extra.astype(BF16)
    q_all = q_scr[...]
    slopes = [slope_ref[g * hg + h] * LOG2E for h in range(hg)]
    tpos = t0 + lax.broadcasted_iota(jnp.int32, (1, tq), 1)

    selt = sel_ref[0]
    krel = lax.broadcasted_iota(jnp.int32, (tk, tq), 0)
    acc_scr[...] = jnp.zeros_like(acc_scr)
    step = g * pl.num_programs(1) + i
    list_base = step * exp_ref.shape[0]

    def kv_step(n, carry):
        ms, ls = carry
        j = tile_ref[list_base + n]
        s0 = pl.multiple_of(j * tk, tk)
        kt = ks_ref[pl.ds(s0, tk), :]
        vtt = vst_ref[j]
        chosen = jnp.dot(exp_ref[j], selt, preferred_element_type=F32)
        allow = (chosen > 0.5) & ((krel + s0) <= tpos)
        k_aug = jnp.concatenate([kt, kpos_ref[:tk, :]], axis=1)
        s_all = lax.dot_general(k_aug, q_all, _NT, preferred_element_type=F32)
        shift = (s0 - t0).astype(F32)
        new_ms, new_ls = [], []
        for h in range(hg):
            lanes = slice(h * tq, (h + 1) * tq)
            cj = slopes[h] * shift
            s = jnp.where(allow, s_all[:, lanes], NEG)
            m_new = jnp.maximum(ms[h], jnp.max(s, axis=0, keepdims=True) + cj)
            alpha = jnp.exp2(ms[h] - m_new)
            p = jnp.exp2(s - (m_new - cj))
            new_ls.append(alpha * ls[h] + jnp.sum(p, axis=0, keepdims=True))
            new_ms.append(m_new)
            rows = slice(h * dk, (h + 1) * dk)
            acc_scr[rows, :] = alpha * acc_scr[rows, :] + jnp.dot(vtt, p.astype(BF16), preferred_element_type=F32)
        return tuple(new_ms), tuple(new_ls)

    init = (tuple(jnp.full((1, tq), NEG, F32) for _ in range(hg)),
            tuple(jnp.zeros((1, tq), F32) for _ in range(hg)))
    ms, ls = lax.fori_loop(0, count_ref[step], kv_step, init)

    w0 = pl.multiple_of(jnp.maximum(t0 - WINDOW, 0), wt)
    span = n_wt * wt
    kw = kw_ref[pl.ds(w0, span), :]
    wi = w0 // wt
    vwt = jnp.concatenate([vwt_ref[wi + r] for r in range(n_wt)], axis=1)
    kpos_w = w0 + lax.broadcasted_iota(jnp.int32, (span, tq), 0)
    dw = tpos - kpos_w
    allow_w = (dw >= 0) & (dw < WINDOW)
    kw_aug = jnp.concatenate([kw, kpos_ref[:span, :]], axis=1)
    sw_all = lax.dot_general(kw_aug, q_all, _NT, preferred_element_type=F32)
    ows, lws = [], []
    for h in range(hg):
        s = jnp.where(allow_w, sw_all[:, h * tq:(h + 1) * tq], NEG)
        m = jnp.max(s, axis=0, keepdims=True)
        p = jnp.exp2(s - m)
        lws.append(jnp.sum(p, axis=0, keepdims=True))
        ows.append(jnp.dot(vwt, p.astype(BF16), preferred_element_type=F32))

    gt = jax.nn.sigmoid(gate_ref[...])
    gtt = gt.T
    ocmp = ocmp_ref[...].astype(F32)
    for h in range(hg):
        o_st = acc_scr[h * dk:(h + 1) * dk, :] / ls[h]
        o_wt = ows[h] / lws[h]
        o_t = gtt[3 * h + 1:3 * h + 2, :] * o_st + gtt[3 * h + 2:3 * h + 3, :] * o_wt
        o = gt[:, 3 * h:3 * h + 1] * ocmp[:, h * dk:(h + 1) * dk] + o_t.T
        o_ref[:, h * dk:(h + 1) * dk] = o.astype(o_ref.dtype)


def _nsa_attend(slopes, slope_parts, tile_list, tile_count, z, q_off, kv_off, vst, vwt, o_cmp, sel, expand, gates, T):
    tq = _pick(T, (256, 128))
    n_exp, tk, ns = expand.shape
    wt = vwt.shape[-1]
    n_wt = (WINDOW + tq) // wt
    kpos = _key_offset_columns(max(tk, n_wt * wt))
    assert T >= n_wt * wt and (WINDOW + tq) % wt == 0 and WINDOW % wt == 0 and tq % wt == 0
    assert q_off % GROUP_Q_WIDTH == 0 and kv_off % HEAD_DIM == 0
    qb = q_off // GROUP_Q_WIDTH
    kvb = kv_off // HEAD_DIM
    G = N_KV_GROUPS

    def k_spec(branch):
        return pl.BlockSpec((T, HEAD_DIM), lambda g, i: (0, kvb + branch * G + g))

    def vt_spec(a):
        return pl.BlockSpec((None,) + a.shape[1:], lambda g, i: (g, 0, 0, 0))

    return pl.pallas_call(
        functools.partial(_nsa_attend_kernel, tq=tq, tk=tk, wt=wt, n_wt=n_wt),
        grid=(G, T // tq),
        in_specs=[
            pl.BlockSpec(memory_space=pltpu.SMEM),
            pl.BlockSpec(memory_space=pltpu.SMEM),
            pl.BlockSpec(memory_space=pltpu.SMEM),
            pl.BlockSpec(memory_space=pltpu.SMEM),
            pl.BlockSpec((tq, GROUP_Q_WIDTH), lambda g, i: (i, qb + g)),
            k_spec(2), vt_spec(vst), k_spec(4), vt_spec(vwt),
            pl.BlockSpec((tq, GROUP_Q_WIDTH), lambda g, i: (i, g)),
            pl.BlockSpec((1, ns, tq), lambda g, i: (g, 0, i)),
            pl.BlockSpec((n_exp, tk, ns), lambda g, i: (0, 0, 0)),
            pl.BlockSpec(kpos.shape, lambda g, i: (0, 0)),
            pl.BlockSpec((tq, GATE_LANES), lambda g, i: (i, g)),
        ],
        out_specs=pl.BlockSpec((tq, GROUP_Q_WIDTH), lambda g, i: (i, g)),
        out_shape=jax.ShapeDtypeStruct((T, Q_WIDTH), BF16),
        scratch_shapes=[
            pltpu.VMEM((HEADS_PER_GROUP * tq, 2 * HEAD_DIM), BF16),
            pltpu.VMEM((HEADS_PER_GROUP * HEAD_DIM, tq), F32),
        ],
        compiler_params=_params("parallel", "arbitrary"),
        name="nsa_attend",
    )(slopes, slope_parts, tile_list, tile_count, z, z, vst, z, vwt, o_cmp, sel, expand, kpos, gates)


_CAND_PIECES = [(0, 16), (1, 8), (2, 8), (3, 8), (4, 8), (5, 8), (6, 8), (7, 8)]


def _peer_select_head(q, k1, k2, exact):
    K = PEER_TOPK
    s1 = lax.dot_general(k1, q[:, :PEER_HALF], _NT, preferred_element_type=F32)
    s2 = lax.dot_general(k2, q[:, PEER_HALF:], _NT, preferred_element_type=F32)
    rank1, v1 = _extract_topk(s1, K, want_vals=True, exact=exact)
    rank2, v2 = _extract_topk(s2, K, want_vals=True, exact=exact)
    v1s = jnp.concatenate(v1, axis=0)
    v2s = jnp.concatenate(v2, axis=0)
    pieces = [v1[r1] + v2s[:L] for r1, L in _CAND_PIECES]
    pieces.append(v1s[8:] + v2[0])
    cand = jnp.concatenate(pieces, axis=0)
    crank = _extract_topk(cand, K, exact=exact)
    chosen = crank < float(K)
    top = v1[0] + v2[0]
    z = jnp.sum(jnp.where(chosen, jnp.exp(cand - top), 0.0), axis=0, keepdims=True)
    chosen_f = jnp.where(chosen, 1.0, 0.0)
    counts = []
    off = 0
    for _, L in _CAND_PIECES:
        counts.append(jnp.sum(chosen_f[off:off + L], axis=0, keepdims=True))
        off += L
    tail = chosen_f[off:off + 8]
    for r in range(8):
        counts.append(tail[r:r + 1])
    cnt = jnp.zeros_like(s1)
    for r1 in range(K):
        cnt = jnp.where(rank1 == float(r1), counts[r1], cnt)

    def n_ranked(rank):
        return jnp.sum(jnp.where(rank < float(K), 1.0, 0.0), axis=0, keepdims=True)

    tied = (jnp.abs(n_ranked(rank1) - float(K)) + jnp.abs(n_ranked(rank2) - float(K))
            + jnp.abs(n_ranked(crank) - float(K)))
    return (cnt, jnp.exp(s1 - v1[0]) / z, rank2, jnp.exp(s2 - v2[0])), tied


def _peer_select_kernel(q_ref, k1_ref, k2_ref, cnt_ref, f1_ref, rank2_ref, e2_ref, *, heads):
    k1 = k1_ref[...]
    k2 = k2_ref[...]
    out_refs = (cnt_ref, f1_ref, rank2_ref, e2_ref)

    def run(exact):
        tied = None
        for hh in range(heads):
            q = q_ref[:, hh * PEER_QDIM:(hh + 1) * PEER_QDIM].astype(BF16)
            outs, t = _peer_select_head(q, k1, k2, exact)
            for ref, val in zip(out_refs, outs):
                ref[hh, 0] = val
            tied = t if tied is None else tied + t
        return jnp.max(tied)

    any_tie = run(False)

    @pl.when(any_tie > 0.0)
    def _():
        run(True)


def _peer_select(qp, keys1, keys2):
    T = qp.shape[0]
    tp = _pick(T, (128,))
    heads = 4
    shp = jax.ShapeDtypeStruct((PEER_HEADS, T // tp, PEER_NKEYS, tp), F32)
    ospec = pl.BlockSpec((heads, 1, PEER_NKEYS, tp), lambda i, h: (h, i, 0, 0))
    return pl.pallas_call(
        functools.partial(_peer_select_kernel, heads=heads),
        grid=(T // tp, PEER_HEADS // heads),
        in_specs=[
            pl.BlockSpec((tp, heads * PEER_QDIM), lambda i, h: (i, h)),
            pl.BlockSpec((PEER_NKEYS, PEER_HALF), lambda i, h: (0, 0)),
            pl.BlockSpec((PEER_NKEYS, PEER_HALF), lambda i, h: (0, 0)),
        ],
        out_specs=[ospec, ospec, ospec, ospec],
        out_shape=[shp, shp, shp, shp],
        compiler_params=_params("parallel", "parallel"),
        name="peer_select",
    )(qp, keys1, keys2)


def _peer_dense_kernel(h_ref, u_ref, v_ref, cnt_ref, f1_ref, rank2_ref, e2_ref, o_ref,
                       w_scr, at_scr, ht_scr, *, ipc, tm, out_chunk):
    c = pl.program_id(1)
    nc = pl.num_programs(1)
    nk = PEER_NKEYS
    lt = LANES
    D = o_ref.shape[1]

    def build_gates(chunk, k, tb, tie):
        i1 = chunk * ipc + k
        lanes = slice(tb * lt, (tb + 1) * lt)
        w = jnp.zeros((nk, lt), F32)
        for h in range(PEER_HEADS):
            crow = cnt_ref[h, tb, pl.ds(i1, 1), :]
            if h == 0:
                crow = crow + (tie if tie.shape[1] == lt else tie[:, lanes])
            frow = f1_ref[h, tb, pl.ds(i1, 1), :]
            w = w + jnp.where(rank2_ref[h, tb] < crow, e2_ref[h, tb] * frow, 0.0)
        w_scr[k * nk:(k + 1) * nk, lanes] = w

    tiles = [(k, tb) for k in range(ipc) for tb in range(tm // lt)]
    early, late = tiles[:len(tiles) // 2], tiles[len(tiles) // 2:]
    zero_row = jnp.zeros((1, tm), F32)

    @pl.when(c == 0)
    def _():
        o_ref[...] = jnp.zeros_like(o_ref)
        for d0 in range(0, D, out_chunk):
            ht_scr[d0:d0 + out_chunk, :] = h_ref[:, d0:d0 + out_chunk].astype(F32).T.astype(BF16)
        for k, tb in early:
            build_gates(c, k, tb, zero_row)

    half = (ipc * nk) // 2
    ht = ht_scr[...]
    at_scr[:half, :] = jnp.dot(u_ref[:half, :], ht, preferred_element_type=F32)
    tie = at_scr[0:1, :] * 0.0
    at_scr[half:, :] = jnp.dot(u_ref[half:, :], ht, preferred_element_type=F32)
    for k, tb in late:
        build_gates(c, k, tb, tie)
    coef = (w_scr[...] * jax.nn.gelu(at_scr[...])).T.astype(BF16)
    nxt = jnp.minimum(c + 1, nc - 1)
    n_out = D // out_chunk
    for j in range(n_out):
        d0 = j * out_chunk
        o_ref[:, d0:d0 + out_chunk] += jnp.dot(coef, v_ref[:, d0:d0 + out_chunk], preferred_element_type=F32)
        tie2 = o_ref[0:1, d0:d0 + lt] * 0.0
        for k, tb in early[j::n_out]:
            build_gates(nxt, k, tb, tie2)


def _peer_dense(h2, u, v, cnt, f1, rank2, e2):
    T, D = h2.shape
    NE = u.shape[0]
    tm = _pick(T, (512, 256, 128))
    ipc = PEER_CHUNK_KEYS
    ec = ipc * PEER_NKEYS
    once = pl.Buffered(1)
    sel_spec = pl.BlockSpec((PEER_HEADS, tm // LANES, PEER_NKEYS, LANES), lambda i, c: (0, i, 0, 0),
                            pipeline_mode=once)
    return pl.pallas_call(
        functools.partial(_peer_dense_kernel, ipc=ipc, tm=tm, out_chunk=_pick(D, (512, 256, 128))),
        grid=(T // tm, NE // ec),
        in_specs=[
            pl.BlockSpec((tm, D), lambda i, c: (i, 0), pipeline_mode=once),
            pl.BlockSpec((ec, D), lambda i, c: (c, 0)),
            pl.BlockSpec((ec, D), lambda i, c: (c, 0)),
            sel_spec, sel_spec, sel_spec, sel_spec,
        ],
        out_specs=pl.BlockSpec((tm, D), lambda i, c: (i, 0), pipeline_mode=once),
        out_shape=jax.ShapeDtypeStruct((T, D), F32),
        scratch_shapes=[pltpu.VMEM((ec, tm), F32), pltpu.VMEM((ec, tm), F32), pltpu.VMEM((D, tm), BF16)],
        compiler_params=_params("parallel", "arbitrary"),
        name="peer_dense",
    )(h2, u, v, cnt, f1, rank2, e2)


def _ple_final_kernel(x1_ref, peer_ref, gple_ref, wg_ref, p_ref, wp_ref, gfin_ref, o_ref, acc_scr, ss_scr,
                      *, tk, nk, tn):
    j = pl.program_id(1)
    D = acc_scr.shape[1]
    x2 = x1_ref[...] + peer_ref[...]
    lhs = (x2 * gple_ref[...]).astype(BF16)
    ssq = jnp.sum(x2 * x2, axis=-1, keepdims=True)

    @pl.when(j == 0)
    def _():
        acc_scr[...] = jnp.zeros_like(acc_scr)
        ss_scr[...] = jnp.zeros_like(ss_scr)

    ss_scr[...] += ssq
    for n0 in range(0, D, tn):
        acc_scr[:, n0:n0 + tn] += jnp.dot(lhs, wg_ref[:, n0:n0 + tn], preferred_element_type=F32)
    for jj in range(nk):
        @pl.when(j == jj)
        def _():
            o_ref[:, jj * tk:(jj + 1) * tk] = x2

    @pl.when(j == nk - 1)
    def _():
        inv = lax.rsqrt(ss_scr[...] / float(D) + RMS_EPS)
        pb = p_ref[...].astype(BF16)
        ss3 = jnp.zeros_like(inv)
        for n0 in range(0, D, tn):
            cols = slice(n0, n0 + tn)
            gate = jax.nn.sigmoid(acc_scr[:, cols] * inv)
            proj = jnp.dot(pb, wp_ref[:, cols], preferred_element_type=F32)
            x3 = o_ref[:, cols] + gate * proj
            o_ref[:, cols] = x3
            ss3 = ss3 + jnp.sum(x3 * x3, axis=-1, keepdims=True)
        inv3 = lax.rsqrt(ss3 / float(D) + RMS_EPS)
        for n0 in range(0, D, tn):
            cols = slice(n0, n0 + tn)
            o_ref[:, cols] = o_ref[:, cols] * inv3 * gfin_ref[:, cols]


def _ple_final(x1, peer, g_ple, w_gate, p, w_proj, g_final):
    T, D = x1.shape
    tm = _pick(T, (512, 256, 128))
    tk = _pick(D, (1024, 512, 256, 128))
    nk = D // tk
    once = pl.Buffered(1)
    return pl.pallas_call(
        functools.partial(_ple_final_kernel, tk=tk, nk=nk, tn=tk),
        grid=(T // tm, nk),
        in_specs=[
            pl.BlockSpec((tm, tk), lambda i, j: (i, j)),
            pl.BlockSpec((tm, tk), lambda i, j: (i, j)),
            pl.BlockSpec((1, tk), lambda i, j: (0, j)),
            pl.BlockSpec((tk, D), lambda i, j: (j, 0)),
            pl.BlockSpec((tm, PLE_DIM), lambda i, j: (i, 0)),
            pl.BlockSpec((PLE_DIM, D), lambda i, j: (0, 0), pipeline_mode=once),
            pl.BlockSpec((1, D), lambda i, j: (0, 0)),
        ],
        out_specs=pl.BlockSpec((tm, D), lambda i, j: (i, 0), pipeline_mode=once),
        out_shape=jax.ShapeDtypeStruct((T, D), F32),
        scratch_shapes=[pltpu.VMEM((tm, D), F32), pltpu.VMEM((tm, 1), F32)],
        compiler_params=_params("parallel", "arbitrary"),
        name="ple_final",
    )(x1, peer, g_ple.reshape(1, D), w_gate, p, w_proj, g_final.reshape(1, D))


def _alibi_slopes():
    return jnp.asarray(2.0 ** (-8.0 * np.arange(1, N_HEADS + 1) / N_HEADS), dtype=F32)


def _alibi_slope_parts():
    s = (2.0 ** (-8.0 * np.arange(1, N_HEADS + 1) / N_HEADS) * LOG2E).astype(np.float32)
    parts = []
    for _ in range(3):
        hi = s.astype(BF16).astype(np.float32)
        parts.append(hi)
        s = (s - hi).astype(np.float32)
    return jnp.asarray(np.stack(parts, axis=1).reshape(-1), dtype=F32)


def _key_offset_columns(n, unit=1, split=16):
    r = np.arange(n)
    hi, lo = unit * split * (r // split), unit * (r % split)
    for part in (hi, lo):
        assert np.array_equal(part.astype(np.float32).astype(BF16).astype(np.float32), part)
    cols = np.zeros((n, HEAD_DIM), np.float32)
    cols[:, 0:3] = hi[:, None]
    cols[:, 3:6] = lo[:, None]
    return jnp.asarray(cols, dtype=BF16)


def _overlap_t(T):
    ncp = T // CMP_STRIDE
    n_cmp = (T - CMP_BLOCK) // CMP_STRIDE + 1
    n_sel = T // SLC_BLOCK
    c_start = np.arange(ncp) * CMP_STRIDE
    s_start = np.arange(n_sel) * SLC_BLOCK
    ov = (c_start[None, :] < s_start[:, None] + SLC_BLOCK) & (c_start[None, :] + CMP_BLOCK > s_start[:, None])
    ov = ov & (np.arange(ncp)[None, :] < n_cmp)
    return jnp.asarray(ov.astype(np.float32), dtype=BF16)


def _expand_blocks(T, tk):
    n_sel = T // SLC_BLOCK
    key_blk = (np.arange(T) // SLC_BLOCK).reshape(T // tk, tk, 1)
    e = key_blk == np.arange(n_sel).reshape(1, 1, n_sel)
    return jnp.asarray(e.astype(np.float32), dtype=BF16)


def _tile_lists(used, n_tiles):
    G, nq, _, ns = used.shape
    flags = used.reshape(G, nq, n_tiles, ns // n_tiles).max(axis=-1) > 0.0
    order = jnp.argsort(jnp.where(flags, 0, 1), axis=-1, stable=True)
    return order.astype(jnp.int32).reshape(-1), flags.sum(axis=-1).astype(jnp.int32).reshape(-1)


def _layer(x, p, norm_mix_g, w_in, pool_w, pool_scale, cmp_pos_k, cmp_pos_v, cmp_w_k, cmp_w_v,
           w_up_pool, w_up_nsa, w_out, norm_ffn_g, peer_w_q, peer_keys1, peer_keys2, peer_u, peer_v,
           norm_ple_g, ple_w_gate, ple_w_proj, norm_final_g):
    T, D = x.shape
    G = N_KV_GROUPS
    q_off = POOL_WIDTH
    kv_off = q_off + Q_WIDTH
    gates_off = kv_off + 6 * KV_WIDTH
    gp_src = gates_off + 3 * N_HEADS

    w_nk = jnp.swapaxes(w_in, 0, 1)
    w_all = w_nk.astype(BF16)
    w_merge_gates = w_all[gp_src:]
    w_g = w_nk[gates_off:gp_src].reshape(G, 3 * HEADS_PER_GROUP, D)
    w_g = jnp.pad(w_g, ((0, 0), (0, GATE_LANES - 3 * HEADS_PER_GROUP), (0, 0))).reshape(G * GATE_LANES, D)
    w_g = w_g.astype(BF16)

    h = _rmsnorm(x, norm_mix_g)
    z = _mm(h, w_all, BF16, b_is_nk=True, n_out=gates_off, name="in_proj")
    zg = _mm(h, w_merge_gates, BF16, b_is_nk=True, name="merge_gate_proj")
    gates = _mm(h, w_g, F32, b_is_nk=True, name="gate_proj")

    ncp = T // CMP_STRIDE
    n_cmp = (T - CMP_BLOCK) // CMP_STRIDE + 1

    def blocks16(col):
        a = z[:, col:col + KV_WIDTH].reshape(ncp, CMP_STRIDE, G, HEAD_DIM)
        return a.transpose(2, 0, 1, 3).reshape(G, ncp, CMP_STRIDE * HEAD_DIM)

    kc = _compress(blocks16(kv_off), cmp_w_k.astype(BF16), cmp_pos_k.reshape(1, -1).astype(BF16), n_cmp, False)
    vct = _compress(blocks16(kv_off + KV_WIDTH), cmp_w_v.astype(BF16), cmp_pos_v.reshape(1, -1).astype(BF16),
                    n_cmp, True)
    slopes = _alibi_slopes()
    slope_parts = _alibi_slope_parts()
    o_cmp, sel, used = _cmp_select(slope_parts, z, q_off, kc, vct, _overlap_t(T), T)
    tk = _pick(T, (512, 256))
    tile_list, tile_count = _tile_lists(used, T // tk)

    def value_tiles(branch, tile):
        col = kv_off + branch * KV_WIDTH
        a = z[:, col:col + KV_WIDTH].reshape(T // tile, tile, G, HEAD_DIM)
        return a.transpose(2, 0, 3, 1)

    nsa_out = _nsa_attend(slopes, slope_parts, tile_list, tile_count, z, q_off, kv_off, value_tiles(3, tk),
                          value_tiles(5, NSA_WINDOW_TILE), o_cmp, sel, _expand_blocks(T, tk), gates, T)

    merged = _pool_merge(z, zg, pool_w.astype(BF16), pool_scale, w_up_pool.astype(BF16),
                         nsa_out, w_up_nsa.astype(BF16), D)
    x1 = _mm(merged, w_out.astype(BF16), F32, res=x, name="out_proj")

    h2 = _rmsnorm(x1, norm_ffn_g)
    qp = _mm(h2, peer_w_q.astype(BF16), F32, name="peer_q")
    cnt, f1, rank2, e2 = _peer_select(qp, peer_keys1.astype(BF16), peer_keys2.astype(BF16))
    peer = _peer_dense(h2, peer_u.astype(BF16), peer_v.astype(BF16), cnt, f1, rank2, e2)

    return _ple_final(x1, peer, norm_ple_g, ple_w_gate.astype(BF16), p, ple_w_proj.astype(BF16), norm_final_g)


def kernel(x, p, norm_mix_g, w_in, pool_w, pool_scale, cmp_pos_k, cmp_pos_v, cmp_w_k, cmp_w_v, w_up_pool, w_up_nsa, w_out, norm_ffn_g, peer_w_q, peer_keys1, peer_keys2, peer_u, peer_v, norm_ple_g, ple_w_gate, ple_w_proj, norm_final_g):
    B, T, D = x.shape
    depth = w_in.shape[0]
    assert depth == 1, "the final rmsnorm is fused into the layer's last kernel"
    outs = []
    for b in range(B):
        outs.append(_layer(
            x[b], p[0, b], norm_mix_g[0], w_in[0], pool_w[0], pool_scale[0], cmp_pos_k[0], cmp_pos_v[0],
            cmp_w_k[0], cmp_w_v[0], w_up_pool[0], w_up_nsa[0], w_out[0], norm_ffn_g[0], peer_w_q[0],
            peer_keys1[0], peer_keys2[0], peer_u[0], peer_v[0], norm_ple_g[0], ple_w_gate[0], ple_w_proj[0],
            norm_final_g))
    return outs[0][None] if B == 1 else jnp.stack(outs, axis=0)
```
